```python
import jax, jax.numpy as jnp
from jax import lax
import numpy as np

D_MODEL = 1024
BATCH = 1
SEQ = 16384
DEPTH = 2

N_EVEN = (DEPTH + 1) // 2
N_ODD = DEPTH // 2
EPS = 1e-6
ROPE_THETA = 500000.0
Q_BLOCK = 128

MOBA_HEADS = 8
MOBA_HEAD_DIM = 64
MOBA_ROT = MOBA_HEAD_DIM // 4
MOBA_BLOCK = 256
MOBA_TOPK = 3
RET_HEADS = 8
RET_DK = 64
RET_DV = 128
RET_CHUNK = 128
RET_THETA = 10000.0
MOBA_W = MOBA_HEADS * MOBA_HEAD_DIM
RET_QK_W = RET_HEADS * RET_DK
RET_V_W = RET_HEADS * RET_DV
EVEN_IN = 3 * MOBA_W + 2 * RET_QK_W + 2 * RET_V_W
EVEN_OUT = MOBA_W + RET_V_W
MLA_HEADS = 16
MLA_NOPE = 64
MLA_ROPE = 32
MLA_V = 64
MLA_Q_RANK = 512
MLA_KV_RANK = 256
MLA_DOWN = MLA_Q_RANK + MLA_KV_RANK + MLA_ROPE
PEER_HEADS = 8
PEER_NKEYS = 128
PEER_EXPERTS = PEER_NKEYS * PEER_NKEYS
PEER_DKEY = 128
PEER_TOPK = 16
PEER_CHUNK = 128

kernel_name = "hybrid_moba_retnet_mla_peer"


def rmsnorm(x, g):
    xf = x.astype(jnp.float32)
    y = xf * lax.rsqrt(jnp.mean(xf * xf, axis=-1, keepdims=True) + EPS)
    return (y * g).astype(x.dtype)


def rope_angles(S, rot_dim, theta):
    inv = 1.0 / (theta ** (jnp.arange(0, rot_dim, 2, dtype=jnp.float32) / rot_dim))
    ang = jnp.arange(S, dtype=jnp.float32)[:, None] * inv[None, :]
    return jnp.cos(ang), jnp.sin(ang)


def apply_rope(x, cos, sin):
    r2 = x.shape[-1] // 2
    x1, x2 = x[..., :r2], x[..., r2:]
    c = cos[None, :, None, :]
    s = sin[None, :, None, :]
    return jnp.concatenate([x1 * c - x2 * s, x2 * c + x1 * s], axis=-1).astype(x.dtype)


def partial_rope(x, cos, sin, rot):
    return jnp.concatenate([apply_rope(x[..., :rot], cos, sin), x[..., rot:]], axis=-1)


def moba_attention(q, k, v):
    B, S, H, D = q.shape
    pad = (-S) % MOBA_BLOCK
    padw = ((0, 0), (0, pad), (0, 0), (0, 0))
    q, k, v = jnp.pad(q, padw), jnp.pad(k, padw), jnp.pad(v, padw)
    Sp = S + pad
    nb = Sp // MOBA_BLOCK
    nq = Sp // Q_BLOCK
    kb = k.reshape(B, nb, MOBA_BLOCK, H, D).transpose(0, 3, 1, 2, 4)
    vb = v.reshape(B, nb, MOBA_BLOCK, H, D).transpose(0, 3, 1, 2, 4)
    kmean = jnp.mean(kb.astype(jnp.float32), axis=3)
    n_sel = min(MOBA_TOPK, nb - 1)
    scale = D ** -0.5
    bi = jnp.arange(B)[:, None, None, None]
    hi = jnp.arange(H)[None, :, None, None]

    def one_block(c):
        q0 = c * Q_BLOCK
        qs = lax.dynamic_slice_in_dim(q, q0, Q_BLOCK, axis=1)
        blk = q0 // MOBA_BLOCK
        qpos = q0 + jnp.arange(Q_BLOCK)
        kpos = blk * MOBA_BLOCK + jnp.arange(MOBA_BLOCK)
        kown = lax.dynamic_index_in_dim(kb, blk, axis=2, keepdims=False)
        vown = lax.dynamic_index_in_dim(vb, blk, axis=2, keepdims=False)
        s_own = jnp.einsum("bqhd,bhkd->bhqk", qs, kown).astype(jnp.float32) * scale
        s_own = jnp.where(kpos[None, :] <= qpos[:, None], s_own, -jnp.inf)
        if n_sel == 0:
            p = jax.nn.softmax(s_own, axis=-1)
            return jnp.einsum("bhqk,bhkd->bqhd", p, vown)
        gate = jnp.einsum("bqhd,bhnd->bhqn", qs.astype(jnp.float32), kmean)
        gate = jnp.where(jnp.arange(nb) < blk, gate, -jnp.inf)
        _, idx = lax.top_k(gate, n_sel)
        ksel = kb[bi, hi, idx]
        vsel = vb[bi, hi, idx]
        s_sel = jnp.einsum("bqhd,bhqnkd->bhqnk", qs, ksel).astype(jnp.float32) * scale
        valid = jnp.arange(n_sel) < blk
        s_sel = jnp.where(valid[:, None], s_sel, -jnp.inf).reshape(B, H, Q_BLOCK, n_sel * MOBA_BLOCK)
        p = jax.nn.softmax(jnp.concatenate([s_sel, s_own], axis=-1), axis=-1)
        p_sel = p[..., : n_sel * MOBA_BLOCK].reshape(B, H, Q_BLOCK, n_sel, MOBA_BLOCK)
        p_own = p[..., n_sel * MOBA_BLOCK:]
        return (jnp.einsum("bhqnk,bhqnkd->bqhd", p_sel, vsel)
                + jnp.einsum("bhqk,bhkd->bqhd", p_own, vown))

    o = lax.map(one_block, jnp.arange(nq))
    return o.transpose(1, 0, 2, 3, 4).reshape(B, Sp, H, D)[:, :S]


def retention_chunkwise(q, k, v):
    B, S, H, dk = q.shape
    dv = v.shape[-1]
    C = RET_CHUNK
    n = S // C
    log_g = jnp.log(1.0 - 2.0 ** (-5.0 - jnp.arange(H, dtype=jnp.float32)))
    pos = jnp.arange(C, dtype=jnp.float32)
    diff = pos[:, None] - pos[None, :]
    dmat = jnp.where(diff >= 0, jnp.exp(log_g[:, None, None] * jnp.maximum(diff, 0.0)), 0.0)
    xi = jnp.exp(log_g[:, None] * (pos + 1.0))[None, :, :, None]
    zeta = jnp.exp(log_g[:, None] * (C - 1.0 - pos))[None, :, :, None]
    g_chunk = jnp.exp(log_g * C)[None, :, None, None]

    def to_chunks(t):
        return t.astype(jnp.float32).reshape(B, n, C, H, t.shape[-1]).transpose(1, 0, 3, 2, 4)

    def body(R, xs):
        qi, ki, vi = xs
        inner = jnp.einsum("bhnd,bhmd->bhnm", qi, ki) * dmat[None]
        out = jnp.einsum("bhnm,bhme->bhne", inner, vi) + jnp.einsum("bhnd,bhde->bhne", qi, R) * xi
        R_new = R * g_chunk + jnp.einsum("bhmd,bhme->bhde", ki * zeta, vi)
        return R_new, out

    R0 = jnp.zeros((B, H, dk, dv), jnp.float32)
    _, o = lax.scan(body, R0, (to_chunks(q), to_chunks(k), to_chunks(v)))
    return o.transpose(1, 0, 3, 2, 4).reshape(B, S, H, dv)


def causal_attention_blocked(q, k, v):
    B, S, H, dq = q.shape
    nq = S // Q_BLOCK
    scale = dq ** -0.5
    kpos = jnp.arange(S)

    def one_block(c):
        q0 = c * Q_BLOCK
        qs = lax.dynamic_slice_in_dim(q, q0, Q_BLOCK, axis=1)
        s = jnp.einsum("bqhd,bkhd->bhqk", qs, k).astype(jnp.float32) * scale
        qpos = q0 + jnp.arange(Q_BLOCK)
        s = jnp.where(kpos[None, :] <= qpos[:, None], s, -jnp.inf)
        p = jax.nn.softmax(s, axis=-1)
        return jnp.einsum("bhqk,bkhd->bqhd", p, v)

    o = lax.map(one_block, jnp.arange(nq))
    return o.transpose(1, 0, 2, 3, 4).reshape(B, S, H, v.shape[-1])


def even_mixer(h, w_in, ret_gn, w_o):
    B, S, _ = h.shape
    proj = h @ w_in
    splits = [MOBA_W, 2 * MOBA_W, 3 * MOBA_W, 3 * MOBA_W + RET_QK_W,
              3 * MOBA_W + 2 * RET_QK_W, 3 * MOBA_W + 2 * RET_QK_W + RET_V_W]
    mq, mk, mv, rq, rk, rv, rg = jnp.split(proj, splits, axis=-1)
    cos_m, sin_m = rope_angles(S, MOBA_ROT, ROPE_THETA)
    mq = partial_rope(mq.reshape(B, S, MOBA_HEADS, MOBA_HEAD_DIM), cos_m, sin_m, MOBA_ROT)
    mk = partial_rope(mk.reshape(B, S, MOBA_HEADS, MOBA_HEAD_DIM), cos_m, sin_m, MOBA_ROT)
    mv = mv.reshape(B, S, MOBA_HEADS, MOBA_HEAD_DIM)
    a_out = moba_attention(mq, mk, mv).reshape(B, S, MOBA_W)
    cos_r, sin_r = rope_angles(S, RET_DK, RET_THETA)
    rq = apply_rope(rq.reshape(B, S, RET_HEADS, RET_DK), cos_r, sin_r)
    rk = apply_rope(rk.reshape(B, S, RET_HEADS, RET_DK), cos_r, sin_r) * (RET_DK ** -0.5)
    rv = rv.reshape(B, S, RET_HEADS, RET_DV)
    r = retention_chunkwise(rq, rk, rv)
    mu = jnp.mean(r, axis=-1, keepdims=True)
    var = jnp.mean(jnp.square(r - mu), axis=-1, keepdims=True)
    r = ((r - mu) * lax.rsqrt(var + EPS)) * ret_gn.reshape(RET_HEADS, RET_DV)
    b_out = r.reshape(B, S, RET_V_W) * jax.nn.silu(rg.astype(jnp.float32))
    cat = jnp.concatenate([a_out.astype(jnp.float32), b_out], axis=-1).astype(h.dtype)
    return cat @ w_o


def odd_mixer(h, w_down, q_norm, w_uq, kv_norm, w_ukv, w_o):
    B, S, _ = h.shape
    down = h @ w_down
    cq, ckv, kr = jnp.split(down, [MLA_Q_RANK, MLA_Q_RANK + MLA_KV_RANK], axis=-1)
    cq = rmsnorm(cq, q_norm)
    ckv = rmsnorm(ckv, kv_norm)
    q = (cq @ w_uq).reshape(B, S, MLA_HEADS, MLA_NOPE + MLA_ROPE)
    cos, sin = rope_angles(S, MLA_ROPE, ROPE_THETA)
    q_rope = apply_rope(q[..., MLA_NOPE:], cos, sin)
    k_rope = apply_rope(kr[:, :, None, :], cos, sin)
    kv = (ckv @ w_ukv).reshape(B, S, MLA_HEADS, MLA_NOPE + MLA_V)
    k_nope, v = kv[..., :MLA_NOPE], kv[..., MLA_NOPE:]
    qf = jnp.concatenate([q[..., :MLA_NOPE], q_rope], axis=-1)
    kf = jnp.concatenate([k_nope, jnp.broadcast_to(k_rope, (B, S, MLA_HEADS, MLA_ROPE)).astype(k_nope.dtype)], axis=-1)
    o = causal_attention_blocked(qf, kf, v)
    return o.reshape(B, S, MLA_HEADS * MLA_V).astype(h.dtype) @ w_o


def peer_ffn(x, w_q, sub_keys, u_tab, v_tab):
    B, S, D = x.shape
    T = B * S
    KK = PEER_TOPK * PEER_TOPK
    xt = x.reshape(T, D)
    qry = (xt @ w_q).reshape(T, PEER_HEADS, 2, PEER_DKEY // 2).astype(jnp.float32)
    s = jnp.einsum("thpd,hpnd->thpn", qry, sub_keys.astype(jnp.float32))
    s1, i1 = lax.top_k(s[:, :, 0], PEER_TOPK)
    s2, i2 = lax.top_k(s[:, :, 1], PEER_TOPK)
    cand = (s1[..., :, None] + s2[..., None, :]).reshape(T, PEER_HEADS, KK)
    cidx = (i1[..., :, None] * PEER_NKEYS + i2[..., None, :]).reshape(T, PEER_HEADS, KK)
    top_s, pos = lax.top_k(cand, PEER_TOPK)
    eidx = jnp.take_along_axis(cidx, pos, axis=-1)
    gate = jax.nn.softmax(top_s, axis=-1)
    nc = T // PEER_CHUNK

    def chunk(args):
        xc, ec, gc = args
        ue = u_tab[ec]
        ve = v_tab[ec]
        a = jax.nn.gelu(jnp.einsum("td,thkd->thk", xc, ue), approximate=False) * gc
        return jnp.einsum("thk,thkd->td", a, ve)

    out = lax.map(chunk, (xt.reshape(nc, PEER_CHUNK, D),
                          eidx.reshape(nc, PEER_CHUNK, PEER_HEADS, PEER_TOPK),
                          gate.reshape(nc, PEER_CHUNK, PEER_HEADS, PEER_TOPK)))
    return out.reshape(B, S, D).astype(x.dtype)


def setup_inputs(seed: int = 0) -> dict:
    key = jax.random.key(seed)
    ks = jax.random.split(key, 17)
    f32 = jnp.float32

    def nrm(k, shape, scale):
        return jax.random.normal(k, shape, f32) * scale

    def gain(k, shape):
        return 1.0 + 0.02 * jax.random.normal(k, shape, f32)

    return {
        "x": nrm(ks[0], (BATCH, SEQ, D_MODEL), 1.0),
        "attn_norm": gain(ks[1], (DEPTH, D_MODEL)),
        "ffn_norm": gain(ks[2], (DEPTH, D_MODEL)),
        "ev_w_in": nrm(ks[3], (N_EVEN, D_MODEL, EVEN_IN), D_MODEL ** -0.5),
        "ev_ret_gn": gain(ks[4], (N_EVEN, RET_V_W)),
        "ev_w_o": nrm(ks[5], (N_EVEN, EVEN_OUT, D_MODEL), EVEN_OUT ** -0.5),
        "od_w_down": nrm(ks[6], (N_ODD, D_MODEL, MLA_DOWN), D_MODEL ** -0.5),
        "od_q_norm": gain(ks[7], (N_ODD, MLA_Q_RANK)),
        "od_w_uq": nrm(ks[8], (N_ODD, MLA_Q_RANK, MLA_HEADS * (MLA_NOPE + MLA_ROPE)), MLA_Q_RANK ** -0.5),
        "od_kv_norm": gain(ks[9], (N_ODD, MLA_KV_RANK)),
        "od_w_ukv": nrm(ks[10], (N_ODD, MLA_KV_RANK, MLA_HEADS * (MLA_NOPE + MLA_V)), MLA_KV_RANK ** -0.5),
        "od_w_o": nrm(ks[11], (N_ODD, MLA_HEADS * MLA_V, D_MODEL), (MLA_HEADS * MLA_V) ** -0.5),
        "peer_wq": nrm(ks[12], (DEPTH, D_MODEL, PEER_HEADS * PEER_DKEY), D_MODEL ** -0.5),
        "peer_keys": nrm(ks[13], (DEPTH, PEER_HEADS, 2, PEER_NKEYS, PEER_DKEY // 2), (PEER_DKEY // 2) ** -0.5),
        "peer_u": nrm(ks[14], (DEPTH, PEER_EXPERTS, D_MODEL), D_MODEL ** -0.5),
        "peer_v": nrm(ks[15], (DEPTH, PEER_EXPERTS, D_MODEL), PEER_HEADS ** -0.5),
        "final_norm": gain(ks[16], (D_MODEL,)),
    }


def reference(x, attn_norm, ffn_norm, ev_w_in, ev_ret_gn, ev_w_o, od_w_down, od_q_norm, od_w_uq,
              od_kv_norm, od_w_ukv, od_w_o, peer_wq, peer_keys, peer_u, peer_v, final_norm):
    h = x
    for i in range(DEPTH):
        hn = rmsnorm(h, attn_norm[i])
        j = i // 2
        if i % 2 == 0:
            mix = even_mixer(hn, ev_w_in[j], ev_ret_gn[j], ev_w_o[j])
        else:
            mix = odd_mixer(hn, od_w_down[j], od_q_norm[j], od_w_uq[j], od_kv_norm[j], od_w_ukv[j], od_w_o[j])
        h = h + mix.astype(h.dtype)
        h = h + peer_ffn(rmsnorm(h, ffn_norm[i]), peer_wq[i], peer_keys[i], peer_u[i], peer_v[i])
    return rmsnorm(h, final_norm).astype(x.dtype)
```

```python
import functools

import numpy as np
import jax
import jax.numpy as jnp
from jax import lax
from jax.experimental import pallas as pl
from jax.experimental.pallas import tpu as pltpu

F32 = jnp.float32
BF16 = jnp.bfloat16

LANES = 128
VMEM_LIMIT = 56 * 1024 * 1024

D_MODEL = 1024
EPS = 1e-6
ROPE_THETA = 500000.0

MOBA_HEADS = 8
MOBA_HEAD_DIM = 64
MOBA_ROT = MOBA_HEAD_DIM // 4
MOBA_BLOCK = 256
MOBA_TOPK = 3
MASK_BIAS = -1e9

RET_HEADS = 8
RET_DK = 64
RET_DV = 128
RET_THETA = 10000.0
RET_CHUNK = 256

MLA_HEADS = 16
MLA_NOPE = 64
MLA_ROPE = 32
MLA_V = 64
MLA_Q_RANK = 512
MLA_KV_RANK = 256

PEER_HEADS = 8
PEER_NKEYS = 128
PEER_DKEY = 128
PEER_TOPK = 16


def _cparams(*sem):
    return pltpu.CompilerParams(dimension_semantics=sem, vmem_limit_bytes=VMEM_LIMIT)


def _proj_body(*refs, norm, rope, colmean, tn):
    it = iter(refs)
    x_ref = next(it)
    g_ref = next(it) if norm else None
    w_ref = next(it)
    if rope:
        wr_ref, c_ref, s_ref = next(it), next(it), next(it)
    o_ref = next(it)
    cm_ref = next(it) if colmean else None
    xn_ref = next(it)

    @pl.when(pl.program_id(1) == 0)
    def _():
        x = x_ref[...].astype(F32)
        if norm:
            x = x * lax.rsqrt(jnp.mean(x * x, axis=-1, keepdims=True) + EPS) * g_ref[...]
        xn_ref[...] = x.astype(BF16)

    xn = xn_ref[...]
    y = jnp.dot(xn, w_ref[...], preferred_element_type=F32)
    if not rope:
        o_ref[...] = y.astype(o_ref.dtype)
        return
    yr = jnp.dot(xn, wr_ref[...], preferred_element_type=F32)
    c = c_ref[...]
    s = s_ref[...]
    for k in range(tn // LANES):
        sl = slice(k * LANES, (k + 1) * LANES)
        val = y[:, sl] * c + yr[:, sl] * s
        o_ref[:, sl] = val.astype(o_ref.dtype)
        if colmean:
            cm_ref[0, :, sl] = jnp.mean(val, axis=0, keepdims=True)


def _proj(x, w, *, gain=None, w_rot=None, cos=None, sin=None, seg=None,
          colmean=False, out_dtype=BF16, tm=256, tn=512):
    T, K = x.shape
    N = w.shape[1]
    tn = min(tn, N)
    assert T % tm == 0 and N % tn == 0 and tn % LANES == 0
    norm, rope = gain is not None, w_rot is not None
    in_specs = [pl.BlockSpec((tm, K), lambda i, j: (i, 0))]
    args = [x]
    if norm:
        in_specs.append(pl.BlockSpec((1, K), lambda i, j: (0, 0)))
        args.append(gain.reshape(1, K).astype(F32))
    in_specs.append(pl.BlockSpec((K, tn), lambda i, j: (0, j)))
    args.append(w)
    if rope:
        assert seg % tn == 0
        tab = lambda i, j: (i, (j * tn) // seg)
        in_specs += [pl.BlockSpec((K, tn), lambda i, j: (0, j)),
                     pl.BlockSpec((tm, LANES), tab), pl.BlockSpec((tm, LANES), tab)]
        args += [w_rot, cos, sin]
    out_shape = [jax.ShapeDtypeStruct((T, N), out_dtype)]
    out_specs = [pl.BlockSpec((tm, tn), lambda i, j: (i, j))]
    if colmean:
        assert rope
        out_shape.append(jax.ShapeDtypeStruct((T // tm, 1, N), F32))
        out_specs.append(pl.BlockSpec((1, 1, tn), lambda i, j: (i, 0, j)))
    res = pl.pallas_call(
        functools.partial(_proj_body, norm=norm, rope=rope, colmean=colmean, tn=tn),
        grid=(T // tm, N // tn),
        in_specs=in_specs, out_specs=out_specs, out_shape=out_shape,
        scratch_shapes=[pltpu.VMEM((tm, K), BF16)],
        compiler_params=_cparams("parallel", "arbitrary"),
        name="proj",
    )(*args)
    return res if colmean else res[0]


def _matmul_res_body(*refs, n_in):
    xs, ws = refs[:n_in], refs[n_in:2 * n_in]
    r_ref, o_ref = refs[2 * n_in], refs[2 * n_in + 1]
    acc = r_ref[...]
    for x_ref, w_ref in zip(xs, ws):
        acc = acc + jnp.dot(x_ref[...], w_ref[...], preferred_element_type=F32)
    o_ref[...] = acc


def _matmul_res(xs, ws, res, *, tm=256, tn=512):
    T, N = res.shape
    n_in = len(xs)
    in_specs = [pl.BlockSpec((tm, x.shape[1]), lambda i, j: (i, 0)) for x in xs]
    in_specs += [pl.BlockSpec((w.shape[0], tn), lambda i, j: (0, j)) for w in ws]
    in_specs.append(pl.BlockSpec((tm, tn), lambda i, j: (i, j)))
    return pl.pallas_call(
        functools.partial(_matmul_res_body, n_in=n_in),
        grid=(T // tm, N // tn),
        in_specs=in_specs,
        out_specs=pl.BlockSpec((tm, tn), lambda i, j: (i, j)),
        out_shape=jax.ShapeDtypeStruct((T, N), F32),
        compiler_params=_cparams("parallel", "parallel"),
        name="matmul_res",
    )(*xs, *ws, res)


def _moba_gate_body(q_ref, k_ref, km_ref, qo_ref, ko_ref, *, tq, heads):
    blk = pl.program_id(0)
    lane = lax.broadcasted_iota(jnp.int32, (tq, LANES), 1)
    bidx = lane - MOBA_HEAD_DIM
    for h in range(heads):
        sl = slice(h * LANES, (h + 1) * LANES)
        q = q_ref[:, sl].astype(F32)
        gate = lax.dot_general(q, km_ref[h], (((1,), (1,)), ((), ())),
                               precision=lax.Precision.HIGHEST, preferred_element_type=F32)
        g = jnp.where((bidx >= 0) & (bidx < blk), gate, -jnp.inf)
        sel = bidx == blk
        for _ in range(MOBA_TOPK):
            m = jnp.max(g, axis=-1, keepdims=True)
            idx = jnp.min(jnp.where(g == m, lane, 2 * LANES), axis=-1, keepdims=True)
            pick = (lane == idx) & (m > -jnp.inf)
            sel = sel | pick
            g = jnp.where(pick, -jnp.inf, g)
        bias = jnp.where(sel, 0.0, MASK_BIAS)
        qo_ref[:, sl] = jnp.where(bidx < 0, q, bias).astype(BF16)
        ko_ref[:, sl] = jnp.where(bidx == blk, 1.0, k_ref[:, sl].astype(F32)).astype(BF16)


def _moba_gate(q, k, kmean_pad):
    T, W = q.shape
    heads = W // LANES
    tq = MOBA_BLOCK
    assert T % tq == 0 and T // tq <= LANES - MOBA_HEAD_DIM
    blk_spec = pl.BlockSpec((tq, W), lambda i: (i, 0))
    return pl.pallas_call(
        functools.partial(_moba_gate_body, tq=tq, heads=heads),
        grid=(T // tq,),
        in_specs=[blk_spec, blk_spec, pl.BlockSpec((heads, LANES, LANES), lambda i: (0, 0, 0))],
        out_specs=[blk_spec, blk_spec],
        out_shape=[jax.ShapeDtypeStruct((T, W), BF16)] * 2,
        compiler_params=_cparams("parallel"),
        name="moba_gate",
    )(q, k, kmean_pad)


def _flash_body(q_ref, k_ref, v_ref, o_ref, m_ref, l_ref, acc_ref, *, tq, tk, dv):
    i = pl.program_id(1)
    n_full = (i * tq) // tk
    row = i * tq + lax.broadcasted_iota(jnp.int32, (tq, tk), 0)
    col0 = lax.broadcasted_iota(jnp.int32, (tq, tk), 1)
    outs = []
    for hh in range(2):
        hsl = slice(hh * LANES, (hh + 1) * LANES)
        q = q_ref[:, hsl]
        m_ref[...] = jnp.full((tq, LANES), -jnp.inf, F32)
        l_ref[...] = jnp.zeros((tq, LANES), F32)
        acc_ref[...] = jnp.zeros((tq, LANES), F32)

        def step(j, masked):
            start = pl.multiple_of(j * tk, tk)
            k = k_ref[pl.ds(start, tk), hsl]
            v = v_ref[pl.ds(start, tk), :]
            s = lax.dot_general(q, k, (((1,), (1,)), ((), ())), preferred_element_type=F32)
            if masked:
                s = jnp.where(col0 + j * tk <= row, s, -jnp.inf)
            m_prev = m_ref[...]
            m_new = jnp.maximum(m_prev, jnp.max(s, axis=-1, keepdims=True))
            alpha = jnp.exp(m_prev - m_new)
            p = jnp.exp(s - pltpu.repeat(m_new, tk // LANES, axis=1))
            l_ref[...] = alpha * l_ref[...] + jnp.sum(p, axis=-1, keepdims=True)
            acc_ref[...] = alpha * acc_ref[...] + jnp.dot(
                p.astype(BF16), v, preferred_element_type=F32)
            m_ref[...] = m_new

        def loop_body(j, carry):
            step(j, False)
            return carry

        lax.fori_loop(0, n_full, loop_body, 0)
        for d in range(tq // tk):
            step(n_full + d, True)
        outs.append(acc_ref[...] / l_ref[...])
    lane = lax.broadcasted_iota(jnp.int32, (tq, LANES), 1)
    o_ref[...] = jnp.where(lane < dv, outs[0], outs[1]).astype(o_ref.dtype)


def _flash(q, k, v, *, tq=256, tk=256):
    T, W = q.shape
    heads = W // LANES
    dv = v.shape[1] // heads
    assert heads % 2 == 0 and 2 * dv == LANES and T % tq == 0 and tq % tk == 0
    return pl.pallas_call(
        functools.partial(_flash_body, tq=tq, tk=tk, dv=dv),
        grid=(heads // 2, T // tq),
        in_specs=[pl.BlockSpec((tq, 2 * LANES), lambda h, i: (i, h)),
                  pl.BlockSpec((T, 2 * LANES), lambda h, i: (0, h)),
                  pl.BlockSpec((T, LANES), lambda h, i: (0, h))],
        out_specs=pl.BlockSpec((tq, LANES), lambda h, i: (i, h)),
        out_shape=jax.ShapeDtypeStruct((T, heads * dv), BF16),
        scratch_shapes=[pltpu.VMEM((tq, LANES), F32)] * 3,
        compiler_params=_cparams("parallel", "arbitrary"),
        name="flash",
    )(q, k, v)


def _retention_body(q_ref, k_ref, v_ref, g_ref, gn_ref, dm_ref, xi_ref, ze_ref, gc_ref,
                    o_ref, r_ref):
    @pl.when(pl.program_id(1) == 0)
    def _():
        r_ref[...] = jnp.zeros_like(r_ref)

    q = q_ref[...]
    k = k_ref[...]
    v = v_ref[...]
    r_old = r_ref[...]
    inner = lax.dot_general(q, k, (((1,), (1,)), ((), ())), preferred_element_type=F32) * dm_ref[0]
    out = jnp.dot(inner.astype(BF16), v, preferred_element_type=F32)
    out = out + jnp.dot(q, r_old.astype(BF16), preferred_element_type=F32) * xi_ref[0]
    kz = (k.astype(F32) * ze_ref[0]).T.astype(BF16)
    r_ref[...] = r_old * gc_ref[0] + jnp.dot(kz, v, preferred_element_type=F32)
    mu = jnp.mean(out, axis=-1, keepdims=True)
    cen = out - mu
    var = jnp.mean(cen * cen, axis=-1, keepdims=True)
    rn = cen * lax.rsqrt(var + EPS) * gn_ref[...]
    gate = g_ref[...].astype(F32)
    o_ref[...] = (rn * (gate / (1.0 + jnp.exp(-gate)))).astype(o_ref.dtype)


def _retention(q, k, v, g, gn):
    T, W = v.shape
    heads = W // LANES
    C = min(RET_CHUNK, T)
    assert T % C == 0
    log_g = jnp.log(1.0 - 2.0 ** (-5.0 - jnp.arange(heads, dtype=F32)))
    pos = jnp.arange(C, dtype=F32)
    diff = pos[:, None] - pos[None, :]
    dmat = jnp.where(diff >= 0, jnp.exp(log_g[:, None, None] * jnp.maximum(diff, 0.0)), 0.0)
    rep = lambda t: jnp.broadcast_to(t[..., None], t.shape + (LANES,))
    xi = rep(jnp.exp(log_g[:, None] * (pos + 1.0)))
    zeta = rep(jnp.exp(log_g[:, None] * (C - 1.0 - pos)))
    g_chunk = rep(jnp.exp(log_g * C)[:, None])
    tile = pl.BlockSpec((C, LANES), lambda h, c: (c, h))
    head_tab = lambda r: pl.BlockSpec((1, r, LANES), lambda h, c: (h, 0, 0))
    return pl.pallas_call(
        _retention_body,
        grid=(heads, T // C),
        in_specs=[tile, tile, tile, tile,
                  pl.BlockSpec((1, LANES), lambda h, c: (0, h)),
                  pl.BlockSpec((1, C, C), lambda h, c: (h, 0, 0)),
                  head_tab(C), head_tab(C), head_tab(1)],
        out_specs=tile,
        out_shape=jax.ShapeDtypeStruct((T, W), BF16),
        scratch_shapes=[pltpu.VMEM((LANES, LANES), F32)],
        compiler_params=_cparams("parallel", "arbitrary"),
        name="retention",
    )(q, k, v, g, gn.reshape(1, W).astype(F32), dmat, xi, zeta, g_chunk)


def _mla_mid_body(d_ref, qn_ref, kvn_ref, c_ref, s_ref, cq_ref, ckv_ref):
    def rms(x, g):
        return x * lax.rsqrt(jnp.mean(x * x, axis=-1, keepdims=True) + EPS) * g

    cq_ref[...] = rms(d_ref[:, :MLA_Q_RANK], qn_ref[...]).astype(BF16)
    lo = MLA_Q_RANK + MLA_KV_RANK
    ckv_ref[:, :MLA_KV_RANK] = rms(d_ref[:, MLA_Q_RANK:lo], kvn_ref[...]).astype(BF16)
    kr = d_ref[:, lo:lo + LANES] * c_ref[...] + d_ref[:, lo + LANES:lo + 2 * LANES] * s_ref[...]
    ckv_ref[:, MLA_KV_RANK:] = kr.astype(BF16)


def _mla_mid(down, q_norm, kv_norm, cos, sin, *, tm=256):
    T, W = down.shape
    wide = MLA_KV_RANK + LANES
    return pl.pallas_call(
        _mla_mid_body,
        grid=(T // tm,),
        in_specs=[pl.BlockSpec((tm, W), lambda i: (i, 0)),
                  pl.BlockSpec((1, MLA_Q_RANK), lambda i: (0, 0)),
                  pl.BlockSpec((1, MLA_KV_RANK), lambda i: (0, 0)),
                  pl.BlockSpec((tm, LANES), lambda i: (i, 0)),
                  pl.BlockSpec((tm, LANES), lambda i: (i, 0))],
        out_specs=[pl.BlockSpec((tm, MLA_Q_RANK), lambda i: (i, 0)),
                   pl.BlockSpec((tm, wide), lambda i: (i, 0))],
        out_shape=[jax.ShapeDtypeStruct((T, MLA_Q_RANK), BF16),
                   jax.ShapeDtypeStruct((T, wide), BF16)],
        compiler_params=_cparams("parallel"),
        name="mla_mid",
    )(down, q_norm.reshape(1, -1).astype(F32), kv_norm.reshape(1, -1).astype(F32), cos, sin)


def _peer_scores_body(x_ref, g_ref, wq_ref, kt_ref, xn_ref, st_ref):
    x = x_ref[...]
    xn = (x * lax.rsqrt(jnp.mean(x * x, axis=-1, keepdims=True) + EPS) * g_ref[...]).astype(BF16)
    xn_ref[...] = xn
    qry = jnp.dot(xn, wq_ref[...], preferred_element_type=F32).astype(BF16)
    st_ref[...] = lax.dot_general(kt_ref[...], qry, (((1,), (1,)), ((), ())),
                                  preferred_element_type=F32)


def _peer_scores(h, gain, wq, keys_t, *, tm=256):
    T, D = h.shape
    R = keys_t.shape[0]
    return pl.pallas_call(
        _peer_scores_body,
        grid=(T // tm,),
        in_specs=[pl.BlockSpec((tm, D), lambda i: (i, 0)),
                  pl.BlockSpec((1, D), lambda i: (0, 0)),
                  pl.BlockSpec(wq.shape, lambda i: (0, 0)),
                  pl.BlockSpec(keys_t.shape, lambda i: (0, 0))],
        out_specs=[pl.BlockSpec((tm, D), lambda i: (i, 0)),
                   pl.BlockSpec((R, tm), lambda i: (0, i))],
        out_shape=[jax.ShapeDtypeStruct((T, D), BF16), jax.ShapeDtypeStruct((R, T), F32)],
        compiler_params=_cparams("parallel"),
        name="peer_scores",
    )(h, gain.reshape(1, D).astype(F32), wq, keys_t)


PEER_RANKS = PEER_TOPK + 1
PEER_CAND = [(i, PEER_RANKS // (i + 1)) for i in range(PEER_RANKS)]
PEER_NCAND = -(-sum(c for _, c in PEER_CAND) // 8) * 8


def _peer_topk_body(st_ref, pk_ref, t1_ref, t2_ref, cand_ref):
    n = PEER_NKEYS
    for h in range(PEER_HEADS):
        s1 = st_ref[(2 * h) * n:(2 * h + 1) * n, :]
        s2 = st_ref[(2 * h + 1) * n:(2 * h + 2) * n, :]
        for s, t_ref in ((s1, t1_ref), (s2, t2_ref)):
            vals = s
            for r in range(PEER_RANKS):
                m = jnp.max(vals, axis=0, keepdims=True)
                t_ref[r:r + 1, :] = m
                vals = jnp.where(vals == m, -jnp.inf, vals)
        cand_ref[...] = jnp.full(cand_ref.shape, -jnp.inf, F32)
        rowp = 0
        for i, cnt in PEER_CAND:
            cand_ref[rowp:rowp + cnt, :] = t1_ref[i:i + 1, :] + t2_ref[0:cnt, :]
            rowp += cnt
        c = cand_ref[...]
        top1 = t1_ref[0:1, :]
        top2 = t2_ref[0:1, :]
        cmax = top1 + top2
        z = jnp.zeros_like(cmax)
        kth = cmax
        for r in range(PEER_RANKS):
            m = jnp.max(c, axis=0, keepdims=True)
            if r < PEER_TOPK:
                z = z + jnp.exp(m - cmax)
                kth = m
            else:
                tau = 0.5 * (kth + m)
            c = jnp.where(c == m, -jnp.inf, c)
        pk_ref[h, 0] = jnp.exp(s1 - top1) / z
        pk_ref[h, 1] = tau - s1
        pk_ref[h, 2] = s2
        pk_ref[h, 3] = jnp.exp(s2 - top2)


def _peer_topk(st, *, tm=256):
    R, T = st.shape
    return pl.pallas_call(
        _peer_topk_body,
        grid=(T // tm,),
        in_specs=[pl.BlockSpec((R, tm), lambda i: (0, i))],
        out_specs=pl.BlockSpec((PEER_HEADS, 4, PEER_NKEYS, tm), lambda i: (0, 0, 0, i)),
        out_shape=jax.ShapeDtypeStruct((PEER_HEADS, 4, PEER_NKEYS, T), F32),
        scratch_shapes=[pltpu.VMEM((PEER_RANKS + 7, tm), F32),
                        pltpu.VMEM((PEER_RANKS + 7, tm), F32),
                        pltpu.VMEM((PEER_NCAND, tm), F32)],
        compiler_params=_cparams("parallel"),
        name="peer_topk",
    )(st)


def _peer_dense_body(xn_ref, u_ref, vt_ref, pk_ref, h_ref, o_ref, at_ref, acc_ref, *, tm, te):
    e = pl.program_id(1)

    @pl.when(e == 0)
    def _():
        acc_ref[...] = jnp.zeros_like(acc_ref)

    ht = lax.dot_general(u_ref[...], xn_ref[...], (((1,), (1,)), ((), ())),
                         preferred_element_type=F32)
    n = PEER_NKEYS
    for ab in range(te // n):
        a = e * (te // n) + ab
        gsum = jnp.zeros((n, tm), F32)
        for h in range(PEER_HEADS):
            w1 = pk_ref[h, 0, pl.ds(a, 1), :]
            thr = pk_ref[h, 1, pl.ds(a, 1), :]
            gsum = gsum + jnp.where(pk_ref[h, 2] > thr, pk_ref[h, 3], 0.0) * w1
        hs = ht[ab * n:(ab + 1) * n, :]
        act = 0.5 * hs * (1.0 + lax.erf(hs * np.float32(1.0 / np.sqrt(2.0))))
        at_ref[ab * n:(ab + 1) * n, :] = (act * gsum).astype(BF16)
    acc_ref[...] += jnp.dot(vt_ref[...], at_ref[...], preferred_element_type=F32)

    @pl.when(e == pl.num_programs(1) - 1)
    def _():
        o_ref[...] = h_ref[...] + acc_ref[...].T


def _peer_dense(xn, u, vt, pk, h, *, tm=512, te=512):
    T, D = xn.shape
    E = u.shape[0]
    tm = min(tm, T)
    assert T % tm == 0 and E % te == 0 and te % PEER_NKEYS == 0
    return pl.pallas_call(
        functools.partial(_peer_dense_body, tm=tm, te=te),
        grid=(T // tm, E // te),
        in_specs=[pl.BlockSpec((tm, D), lambda i, e: (i, 0)),
                  pl.BlockSpec((te, D), lambda i, e: (e, 0)),
                  pl.BlockSpec((D, te), lambda i, e: (0, e)),
                  pl.BlockSpec((PEER_HEADS, 4, PEER_NKEYS, tm), lambda i, e: (0, 0, 0, i)),
                  pl.BlockSpec((tm, D), lambda i, e: (i, 0))],
        out_specs=pl.BlockSpec((tm, D), lambda i, e: (i, 0)),
        out_shape=jax.ShapeDtypeStruct((T, D), F32),
        scratch_shapes=[pltpu.VMEM((te, tm), BF16), pltpu.VMEM((D, tm), F32)],
        compiler_params=_cparams("parallel", "arbitrary"),
        name="peer_dense",
    )(xn, u, vt, pk, h)


def _peer_ffn(h, gain, wq, keys, u_tab, v_tab):
    nk, dh = PEER_NKEYS, PEER_DKEY // 2
    groups = PEER_HEADS * 2
    keys_t = jnp.einsum("gnd,gk->gnkd", keys.reshape(groups, nk, dh).astype(F32),
                        jnp.eye(groups, dtype=F32)).reshape(groups * nk, groups * dh).astype(BF16)
    xn, st = _peer_scores(h, gain, wq.astype(BF16), keys_t)
    pk = _peer_topk(st)
    return _peer_dense(xn, u_tab.astype(BF16), v_tab.astype(BF16).T, pk, h)


def _rmsnorm_body(x_ref, g_ref, o_ref):
    x = x_ref[...]
    o_ref[...] = x * lax.rsqrt(jnp.mean(x * x, axis=-1, keepdims=True) + EPS) * g_ref[...]


def _rmsnorm(x, g, *, tm=512):
    T, D = x.shape
    tm = min(tm, T)
    return pl.pallas_call(
        _rmsnorm_body,
        grid=(T // tm,),
        in_specs=[pl.BlockSpec((tm, D), lambda i: (i, 0)), pl.BlockSpec((1, D), lambda i: (0, 0))],
        out_specs=pl.BlockSpec((tm, D), lambda i: (i, 0)),
        out_shape=jax.ShapeDtypeStruct((T, D), F32),
        compiler_params=_cparams("parallel"),
        name="rmsnorm",
    )(x, g.reshape(1, D).astype(F32))


def _rope_cs(T, rot_dim, theta):
    inv = 1.0 / (theta ** (jnp.arange(0, rot_dim, 2, dtype=F32) / rot_dim))
    ang = jnp.arange(T, dtype=F32)[:, None] * inv[None, :]
    return jnp.cos(ang), jnp.sin(ang)


def _lane_tables(T, cos, sin, *, rot_at, keep, scale):
    r = cos.shape[1]
    c = jnp.zeros((T, LANES), F32).at[:, :keep].set(1.0)
    c = c.at[:, rot_at:rot_at + r].set(cos).at[:, rot_at + r:rot_at + 2 * r].set(cos)
    s = jnp.zeros((T, LANES), F32)
    s = s.at[:, rot_at:rot_at + r].set(sin).at[:, rot_at + r:rot_at + 2 * r].set(sin)
    return c * scale, s * scale


def _head_cols(n_heads, src_stride, src_off, width, *, dst_stride=LANES, dst_off=0):
    idx = np.zeros(n_heads * dst_stride, np.int32)
    sgn = np.zeros(n_heads * dst_stride, np.float32)
    for h in range(n_heads):
        d = h * dst_stride + dst_off
        idx[d:d + width] = h * src_stride + src_off + np.arange(width)
        sgn[d:d + width] = 1.0
    return idx, sgn


def _rot_cols(n_heads, src_stride, src_off, r, *, dst_stride=LANES, dst_off=0):
    idx = np.zeros(n_heads * dst_stride, np.int32)
    sgn = np.zeros(n_heads * dst_stride, np.float32)
    for h in range(n_heads):
        d = h * dst_stride + dst_off
        s = h * src_stride + src_off
        idx[d:d + r] = s + r + np.arange(r)
        sgn[d:d + r] = -1.0
        idx[d + r:d + 2 * r] = s + np.arange(r)
        sgn[d + r:d + 2 * r] = 1.0
    return idx, sgn


def _take_cols(w, idx_sgn):
    idx, sgn = idx_sgn
    return (jnp.take(w, jnp.asarray(idx), axis=1) * jnp.asarray(sgn)[None, :]).astype(BF16)


def _even_mixer(h, norm_g, w_in, ret_gn, w_o):
    T = h.shape[0]
    mw = MOBA_HEADS * MOBA_HEAD_DIM
    rw = RET_HEADS * RET_DK
    vw = RET_HEADS * RET_DV
    o_mq, o_mk, o_mv, o_rq, o_rk, o_rv, o_rg = np.cumsum([0, mw, mw, mw, rw, rw, vw])
    hm, hr = MOBA_ROT // 2, RET_DK // 2

    def seg(off, heads, stride, r):
        w = w_in[:, off:off + heads * stride]
        return (_take_cols(w, _head_cols(heads, stride, 0, stride)),
                _take_cols(w, _rot_cols(heads, stride, 0, r)))

    segs = [seg(o_mq, MOBA_HEADS, MOBA_HEAD_DIM, hm), seg(o_mk, MOBA_HEADS, MOBA_HEAD_DIM, hm),
            seg(o_rq, RET_HEADS, RET_DK, hr), seg(o_rk, RET_HEADS, RET_DK, hr)]
    w_lin = jnp.concatenate([s[0] for s in segs], axis=1)
    w_rot = jnp.concatenate([s[1] for s in segs], axis=1)
    cm, sm = _rope_cs(T, MOBA_ROT, ROPE_THETA)
    cr, sr = _rope_cs(T, RET_DK, RET_THETA)
    tabs = [_lane_tables(T, cm, sm, rot_at=0, keep=MOBA_HEAD_DIM, scale=MOBA_HEAD_DIM ** -0.5),
            _lane_tables(T, cm, sm, rot_at=0, keep=MOBA_HEAD_DIM, scale=1.0),
            _lane_tables(T, cr, sr, rot_at=0, keep=RET_DK, scale=1.0),
            _lane_tables(T, cr, sr, rot_at=0, keep=RET_DK, scale=RET_DK ** -0.5)]
    cos = jnp.concatenate([t[0] for t in tabs], axis=1)
    sin = jnp.concatenate([t[1] for t in tabs], axis=1)
    seg_w = MOBA_HEADS * LANES
    qk, colmean = _proj(h, w_lin, gain=norm_g, w_rot=w_rot, cos=cos, sin=sin, seg=seg_w,
                        colmean=True, tm=MOBA_BLOCK)
    plain = _proj(h, jnp.concatenate([w_in[:, o_mv:o_rq], w_in[:, o_rv:]], axis=1).astype(BF16),
                  gain=norm_g)
    mq, mk = qk[:, :seg_w], qk[:, seg_w:2 * seg_w]
    rq, rk = qk[:, 2 * seg_w:3 * seg_w], qk[:, 3 * seg_w:]
    mv, rv, rg = plain[:, :mw], plain[:, mw:mw + vw], plain[:, mw + vw:]
    nb = T // MOBA_BLOCK
    km = colmean[:, 0, seg_w:2 * seg_w].reshape(nb, MOBA_HEADS, LANES).transpose(1, 0, 2)
    km = jnp.pad(km, ((0, 0), (MOBA_HEAD_DIM, LANES - MOBA_HEAD_DIM - nb), (0, 0)))
    mq_b, mk_b = _moba_gate(mq, mk, km)
    a_out = _flash(mq_b, mk_b, mv, tq=MOBA_BLOCK, tk=MOBA_BLOCK)
    b_out = _retention(rq, rk, rv, rg, ret_gn)
    return _matmul_res([a_out, b_out], [w_o[:mw].astype(BF16), w_o[mw:].astype(BF16)], h)


def _odd_mixer(h, norm_g, w_down, q_norm, w_uq, kv_norm, w_ukv, w_o):
    T = h.shape[0]
    lat = MLA_Q_RANK + MLA_KV_RANK
    half = MLA_ROPE // 2
    dq = MLA_NOPE + MLA_ROPE
    w_dn = jnp.concatenate(
        [w_down[:, :lat].astype(BF16),
         _take_cols(w_down[:, lat:], _head_cols(1, MLA_ROPE, 0, MLA_ROPE)),
         _take_cols(w_down[:, lat:], _rot_cols(1, MLA_ROPE, 0, half))], axis=1)
    down = _proj(h, w_dn, gain=norm_g, out_dtype=F32)
    cos, sin = _rope_cs(T, MLA_ROPE, ROPE_THETA)
    ck, sk = _lane_tables(T, cos, sin, rot_at=0, keep=MLA_ROPE, scale=1.0)
    cqn, ckvx = _mla_mid(down, q_norm, kv_norm, ck, sk)
    cq_t, sq_t = _lane_tables(T, cos, sin, rot_at=MLA_NOPE, keep=dq, scale=dq ** -0.5)
    q = _proj(cqn, _take_cols(w_uq, _head_cols(MLA_HEADS, dq, 0, dq)),
              w_rot=_take_cols(w_uq, _rot_cols(MLA_HEADS, dq, MLA_NOPE, half, dst_off=MLA_NOPE)),
              cos=cq_t, sin=sq_t, seg=MLA_HEADS * LANES)
    kvw = MLA_NOPE + MLA_V
    wk = _take_cols(w_ukv, _head_cols(MLA_HEADS, kvw, 0, MLA_NOPE))
    place = np.zeros((LANES, MLA_HEADS * LANES), np.float32)
    for hh in range(MLA_HEADS):
        place[np.arange(MLA_ROPE), hh * LANES + MLA_NOPE + np.arange(MLA_ROPE)] = 1.0
    wk = jnp.concatenate([wk, jnp.asarray(place, BF16)], axis=0)
    wv = _take_cols(w_ukv, _head_cols(MLA_HEADS, kvw, MLA_NOPE, MLA_V, dst_stride=MLA_V))
    wv = jnp.concatenate([wv, jnp.zeros((LANES, wv.shape[1]), BF16)], axis=0)
    kv = _proj(ckvx, jnp.concatenate([wk, wv], axis=1))
    kw = MLA_HEADS * LANES
    o = _flash(q, kv[:, :kw], kv[:, kw:])
    return _matmul_res([o], [w_o.astype(BF16)], h)


def kernel(x, attn_norm, ffn_norm, ev_w_in, ev_ret_gn, ev_w_o, od_w_down, od_q_norm, od_w_uq,
           od_kv_norm, od_w_ukv, od_w_o, peer_wq, peer_keys, peer_u, peer_v, final_norm):
    B, S, D = x.shape
    assert B == 1
    h = x.reshape(S, D)
    depth = attn_norm.shape[0]
    for i in range(depth):
        j = i // 2
        if i % 2 == 0:
            h = _even_mixer(h, attn_norm[i], ev_w_in[j], ev_ret_gn[j], ev_w_o[j])
        else:
            h = _odd_mixer(h, attn_norm[i], od_w_down[j], od_q_norm[j], od_w_uq[j],
                           od_kv_norm[j], od_w_ukv[j], od_w_o[j])
        h = _peer_ffn(h, ffn_norm[i], peer_wq[i], peer_keys[i], peer_u[i], peer_v[i])
    return _rmsnorm(h, final_norm).reshape(B, S, D)
```

```python
import functools

import numpy as np
import jax
import jax.numpy as jnp
from jax import lax
from jax.experimental import pallas as pl
from jax.experimental.pallas import tpu as pltpu

F32 = jnp.float32
BF16 = jnp.bfloat16

LANES = 128
VMEM_LIMIT = 56 * 1024 * 1024

D_MODEL = 1024
EPS = 1e-6
ROPE_THETA = 500000.0
LOG2E = float(np.log2(np.e))

MOBA_HEADS = 8
MOBA_HEAD_DIM = 64
MOBA_ROT = MOBA_HEAD_DIM // 4
MOBA_BLOCK = 256
MOBA_TOPK = 3
MASK_BIAS = -1e9

RET_HEADS = 8
RET_DK = 64
RET_DV = 128
RET_THETA = 10000.0
RET_CHUNK = 256

MLA_HEADS = 16
MLA_NOPE = 64
MLA_ROPE = 32
MLA_V = 64
MLA_Q_RANK = 512
MLA_KV_RANK = 256

PEER_HEADS = 8
PEER_NKEYS = 128
PEER_DKEY = 128
PEER_TOPK = 16


def _cparams(*sem):
    return pltpu.CompilerParams(dimension_semantics=sem, vmem_limit_bytes=VMEM_LIMIT)


def _proj_body(*refs, norm, rope, colmean, tn):
    it = iter(refs)
    x_ref = next(it)
    g_ref = next(it) if norm else None
    w_ref = next(it)
    if rope:
        wr_ref, c_ref, s_ref = next(it), next(it), next(it)
    o_ref = next(it)
    cm_ref = next(it) if colmean else None
    xn_ref = next(it)

    @pl.when(pl.program_id(1) == 0)
    def _():
        x = x_ref[...].astype(F32)
        if norm:
            x = x * lax.rsqrt(jnp.mean(x * x, axis=-1, keepdims=True) + EPS) * g_ref[...]
        xn_ref[...] = x.astype(BF16)

    xn = xn_ref[...]
    y = jnp.dot(xn, w_ref[...], preferred_element_type=F32)
    if not rope:
        o_ref[...] = y.astype(o_ref.dtype)
        return
    yr = jnp.dot(xn, wr_ref[...], preferred_element_type=F32)
    c = c_ref[...]
    s = s_ref[...]
    for k in range(tn // LANES):
        sl = slice(k * LANES, (k + 1) * LANES)
        val = y[:, sl] * c + yr[:, sl] * s
        o_ref[:, sl] = val.astype(o_ref.dtype)
        if colmean:
            cm_ref[0, :, sl] = jnp.mean(val, axis=0, keepdims=True)


def _proj(x, w, *, gain=None, w_rot=None, cos=None, sin=None, seg=None,
          colmean=False, out_dtype=BF16, tm=256, tn=512):
    T, K = x.shape
    N = w.shape[1]
    tn = min(tn, N)
    assert T % tm == 0 and N % tn == 0 and tn % LANES == 0
    norm, rope = gain is not None, w_rot is not None
    in_specs = [pl.BlockSpec((tm, K), lambda i, j: (i, 0))]
    args = [x]
    if norm:
        in_specs.append(pl.BlockSpec((1, K), lambda i, j: (0, 0)))
        args.append(gain.reshape(1, K).astype(F32))
    in_specs.append(pl.BlockSpec((K, tn), lambda i, j: (0, j)))
    args.append(w)
    if rope:
        assert seg % tn == 0
        tab = lambda i, j: (i, (j * tn) // seg)
        in_specs += [pl.BlockSpec((K, tn), lambda i, j: (0, j)),
                     pl.BlockSpec((tm, LANES), tab), pl.BlockSpec((tm, LANES), tab)]
        args += [w_rot, cos, sin]
    out_shape = [jax.ShapeDtypeStruct((T, N), out_dtype)]
    out_specs = [pl.BlockSpec((tm, tn), lambda i, j: (i, j))]
    if colmean:
        assert rope
        out_shape.append(jax.ShapeDtypeStruct((T // tm, 1, N), F32))
        out_specs.append(pl.BlockSpec((1, 1, tn), lambda i, j: (i, 0, j)))
    res = pl.pallas_call(
        functools.partial(_proj_body, norm=norm, rope=rope, colmean=colmean, tn=tn),
        grid=(T // tm, N // tn),
        in_specs=in_specs, out_specs=out_specs, out_shape=out_shape,
        scratch_shapes=[pltpu.VMEM((tm, K), BF16)],
        compiler_params=_cparams("parallel", "arbitrary"),
        name="proj",
    )(*args)
    return res if colmean else res[0]


def _matmul_res_body(*refs, n_in):
    xs, ws = refs[:n_in], refs[n_in:2 * n_in]
    r_ref, o_ref = refs[2 * n_in], refs[2 * n_in + 1]
    acc = r_ref[...]
    for x_ref, w_ref in zip(xs, ws):
        acc = acc + jnp.dot(x_ref[...], w_ref[...], preferred_element_type=F32)
    o_ref[...] = acc


def _matmul_res(xs, ws, res, *, tm=256, tn=512):
    T, N = res.shape
    n_in = len(xs)
    in_specs = [pl.BlockSpec((tm, x.shape[1]), lambda i, j: (i, 0)) for x in xs]
    in_specs += [pl.BlockSpec((w.shape[0], tn), lambda i, j: (0, j)) for w in ws]
    in_specs.append(pl.BlockSpec((tm, tn), lambda i, j: (i, j)))
    return pl.pallas_call(
        functools.partial(_matmul_res_body, n_in=n_in),
        grid=(T // tm, N // tn),
        in_specs=in_specs,
        out_specs=pl.BlockSpec((tm, tn), lambda i, j: (i, j)),
        out_shape=jax.ShapeDtypeStruct((T, N), F32),
        compiler_params=_cparams("parallel", "parallel"),
        name="matmul_res",
    )(*xs, *ws, res)


def _moba_gate_body(q_ref, k_ref, v_ref, km_ref, qo_ref, ko_ref, vo_ref, *, tq, heads):
    blk = pl.program_id(0)
    lane = lax.broadcasted_iota(jnp.int32, (tq, LANES), 1)
    bidx = lane - MOBA_HEAD_DIM
    for h in range(heads):
        sl = slice(h * LANES, (h + 1) * LANES)
        q = q_ref[:, sl].astype(F32)
        gate = lax.dot_general(q, km_ref[h], (((1,), (1,)), ((), ())),
                               precision=lax.Precision.HIGHEST, preferred_element_type=F32)
        g = jnp.where((bidx >= 0) & (bidx < blk), gate, -jnp.inf)
        sel = bidx == blk
        for _ in range(MOBA_TOPK):
            m = jnp.max(g, axis=-1, keepdims=True)
            idx = jnp.min(jnp.where(g == m, lane, 2 * LANES), axis=-1, keepdims=True)
            pick = (lane == idx) & (m > -jnp.inf)
            sel = sel | pick
            g = jnp.where(pick, -jnp.inf, g)
        bias = jnp.where(sel, 0.0, MASK_BIAS)
        qo_ref[:, sl] = jnp.where(bidx < 0, q, bias).astype(BF16)
        ko_ref[:, sl] = jnp.where(bidx == blk, 1.0, k_ref[:, sl].astype(F32)).astype(BF16)
        vo_ref[:, sl] = jnp.where(bidx == 0, 1.0, v_ref[:, sl].astype(F32)).astype(BF16)


def _moba_gate(qk, plain, kmean_pad, heads):
    T = qk.shape[0]
    W = heads * LANES
    tq = MOBA_BLOCK
    assert T % tq == 0 and T // tq <= LANES - MOBA_HEAD_DIM
    col = lambda c: pl.BlockSpec((tq, W), lambda i: (i, c))
    return pl.pallas_call(
        functools.partial(_moba_gate_body, tq=tq, heads=heads),
        grid=(T // tq,),
        in_specs=[col(0), col(1), col(0),
                  pl.BlockSpec((heads, LANES, LANES), lambda i: (0, 0, 0))],
        out_specs=[col(0)] * 3,
        out_shape=[jax.ShapeDtypeStruct((T, W), BF16)] * 3,
        compiler_params=_cparams("parallel"),
        name="moba_gate",
    )(qk, qk, plain, kmean_pad)


def _flash_body(q_ref, k_ref, v_ref, o_ref, m_ref, acc_ref, *, tq, tk, l_lane):
    i = pl.program_id(1)
    n_full = (i * tq) // tk
    m_ref[...] = jnp.full(m_ref.shape, -jnp.inf, F32)
    acc_ref[...] = jnp.zeros(acc_ref.shape, F32)

    def step(j, masked):
        start = pl.multiple_of(j * tk, tk)
        for hh in range(2):
            hsl = slice(hh * LANES, (hh + 1) * LANES)
            s = lax.dot_general(q_ref[:, hsl], k_ref[pl.ds(start, tk), hsl],
                                (((1,), (1,)), ((), ())), preferred_element_type=F32)
            if masked:
                row = i * tq + lax.broadcasted_iota(jnp.int32, (tq, tk), 0)
                col = j * tk + lax.broadcasted_iota(jnp.int32, (tq, tk), 1)
                s = jnp.where(col <= row, s, -jnp.inf)
            m_prev = m_ref[hh]
            m_new = jnp.maximum(m_prev, jnp.max(s, axis=-1, keepdims=True))
            p = jnp.exp2(s - pltpu.repeat(m_new, tk // LANES, axis=1))
            acc_ref[hh] = jnp.exp2(m_prev - m_new) * acc_ref[hh] + jnp.dot(
                p.astype(BF16), v_ref[pl.ds(start, tk), hsl], preferred_element_type=F32)
            m_ref[hh] = m_new

    def loop_body(j, carry):
        step(j, False)
        return carry

    lax.fori_loop(0, n_full, loop_body, 0)
    for d in range(tq // tk):
        step(n_full + d, True)
    for hh in range(2):
        acc = acc_ref[hh]
        o_ref[:, hh * LANES:(hh + 1) * LANES] = (
            acc / acc[:, l_lane:l_lane + 1]).astype(o_ref.dtype)


def _flash(q, k, v, heads, *, q_off=0, k_off=0, v_off=0, l_lane, tq=512, tk=512):
    T = q.shape[0]
    tq, tk = min(tq, T), min(tk, T)
    assert heads % 2 == 0 and T % tq == 0 and tq % tk == 0
    pair = 2 * LANES
    return pl.pallas_call(
        functools.partial(_flash_body, tq=tq, tk=tk, l_lane=l_lane),
        grid=(heads // 2, T // tq),
        in_specs=[pl.BlockSpec((tq, pair), lambda h, i: (i, q_off + h)),
                  pl.BlockSpec((T, pair), lambda h, i: (0, k_off + h)),
                  pl.BlockSpec((T, pair), lambda h, i: (0, v_off + h))],
        out_specs=pl.BlockSpec((tq, pair), lambda h, i: (i, h)),
        out_shape=jax.ShapeDtypeStruct((T, heads * LANES), BF16),
        scratch_shapes=[pltpu.VMEM((2, tq, LANES), F32)] * 2,
        compiler_params=_cparams("parallel", "arbitrary"),
        name="flash",
    )(q, k, v)


def _retention_body(q_ref, k_ref, v_ref, g_ref, gn_ref, dm_ref, xi_ref, ze_ref, gc_ref,
                    o_ref, r_ref):
    @pl.when(pl.program_id(1) == 0)
    def _():
        r_ref[...] = jnp.zeros_like(r_ref)

    q = q_ref[...]
    k = k_ref[...]
    v = v_ref[...]
    r_old = r_ref[...]
    inner = lax.dot_general(q, k, (((1,), (1,)), ((), ())), preferred_element_type=F32) * dm_ref[0]
    out = jnp.dot(inner.astype(BF16), v, preferred_element_type=F32)
    out = out + jnp.dot(q, r_old.astype(BF16), preferred_element_type=F32) * xi_ref[0]
    kz = (k.astype(F32) * ze_ref[0]).T.astype(BF16)
    r_ref[...] = r_old * gc_ref[0] + jnp.dot(kz, v, preferred_element_type=F32)
    mu = jnp.mean(out, axis=-1, keepdims=True)
    cen = out - mu
    var = jnp.mean(cen * cen, axis=-1, keepdims=True)
    rn = cen * lax.rsqrt(var + EPS) * gn_ref[...]
    gate = g_ref[...].astype(F32)
    o_ref[...] = (rn * (gate / (1.0 + jnp.exp(-gate)))).astype(o_ref.dtype)


def _retention(qk, plain, gn, heads, *, q_off, k_off, v_off, g_off):
    T = qk.shape[0]
    W = heads * LANES
    C = min(RET_CHUNK, T)
    assert T % C == 0
    log_g = jnp.log(1.0 - 2.0 ** (-5.0 - jnp.arange(heads, dtype=F32)))
    pos = jnp.arange(C, dtype=F32)
    diff = pos[:, None] - pos[None, :]
    dmat = jnp.where(diff >= 0, jnp.exp(log_g[:, None, None] * jnp.maximum(diff, 0.0)), 0.0)
    rep = lambda t: jnp.broadcast_to(t[..., None], t.shape + (LANES,))
    xi = rep(jnp.exp(log_g[:, None] * (pos + 1.0)))
    zeta = rep(jnp.exp(log_g[:, None] * (C - 1.0 - pos)))
    g_chunk = rep(jnp.exp(log_g * C)[:, None])
    tile = lambda off: pl.BlockSpec((C, LANES), lambda h, c: (c, off + h))
    head_tab = lambda r: pl.BlockSpec((1, r, LANES), lambda h, c: (h, 0, 0))
    return pl.pallas_call(
        _retention_body,
        grid=(heads, T // C),
        in_specs=[tile(q_off), tile(k_off), tile(v_off), tile(g_off),
                  pl.BlockSpec((1, LANES), lambda h, c: (0, h)),
                  pl.BlockSpec((1, C, C), lambda h, c: (h, 0, 0)),
                  head_tab(C), head_tab(C), head_tab(1)],
        out_specs=tile(0),
        out_shape=jax.ShapeDtypeStruct((T, W), BF16),
        scratch_shapes=[pltpu.VMEM((LANES, LANES), F32)],
        compiler_params=_cparams("parallel", "arbitrary"),
        name="retention",
    )(qk, qk, plain, plain, gn.reshape(1, W).astype(F32), dmat, xi, zeta, g_chunk)


def _mla_mid_body(d_ref, qn_ref, kvn_ref, c_ref, s_ref, cq_ref, ckv_ref):
    def rms(x, g):
        return x * lax.rsqrt(jnp.mean(x * x, axis=-1, keepdims=True) + EPS) * g

    cq_ref[...] = rms(d_ref[:, :MLA_Q_RANK], qn_ref[...]).astype(BF16)
    lo = MLA_Q_RANK + MLA_KV_RANK
    ckv_ref[:, :MLA_KV_RANK] = rms(d_ref[:, MLA_Q_RANK:lo], kvn_ref[...]).astype(BF16)
    kr = d_ref[:, lo:lo + LANES] * c_ref[...] + d_ref[:, lo + LANES:lo + 2 * LANES] * s_ref[...]
    lane = lax.broadcasted_iota(jnp.int32, kr.shape, 1)
    ckv_ref[:, MLA_KV_RANK:] = jnp.where(lane == MLA_ROPE, 1.0, kr).astype(BF16)


def _mla_mid(down, q_norm, kv_norm, cos, sin, *, tm=256):
    T, W = down.shape
    wide = MLA_KV_RANK + LANES
    return pl.pallas_call(
        _mla_mid_body,
        grid=(T // tm,),
        in_specs=[pl.BlockSpec((tm, W), lambda i: (i, 0)),
                  pl.BlockSpec((1, MLA_Q_RANK), lambda i: (0, 0)),
                  pl.BlockSpec((1, MLA_KV_RANK), lambda i: (0, 0)),
                  pl.BlockSpec((tm, LANES), lambda i: (i, 0)),
                  pl.BlockSpec((tm, LANES), lambda i: (i, 0))],
        out_specs=[pl.BlockSpec((tm, MLA_Q_RANK), lambda i: (i, 0)),
                   pl.BlockSpec((tm, wide), lambda i: (i, 0))],
        out_shape=[jax.ShapeDtypeStruct((T, MLA_Q_RANK), BF16),
                   jax.ShapeDtypeStruct((T, wide), BF16)],
        compiler_params=_cparams("parallel"),
        name="mla_mid",
    )(down, q_norm.reshape(1, -1).astype(F32), kv_norm.reshape(1, -1).astype(F32), cos, sin)


def _peer_scores_body(x_ref, g_ref, wq_ref, kt_ref, xn_ref, st_ref):
    x = x_ref[...]
    xn = (x * lax.rsqrt(jnp.mean(x * x, axis=-1, keepdims=True) + EPS) * g_ref[...]).astype(BF16)
    xn_ref[...] = xn
    qry = jnp.dot(xn, wq_ref[...], preferred_element_type=F32).astype(BF16)
    st_ref[...] = lax.dot_general(kt_ref[...], qry, (((1,), (1,)), ((), ())),
                                  preferred_element_type=F32)


def _peer_scores(h, gain, wq, keys_t, *, tm=256):
    T, D = h.shape
    R = keys_t.shape[0]
    return pl.pallas_call(
        _peer_scores_body,
        grid=(T // tm,),
        in_specs=[pl.BlockSpec((tm, D), lambda i: (i, 0)),
                  pl.BlockSpec((1, D), lambda i: (0, 0)),
                  pl.BlockSpec(wq.shape, lambda i: (0, 0)),
                  pl.BlockSpec(keys_t.shape, lambda i: (0, 0))],
        out_specs=[pl.BlockSpec((tm, D), lambda i: (i, 0)),
                   pl.BlockSpec((R, tm), lambda i: (0, i))],
        out_shape=[jax.ShapeDtypeStruct((T, D), BF16), jax.ShapeDtypeStruct((R, T), F32)],
        compiler_params=_cparams("parallel"),
        name="peer_scores",
    )(h, gain.reshape(1, D).astype(F32), wq, keys_t)


PEER_RANKS = PEER_TOPK + 1
PEER_CAND = [(i, PEER_RANKS // (i + 1)) for i in range(PEER_RANKS)]
PEER_NCAND = -(-sum(c for _, c in PEER_CAND) // 8) * 8


PEER_NORANK = 64.0


def _peer_topk_body(st_ref, pkf_ref, pkb_ref, t1_ref, t2_ref, cand_ref):
    n = PEER_NKEYS
    for h in range(PEER_HEADS):
        s1 = st_ref[(2 * h) * n:(2 * h + 1) * n, :]
        s2 = st_ref[(2 * h + 1) * n:(2 * h + 2) * n, :]
        vals = s1
        for r in range(PEER_RANKS):
            m = jnp.max(vals, axis=0, keepdims=True)
            t1_ref[r:r + 1, :] = m
            vals = jnp.where(vals == m, -jnp.inf, vals)
        vals = s2
        rank2 = jnp.full(s2.shape, PEER_NORANK, F32)
        for r in range(PEER_RANKS):
            m = jnp.max(vals, axis=0, keepdims=True)
            t2_ref[r:r + 1, :] = m
            hit = vals == m
            rank2 = jnp.where(hit, float(r), rank2)
            vals = jnp.where(hit, -jnp.inf, vals)
        cand_ref[...] = jnp.full(cand_ref.shape, -jnp.inf, F32)
        rowp = 0
        for i, cnt in PEER_CAND:
            cand_ref[rowp:rowp + cnt, :] = t1_ref[i:i + 1, :] + t2_ref[0:cnt, :]
            rowp += cnt
        c = cand_ref[...]
        top1 = t1_ref[0:1, :]
        top2 = t2_ref[0:1, :]
        cmax = top1 + top2
        z = jnp.zeros_like(cmax)
        kth = cmax
        for r in range(PEER_RANKS):
            m = jnp.max(c, axis=0, keepdims=True)
            if r < PEER_TOPK:
                z = z + jnp.exp(m - cmax)
                kth = m
            else:
                tau = 0.5 * (kth + m)
            c = jnp.where(c == m, -jnp.inf, c)
        cnt = jnp.zeros(s1.shape, F32)
        for j in range(PEER_TOPK):
            cnt = cnt + jnp.where(s1 + t2_ref[j:j + 1, :] > tau, 1.0, 0.0)
        pkf_ref[h, 0] = jnp.exp(s1 - top1) / z
        pkf_ref[h, 1] = cnt
        pkb_ref[h, 0] = rank2.astype(BF16)
        pkb_ref[h, 1] = jnp.exp(s2 - top2).astype(BF16)


def _peer_topk(st, *, tm=256):
    R, T = st.shape
    tm = min(tm, T)
    blk = lambda i: (0, 0, 0, i)
    shape = (PEER_HEADS, 2, PEER_NKEYS, T)
    return pl.pallas_call(
        _peer_topk_body,
        grid=(T // tm,),
        in_specs=[pl.BlockSpec((R, tm), lambda i: (0, i))],
        out_specs=[pl.BlockSpec((PEER_HEADS, 2, PEER_NKEYS, tm), blk)] * 2,
        out_shape=[jax.ShapeDtypeStruct(shape, F32), jax.ShapeDtypeStruct(shape, BF16)],
        scratch_shapes=[pltpu.VMEM((PEER_RANKS + 7, tm), F32),
                        pltpu.VMEM((PEER_RANKS + 7, tm), F32),
                        pltpu.VMEM((PEER_NCAND, tm), F32)],
        compiler_params=_cparams("parallel"),
        name="peer_topk",
    )(st)


def _peer_dense_body(xn_ref, u_ref, vt_ref, pkf_ref, pkb_ref, h_ref, o_ref,
                     ht0_ref, ht1_ref, acc_ref, *, tm, te, n_e, n_steps):
    s = pl.program_id(0)
    n = PEER_NKEYS
    group = 2

    def scores(dst_ref):
        dst_ref[...] = lax.dot_general(u_ref[...], xn_ref[...], (((1,), (1,)), ((), ())),
                                       preferred_element_type=F32)

    def experts(src_ref):
        e = lax.rem(s - 1, n_e)
        for gb in range(te // (group * n)):
            acts = []
            for ab in range(gb * group, (gb + 1) * group):
                a = e * (te // n) + ab
                gsum = jnp.zeros((n, tm), BF16)
                for h in range(PEER_HEADS):
                    w1 = jnp.broadcast_to(pkf_ref[h, 0, pl.ds(a, 1), :], (n, tm)).astype(BF16)
                    cnt = jnp.broadcast_to(pkf_ref[h, 1, pl.ds(a, 1), :], (n, tm)).astype(BF16)
                    gsum = gsum + jnp.where(pkb_ref[h, 0] < cnt, pkb_ref[h, 1], 0.0) * w1
                hs = src_ref[ab * n:(ab + 1) * n, :]
                act = 0.5 * hs * (1.0 + lax.erf(hs * np.float32(1.0 / np.sqrt(2.0))))
                acts.append(act.astype(BF16) * gsum)
            rows = slice(gb * group * n, (gb + 1) * group * n)
            acc_ref[...] += jnp.dot(vt_ref[:, rows], jnp.concatenate(acts, axis=0),
                                    preferred_element_type=F32)

        @pl.when(e == n_e - 1)
        def _():
            o_ref[...] = h_ref[...] + acc_ref[...].T
            acc_ref[...] = jnp.zeros_like(acc_ref)

    even = lax.rem(s, 2) == 0
    steady = (s > 0) & (s < n_steps)

    @pl.when(s == 0)
    def _():
        acc_ref[...] = jnp.zeros_like(acc_ref)
        scores(ht0_ref)

    @pl.when(steady & even)
    def _():
        scores(ht0_ref)
        experts(ht1_ref)

    @pl.when(steady & jnp.logical_not(even))
    def _():
        scores(ht1_ref)
        experts(ht0_ref)

    @pl.when(s == n_steps)
    def _():
        experts(ht1_ref if n_steps % 2 == 0 else ht0_ref)


def _peer_dense(xn, u, vt, pkf, pkb, h, *, tm=512, te=1024):
    T, D = xn.shape
    E = u.shape[0]
    tm = min(tm, T)
    assert T % tm == 0 and E % te == 0 and te % (2 * PEER_NKEYS) == 0
    n_e = E // te
    n_steps = (T // tm) * n_e
    cur = lambda s: jnp.minimum(s, n_steps - 1)
    prev = lambda s: jnp.maximum(s - 1, 0)
    pk_spec = pl.BlockSpec((PEER_HEADS, 2, PEER_NKEYS, tm), lambda s: (0, 0, 0, prev(s) // n_e))
    return pl.pallas_call(
        functools.partial(_peer_dense_body, tm=tm, te=te, n_e=n_e, n_steps=n_steps),
        grid=(n_steps + 1,),
        in_specs=[pl.BlockSpec((tm, D), lambda s: (cur(s) // n_e, 0)),
                  pl.BlockSpec((te, D), lambda s: (cur(s) % n_e, 0)),
                  pl.BlockSpec((D, te), lambda s: (0, prev(s) % n_e)),
                  pk_spec, pk_spec,
                  pl.BlockSpec((tm, D), lambda s: (prev(s) // n_e, 0))],
        out_specs=pl.BlockSpec((tm, D), lambda s: (prev(s) // n_e, 0)),
        out_shape=jax.ShapeDtypeStruct((T, D), F32),
        scratch_shapes=[pltpu.VMEM((te, tm), F32), pltpu.VMEM((te, tm), F32),
                        pltpu.VMEM((D, tm), F32)],
        compiler_params=_cparams("arbitrary"),
        name="peer_dense",
    )(xn, u, vt, pkf, pkb, h)


def _peer_ffn(h, gain, wq, keys, u_tab, v_tab):
    nk, dh = PEER_NKEYS, PEER_DKEY // 2
    groups = PEER_HEADS * 2
    keys_t = jnp.einsum("gnd,gk->gnkd", keys.reshape(groups, nk, dh).astype(F32),
                        jnp.eye(groups, dtype=F32)).reshape(groups * nk, groups * dh).astype(BF16)
    xn, st = _peer_scores(h, gain, wq.astype(BF16), keys_t)
    pkf, pkb = _peer_topk(st)
    return _peer_dense(xn, u_tab.astype(BF16), v_tab.astype(BF16).T, pkf, pkb, h)


def _rmsnorm_body(x_ref, g_ref, o_ref):
    x = x_ref[...]
    o_ref[...] = x * lax.rsqrt(jnp.mean(x * x, axis=-1, keepdims=True) + EPS) * g_ref[...]


def _rmsnorm(x, g, *, tm=512):
    T, D = x.shape
    tm = min(tm, T)
    return pl.pallas_call(
        _rmsnorm_body,
        grid=(T // tm,),
        in_specs=[pl.BlockSpec((tm, D), lambda i: (i, 0)), pl.BlockSpec((1, D), lambda i: (0, 0))],
        out_specs=pl.BlockSpec((tm, D), lambda i: (i, 0)),
        out_shape=jax.ShapeDtypeStruct((T, D), F32),
        compiler_params=_cparams("parallel"),
        name="rmsnorm",
    )(x, g.reshape(1, D).astype(F32))


def _lane_tables(T, rot_dim, theta, *, rot_at, keep, scale):
    r = rot_dim // 2
    inv = 1.0 / (theta ** (jnp.arange(0, rot_dim, 2, dtype=F32) / rot_dim))
    lane = np.arange(LANES)
    in_rot = (lane >= rot_at) & (lane < rot_at + rot_dim)
    inv_lane = jnp.where(jnp.asarray(in_rot), inv[np.where(in_rot, (lane - rot_at) % r, 0)], 0.0)
    ang = jnp.arange(T, dtype=F32)[:, None] * inv_lane[None, :]
    c = jnp.cos(ang) * jnp.asarray((lane < keep) * scale, F32)[None, :]
    s = jnp.sin(ang) * jnp.asarray(in_rot * scale, F32)[None, :]
    return c, s


def _head_cols(n_heads, src_stride, src_off, width, *, dst_stride=LANES, dst_off=0):
    idx = np.zeros(n_heads * dst_stride, np.int32)
    sgn = np.zeros(n_heads * dst_stride, np.float32)
    for h in range(n_heads):
        d = h * dst_stride + dst_off
        idx[d:d + width] = h * src_stride + src_off + np.arange(width)
        sgn[d:d + width] = 1.0
    return idx, sgn


def _rot_cols(n_heads, src_stride, src_off, r, *, dst_stride=LANES, dst_off=0):
    idx = np.zeros(n_heads * dst_stride, np.int32)
    sgn = np.zeros(n_heads * dst_stride, np.float32)
    for h in range(n_heads):
        d = h * dst_stride + dst_off
        s = h * src_stride + src_off
        idx[d:d + r] = s + r + np.arange(r)
        sgn[d:d + r] = -1.0
        idx[d + r:d + 2 * r] = s + np.arange(r)
        sgn[d + r:d + 2 * r] = 1.0
    return idx, sgn


def _take_cols(w, idx_sgn):
    idx, sgn = idx_sgn
    return (jnp.take(w, jnp.asarray(idx), axis=1) * jnp.asarray(sgn)[None, :]).astype(BF16)


def _even_mixer(h, norm_g, w_in, ret_gn, w_o):
    T = h.shape[0]
    mw = MOBA_HEADS * MOBA_HEAD_DIM
    rw = RET_HEADS * RET_DK
    vw = RET_HEADS * RET_DV
    o_mq, o_mk, o_mv, o_rq, o_rk, o_rv, o_rg = np.cumsum([0, mw, mw, mw, rw, rw, vw])

    def seg(off, heads, stride, rot_dim):
        w = w_in[:, off:off + heads * stride]
        return (_take_cols(w, _head_cols(heads, stride, 0, stride)),
                _take_cols(w, _rot_cols(heads, stride, 0, rot_dim // 2)))

    segs = [seg(o_mq, MOBA_HEADS, MOBA_HEAD_DIM, MOBA_ROT), seg(o_mk, MOBA_HEADS, MOBA_HEAD_DIM, MOBA_ROT),
            seg(o_rq, RET_HEADS, RET_DK, RET_DK), seg(o_rk, RET_HEADS, RET_DK, RET_DK)]
    w_lin = jnp.concatenate([s[0] for s in segs], axis=1)
    w_rot = jnp.concatenate([s[1] for s in segs], axis=1)
    mtab = functools.partial(_lane_tables, T, MOBA_ROT, ROPE_THETA, rot_at=0, keep=MOBA_HEAD_DIM)
    rtab = functools.partial(_lane_tables, T, RET_DK, RET_THETA, rot_at=0, keep=RET_DK)
    tabs = [mtab(scale=MOBA_HEAD_DIM ** -0.5 * LOG2E), mtab(scale=1.0),
            rtab(scale=1.0), rtab(scale=RET_DK ** -0.5)]
    cos = jnp.concatenate([t[0] for t in tabs], axis=1)
    sin = jnp.concatenate([t[1] for t in tabs], axis=1)
    seg_w = MOBA_HEADS * LANES
    qk, colmean = _proj(h, w_lin, gain=norm_g, w_rot=w_rot, cos=cos, sin=sin, seg=seg_w,
                        colmean=True, tm=MOBA_BLOCK)
    w_plain = jnp.concatenate(
        [_take_cols(w_in[:, o_mv:o_rq], _head_cols(MOBA_HEADS, MOBA_HEAD_DIM, 0, MOBA_HEAD_DIM)),
         w_in[:, o_rv:].astype(BF16)], axis=1)
    plain = _proj(h, w_plain, gain=norm_g)
    nb = T // MOBA_BLOCK
    km = colmean[:, 0, seg_w:2 * seg_w].reshape(nb, MOBA_HEADS, LANES).transpose(1, 0, 2)
    km = jnp.pad(km, ((0, 0), (MOBA_HEAD_DIM, LANES - MOBA_HEAD_DIM - nb), (0, 0)))
    mq_b, mk_b, mv_b = _moba_gate(qk, plain, km, MOBA_HEADS)
    a_out = _flash(mq_b, mk_b, mv_b, MOBA_HEADS, l_lane=MOBA_HEAD_DIM)
    b_out = _retention(qk, plain, ret_gn, RET_HEADS, q_off=2 * RET_HEADS, k_off=3 * RET_HEADS,
                       v_off=MOBA_HEADS, g_off=MOBA_HEADS + RET_HEADS)
    w_oa = jnp.zeros((MOBA_HEADS, LANES, w_o.shape[1]), BF16).at[:, :MOBA_HEAD_DIM].set(
        w_o[:mw].astype(BF16).reshape(MOBA_HEADS, MOBA_HEAD_DIM, -1)).reshape(seg_w, -1)
    return _matmul_res([a_out, b_out], [w_oa, w_o[mw:].astype(BF16)], h)


def _odd_mixer(h, norm_g, w_down, q_norm, w_uq, kv_norm, w_ukv, w_o):
    T = h.shape[0]
    lat = MLA_Q_RANK + MLA_KV_RANK
    half = MLA_ROPE // 2
    dq = MLA_NOPE + MLA_ROPE
    kw = MLA_HEADS * LANES
    w_dn = jnp.concatenate(
        [w_down[:, :lat].astype(BF16),
         _take_cols(w_down[:, lat:], _head_cols(1, MLA_ROPE, 0, MLA_ROPE)),
         _take_cols(w_down[:, lat:], _rot_cols(1, MLA_ROPE, 0, half))], axis=1)
    down = _proj(h, w_dn, gain=norm_g, out_dtype=F32)
    ck, sk = _lane_tables(T, MLA_ROPE, ROPE_THETA, rot_at=0, keep=MLA_ROPE, scale=1.0)
    cqn, ckvx = _mla_mid(down, q_norm, kv_norm, ck, sk)
    cq_t, sq_t = _lane_tables(T, MLA_ROPE, ROPE_THETA, rot_at=MLA_NOPE, keep=dq,
                              scale=dq ** -0.5 * LOG2E)
    q = _proj(cqn, _take_cols(w_uq, _head_cols(MLA_HEADS, dq, 0, dq)),
              w_rot=_take_cols(w_uq, _rot_cols(MLA_HEADS, dq, MLA_NOPE, half, dst_off=MLA_NOPE)),
              cos=cq_t, sin=sq_t, seg=kw)
    kvw = MLA_NOPE + MLA_V
    place_k = np.zeros((LANES, kw), np.float32)
    place_v = np.zeros((LANES, kw), np.float32)
    for hh in range(MLA_HEADS):
        place_k[np.arange(MLA_ROPE), hh * LANES + MLA_NOPE + np.arange(MLA_ROPE)] = 1.0
        place_v[MLA_ROPE, hh * LANES + MLA_V] = 1.0
    wk = jnp.concatenate([_take_cols(w_ukv, _head_cols(MLA_HEADS, kvw, 0, MLA_NOPE)),
                          jnp.asarray(place_k, BF16)], axis=0)
    wv = jnp.concatenate([_take_cols(w_ukv, _head_cols(MLA_HEADS, kvw, MLA_NOPE, MLA_V)),
                          jnp.asarray(place_v, BF16)], axis=0)
    kv = _proj(ckvx, jnp.concatenate([wk, wv], axis=1))
    o = _flash(q, kv, kv, MLA_HEADS, v_off=MLA_HEADS // 2, l_lane=MLA_V)
    w_oa = jnp.zeros((MLA_HEADS, LANES, w_o.shape[1]), BF16).at[:, :MLA_V].set(
        w_o.astype(BF16).reshape(MLA_HEADS, MLA_V, -1)).reshape(kw, -1)
    return _matmul_res([o], [w_oa], h)


def kernel(x, attn_norm, ffn_norm, ev_w_in, ev_ret_gn, ev_w_o, od_w_down, od_q_norm, od_w_uq,
           od_kv_norm, od_w_ukv, od_w_o, peer_wq, peer_keys, peer_u, peer_v, final_norm):
    B, S, D = x.shape
    assert B == 1
    h = x.reshape(S, D)
    depth = attn_norm.shape[0]
    for i in range(depth):
        j = i // 2
        if i % 2 == 0:
            h = _even_mixer(h, attn_norm[i], ev_w_in[j], ev_ret_gn[j], ev_w_o[j])
        else:
            h = _odd_mixer(h, attn_norm[i], od_w_down[j], od_q_norm[j], od_w_uq[j],
                           od_kv_norm[j], od_w_ukv[j], od_w_o[j])
        h = _peer_ffn(h, ffn_norm[i], peer_wq[i], peer_keys[i], peer_u[i], peer_v[i])
    return _rmsnorm(h, final_norm).reshape(B, S, D)
```

```python
import functools

import numpy as np
import jax
import jax.numpy as jnp
from jax import lax
from jax.experimental import pallas as pl
from jax.experimental.pallas import tpu as pltpu

F32 = jnp.float32
BF16 = jnp.bfloat16

LANES = 128
VMEM_LIMIT = 56 * 1024 * 1024

D_MODEL = 1024
EPS = 1e-6
ROPE_THETA = 500000.0
LOG2E = float(np.log2(np.e))

MOBA_HEADS = 8
MOBA_HEAD_DIM = 64
MOBA_ROT = MOBA_HEAD_DIM // 4
MOBA_BLOCK = 256
MOBA_TOPK = 3
MASK_BIAS = -1e9

RET_HEADS = 8
RET_DK = 64
RET_DV = 128
RET_THETA = 10000.0
RET_CHUNK = 512

MLA_HEADS = 16
MLA_NOPE = 64
MLA_ROPE = 32
MLA_V = 64
MLA_Q_RANK = 512
MLA_KV_RANK = 256

PEER_HEADS = 8
PEER_NKEYS = 128
PEER_DKEY = 128
PEER_TOPK = 16


def _cparams(*sem):
    return pltpu.CompilerParams(dimension_semantics=sem, vmem_limit_bytes=VMEM_LIMIT)


def _proj_body(*refs, norm, rope, colmean, tn):
    it = iter(refs)
    x_ref = next(it)
    g_ref = next(it) if norm else None
    w_ref = next(it)
    if rope:
        wr_ref, c_ref, s_ref = next(it), next(it), next(it)
    o_ref = next(it)
    cm_ref = next(it) if colmean else None
    xn_ref = next(it)

    @pl.when(pl.program_id(1) == 0)
    def _():
        x = x_ref[...].astype(F32)
        if norm:
            x = x * lax.rsqrt(jnp.mean(x * x, axis=-1, keepdims=True) + EPS) * g_ref[...]
        xn_ref[...] = x.astype(BF16)

    xn = xn_ref[...]
    y = jnp.dot(xn, w_ref[...], preferred_element_type=F32)
    if not rope:
        o_ref[...] = y.astype(o_ref.dtype)
        return
    yr = jnp.dot(xn, wr_ref[...], preferred_element_type=F32)
    c = c_ref[...]
    s = s_ref[...]
    for k in range(tn // LANES):
        sl = slice(k * LANES, (k + 1) * LANES)
        val = y[:, sl] * c + yr[:, sl] * s
        o_ref[:, sl] = val.astype(o_ref.dtype)
        if colmean:
            for b in range(val.shape[0] // colmean):
                cm_ref[b, :, sl] = jnp.mean(val[b * colmean:(b + 1) * colmean], axis=0,
                                            keepdims=True)


def _proj(x, w, *, gain=None, w_rot=None, cos=None, sin=None, seg=None,
          colmean=None, out_dtype=BF16, tm=1024, tn=512):
    T, K = x.shape
    N = w.shape[1]
    tm, tn = min(tm, T), min(tn, N)
    assert T % tm == 0 and N % tn == 0 and tn % LANES == 0
    norm, rope = gain is not None, w_rot is not None
    in_specs = [pl.BlockSpec((tm, K), lambda i, j: (i, 0))]
    args = [x]
    if norm:
        in_specs.append(pl.BlockSpec((1, K), lambda i, j: (0, 0)))
        args.append(gain.reshape(1, K).astype(F32))
    in_specs.append(pl.BlockSpec((K, tn), lambda i, j: (0, j)))
    args.append(w)
    if rope:
        assert seg % tn == 0
        tab = lambda i, j: (i, (j * tn) // seg)
        in_specs += [pl.BlockSpec((K, tn), lambda i, j: (0, j)),
                     pl.BlockSpec((tm, LANES), tab), pl.BlockSpec((tm, LANES), tab)]
        args += [w_rot, cos, sin]
    out_shape = [jax.ShapeDtypeStruct((T, N), out_dtype)]
    out_specs = [pl.BlockSpec((tm, tn), lambda i, j: (i, j))]
    if colmean:
        assert rope and tm % colmean == 0
        out_shape.append(jax.ShapeDtypeStruct((T // colmean, 1, N), F32))
        out_specs.append(pl.BlockSpec((tm // colmean, 1, tn), lambda i, j: (i, 0, j)))
    res = pl.pallas_call(
        functools.partial(_proj_body, norm=norm, rope=rope, colmean=colmean, tn=tn),
        grid=(T // tm, N // tn),
        in_specs=in_specs, out_specs=out_specs, out_shape=out_shape,
        scratch_shapes=[pltpu.VMEM((tm, K), BF16)],
        compiler_params=_cparams("parallel", "arbitrary"),
        name="proj",
    )(*args)
    return res if colmean else res[0]


def _matmul_res_body(*refs, n_in):
    xs, ws = refs[:n_in], refs[n_in:2 * n_in]
    r_ref, o_ref = refs[2 * n_in], refs[2 * n_in + 1]
    acc = r_ref[...]
    for x_ref, w_ref in zip(xs, ws):
        acc = acc + jnp.dot(x_ref[...], w_ref[...], preferred_element_type=F32)
    o_ref[...] = acc


def _matmul_res(xs, ws, res, *, tm=512, tn=512):
    T, N = res.shape
    n_in = len(xs)
    in_specs = [pl.BlockSpec((tm, x.shape[1]), lambda i, j: (i, 0)) for x in xs]
    in_specs += [pl.BlockSpec((w.shape[0], tn), lambda i, j: (0, j)) for w in ws]
    in_specs.append(pl.BlockSpec((tm, tn), lambda i, j: (i, j)))
    return pl.pallas_call(
        functools.partial(_matmul_res_body, n_in=n_in),
        grid=(T // tm, N // tn),
        in_specs=in_specs,
        out_specs=pl.BlockSpec((tm, tn), lambda i, j: (i, j)),
        out_shape=jax.ShapeDtypeStruct((T, N), F32),
        compiler_params=_cparams("parallel", "parallel"),
        name="matmul_res",
    )(*xs, *ws, res)


def _moba_gate_body(q_ref, k_ref, v_ref, km_ref, qo_ref, ko_ref, vo_ref, *, tq, heads):
    blk = pl.program_id(0)
    lane = lax.broadcasted_iota(jnp.int32, (tq, LANES), 1)
    bidx = lane - MOBA_HEAD_DIM
    for h in range(heads):
        sl = slice(h * LANES, (h + 1) * LANES)
        q = q_ref[:, sl].astype(F32)
        gate = lax.dot_general(q, km_ref[h], (((1,), (1,)), ((), ())),
                               precision=lax.Precision.HIGHEST, preferred_element_type=F32)
        g = jnp.where((bidx >= 0) & (bidx < blk), gate, -jnp.inf)
        sel = bidx == blk
        for _ in range(MOBA_TOPK):
            m = jnp.max(g, axis=-1, keepdims=True)
            idx = jnp.min(jnp.where(g == m, lane, 2 * LANES), axis=-1, keepdims=True)
            pick = (lane == idx) & (m > -jnp.inf)
            sel = sel | pick
            g = jnp.where(pick, -jnp.inf, g)
        bias = jnp.where(sel, 0.0, MASK_BIAS)
        qo_ref[:, sl] = jnp.where(bidx < 0, q, bias).astype(BF16)
        ko_ref[:, sl] = jnp.where(bidx == blk, 1.0, k_ref[:, sl].astype(F32)).astype(BF16)
        vo_ref[:, sl] = jnp.where(bidx == 0, 1.0, v_ref[:, sl].astype(F32)).astype(BF16)


def _moba_gate(qk, plain, kmean_pad, heads):
    T = qk.shape[0]
    W = heads * LANES
    tq = MOBA_BLOCK
    assert T % tq == 0 and T // tq <= LANES - MOBA_HEAD_DIM
    col = lambda c: pl.BlockSpec((tq, W), lambda i: (i, c))
    return pl.pallas_call(
        functools.partial(_moba_gate_body, tq=tq, heads=heads),
        grid=(T // tq,),
        in_specs=[col(0), col(1), col(0),
                  pl.BlockSpec((heads, LANES, LANES), lambda i: (0, 0, 0))],
        out_specs=[col(0)] * 3,
        out_shape=[jax.ShapeDtypeStruct((T, W), BF16)] * 3,
        compiler_params=_cparams("parallel"),
        name="moba_gate",
    )(qk, qk, plain, kmean_pad)


def _flash_body(q_ref, k_ref, v_ref, o_ref, m_ref, acc_ref, *, tq, tk, l_lane):
    i = pl.program_id(1)
    n_full = (i * tq) // tk
    m_ref[...] = jnp.full(m_ref.shape, -jnp.inf, F32)
    acc_ref[...] = jnp.zeros(acc_ref.shape, F32)

    def step(j, masked):
        start = pl.multiple_of(j * tk, tk)
        for hh in range(2):
            hsl = slice(hh * LANES, (hh + 1) * LANES)
            s = lax.dot_general(q_ref[:, hsl], k_ref[pl.ds(start, tk), hsl],
                                (((1,), (1,)), ((), ())), preferred_element_type=F32)
            if masked:
                row = i * tq + lax.broadcasted_iota(jnp.int32, (tq, tk), 0)
                col = j * tk + lax.broadcasted_iota(jnp.int32, (tq, tk), 1)
                s = jnp.where(col <= row, s, -jnp.inf)
            m_prev = m_ref[hh]
            m_new = jnp.maximum(m_prev, jnp.max(s, axis=-1, keepdims=True))
            p = jnp.exp2(s - pltpu.repeat(m_new, tk // LANES, axis=1))
            acc_ref[hh] = jnp.exp2(m_prev - m_new) * acc_ref[hh] + jnp.dot(
                p.astype(BF16), v_ref[pl.ds(start, tk), hsl], preferred_element_type=F32)
            m_ref[hh] = m_new

    def loop_body(j, carry):
        step(j, False)
        return carry

    lax.fori_loop(0, n_full, loop_body, 0)
    for d in range(tq // tk):
        step(n_full + d, True)
    for hh in range(2):
        acc = acc_ref[hh]
        o_ref[:, hh * LANES:(hh + 1) * LANES] = (
            acc / acc[:, l_lane:l_lane + 1]).astype(o_ref.dtype)


def _flash(q, k, v, heads, *, q_off=0, k_off=0, v_off=0, l_lane, tq=1024, tk=1024):
    T = q.shape[0]
    tq, tk = min(tq, T), min(tk, T)
    assert heads % 2 == 0 and T % tq == 0 and tq % tk == 0
    pair = 2 * LANES
    return pl.pallas_call(
        functools.partial(_flash_body, tq=tq, tk=tk, l_lane=l_lane),
        grid=(heads // 2, T // tq),
        in_specs=[pl.BlockSpec((tq, pair), lambda h, i: (i, q_off + h)),
                  pl.BlockSpec((T, pair), lambda h, i: (0, k_off + h)),
                  pl.BlockSpec((T, pair), lambda h, i: (0, v_off + h))],
        out_specs=pl.BlockSpec((tq, pair), lambda h, i: (i, h)),
        out_shape=jax.ShapeDtypeStruct((T, heads * LANES), BF16),
        scratch_shapes=[pltpu.VMEM((2, tq, LANES), F32)] * 2,
        compiler_params=_cparams("parallel", "arbitrary"),
        name="flash",
    )(q, k, v)


def _retention_body(q_ref, k_ref, v_ref, g_ref, gn_ref, dm_ref, xi_ref, ze_ref, gc_ref,
                    o_ref, r_ref):
    @pl.when(pl.program_id(1) == 0)
    def _():
        r_ref[...] = jnp.zeros_like(r_ref)

    q = q_ref[...]
    k = k_ref[...]
    v = v_ref[...]
    r_old = r_ref[...]
    inner = lax.dot_general(q, k, (((1,), (1,)), ((), ())), preferred_element_type=F32) * dm_ref[0]
    out = jnp.dot(inner.astype(BF16), v, preferred_element_type=F32)
    out = out + jnp.dot(q, r_old.astype(BF16), preferred_element_type=F32) * xi_ref[0]
    kz = (k.astype(F32) * ze_ref[0]).T.astype(BF16)
    r_ref[...] = r_old * gc_ref[0] + jnp.dot(kz, v, preferred_element_type=F32)
    mu = jnp.mean(out, axis=-1, keepdims=True)
    cen = out - mu
    var = jnp.mean(cen * cen, axis=-1, keepdims=True)
    rn = cen * lax.rsqrt(var + EPS) * gn_ref[...]
    gate = g_ref[...].astype(F32)
    o_ref[...] = (rn * (gate / (1.0 + jnp.exp(-gate)))).astype(o_ref.dtype)


def _retention(qk, plain, gn, heads, *, q_off, k_off, v_off, g_off):
    T = qk.shape[0]
    W = heads * LANES
    C = min(RET_CHUNK, T)
    assert T % C == 0
    log_g = jnp.log(1.0 - 2.0 ** (-5.0 - jnp.arange(heads, dtype=F32)))
    pos = jnp.arange(C, dtype=F32)
    diff = pos[:, None] - pos[None, :]
    dmat = jnp.where(diff >= 0, jnp.exp(log_g[:, None, None] * jnp.maximum(diff, 0.0)), 0.0)
    rep = lambda t: jnp.broadcast_to(t[..., None], t.shape + (LANES,))
    xi = rep(jnp.exp(log_g[:, None] * (pos + 1.0)))
    zeta = rep(jnp.exp(log_g[:, None] * (C - 1.0 - pos)))
    g_chunk = rep(jnp.exp(log_g * C)[:, None])
    tile = lambda off: pl.BlockSpec((C, LANES), lambda h, c: (c, off + h))
    head_tab = lambda r: pl.BlockSpec((1, r, LANES), lambda h, c: (h, 0, 0))
    return pl.pallas_call(
        _retention_body,
        grid=(heads, T // C),
        in_specs=[tile(q_off), tile(k_off), tile(v_off), tile(g_off),
                  pl.BlockSpec((1, LANES), lambda h, c: (0, h)),
                  pl.BlockSpec((1, C, C), lambda h, c: (h, 0, 0)),
                  head_tab(C), head_tab(C), head_tab(1)],
        out_specs=tile(0),
        out_shape=jax.ShapeDtypeStruct((T, W), BF16),
        scratch_shapes=[pltpu.VMEM((LANES, LANES), F32)],
        compiler_params=_cparams("parallel", "arbitrary"),
        name="retention",
    )(qk, qk, plain, plain, gn.reshape(1, W).astype(F32), dmat, xi, zeta, g_chunk)


def _mla_mid_body(d_ref, qn_ref, kvn_ref, c_ref, s_ref, cq_ref, ckv_ref):
    def rms(x, g):
        return x * lax.rsqrt(jnp.mean(x * x, axis=-1, keepdims=True) + EPS) * g

    cq_ref[...] = rms(d_ref[:, :MLA_Q_RANK], qn_ref[...]).astype(BF16)
    lo = MLA_Q_RANK + MLA_KV_RANK
    ckv_ref[:, :MLA_KV_RANK] = rms(d_ref[:, MLA_Q_RANK:lo], kvn_ref[...]).astype(BF16)
    kr = d_ref[:, lo:lo + LANES] * c_ref[...] + d_ref[:, lo + LANES:lo + 2 * LANES] * s_ref[...]
    lane = lax.broadcasted_iota(jnp.int32, kr.shape, 1)
    ckv_ref[:, MLA_KV_RANK:] = jnp.where(lane == MLA_ROPE, 1.0, kr).astype(BF16)


def _mla_mid(down, q_norm, kv_norm, cos, sin, *, tm=256):
    T, W = down.shape
    wide = MLA_KV_RANK + LANES
    return pl.pallas_call(
        _mla_mid_body,
        grid=(T // tm,),
        in_specs=[pl.BlockSpec((tm, W), lambda i: (i, 0)),
                  pl.BlockSpec((1, MLA_Q_RANK), lambda i: (0, 0)),
                  pl.BlockSpec((1, MLA_KV_RANK), lambda i: (0, 0)),
                  pl.BlockSpec((tm, LANES), lambda i: (i, 0)),
                  pl.BlockSpec((tm, LANES), lambda i: (i, 0))],
        out_specs=[pl.BlockSpec((tm, MLA_Q_RANK), lambda i: (i, 0)),
                   pl.BlockSpec((tm, wide), lambda i: (i, 0))],
        out_shape=[jax.ShapeDtypeStruct((T, MLA_Q_RANK), BF16),
                   jax.ShapeDtypeStruct((T, wide), BF16)],
        compiler_params=_cparams("parallel"),
        name="mla_mid",
    )(down, q_norm.reshape(1, -1).astype(F32), kv_norm.reshape(1, -1).astype(F32), cos, sin)


def _peer_scores_body(x_ref, g_ref, wq_ref, kt_ref, xn_ref, st_ref):
    x = x_ref[...]
    xn = (x * lax.rsqrt(jnp.mean(x * x, axis=-1, keepdims=True) + EPS) * g_ref[...]).astype(BF16)
    xn_ref[...] = xn
    qry = jnp.dot(xn, wq_ref[...], preferred_element_type=F32).astype(BF16)
    st_ref[...] = lax.dot_general(kt_ref[...], qry, (((1,), (1,)), ((), ())),
                                  preferred_element_type=F32)


def _peer_scores(h, gain, wq, keys_t, *, tm=256):
    T, D = h.shape
    R = keys_t.shape[0]
    return pl.pallas_call(
        _peer_scores_body,
        grid=(T // tm,),
        in_specs=[pl.BlockSpec((tm, D), lambda i: (i, 0)),
                  pl.BlockSpec((1, D), lambda i: (0, 0)),
                  pl.BlockSpec(wq.shape, lambda i: (0, 0)),
                  pl.BlockSpec(keys_t.shape, lambda i: (0, 0))],
        out_specs=[pl.BlockSpec((tm, D), lambda i: (i, 0)),
                   pl.BlockSpec((R, tm), lambda i: (0, i))],
        out_shape=[jax.ShapeDtypeStruct((T, D), BF16), jax.ShapeDtypeStruct((R, T), F32)],
        compiler_params=_cparams("parallel"),
        name="peer_scores",
    )(h, gain.reshape(1, D).astype(F32), wq, keys_t)


PEER_CAND = [(i, PEER_TOPK // (i + 1)) for i in range(PEER_TOPK)]
PEER_NCAND = -(-sum(c for _, c in PEER_CAND) // 8) * 8
PEER_NORANK = 64.0


def _peer_topk_body(st_ref, pkf_ref, pkb_ref, t1_ref, t2_ref, cand_ref):
    n = PEER_NKEYS
    for h in range(PEER_HEADS):
        s1 = st_ref[(2 * h) * n:(2 * h + 1) * n, :]
        s2 = st_ref[(2 * h + 1) * n:(2 * h + 2) * n, :]
        vals = s1
        for r in range(PEER_TOPK):
            m = jnp.max(vals, axis=0, keepdims=True)
            t1_ref[r:r + 1, :] = m
            vals = jnp.where(vals == m, -jnp.inf, vals)
        vals = s2
        rank2 = jnp.full(s2.shape, PEER_NORANK, F32)
        for r in range(PEER_TOPK):
            m = jnp.max(vals, axis=0, keepdims=True)
            t2_ref[r:r + 1, :] = m
            hit = vals == m
            rank2 = jnp.where(hit, float(r), rank2)
            vals = jnp.where(hit, -jnp.inf, vals)
        cand_ref[...] = jnp.full(cand_ref.shape, -jnp.inf, F32)
        rowp = 0
        for i, cnt in PEER_CAND:
            cand_ref[rowp:rowp + cnt, :] = t1_ref[i:i + 1, :] + t2_ref[0:cnt, :]
            rowp += cnt
        c = cand_ref[...]
        top1 = t1_ref[0:1, :]
        top2 = t2_ref[0:1, :]
        cmax = top1 + top2
        z = jnp.zeros_like(cmax)
        for r in range(PEER_TOPK):
            kth = jnp.max(c, axis=0, keepdims=True)
            z = z + jnp.exp(kth - cmax)
            c = jnp.where(c == kth, -jnp.inf, c)
        cnt = jnp.zeros(s1.shape, F32)
        for j in range(PEER_TOPK // 2):
            cnt = cnt + jnp.where(s1 + t2_ref[j:j + 1, :] >= kth, 1.0, 0.0)
        cnt_best = jnp.zeros_like(top1)
        for j in range(PEER_TOPK):
            cnt_best = cnt_best + jnp.where(top1 + t2_ref[j:j + 1, :] >= kth, 1.0, 0.0)
        cnt = jnp.where(s1 == top1, cnt_best, cnt)
        pkf_ref[h, 0] = jnp.exp(s1 - top1) / z
        pkf_ref[h, 1] = cnt
        pkb_ref[h, 0] = rank2.astype(BF16)
        pkb_ref[h, 1] = jnp.exp(s2 - top2).astype(BF16)


def _peer_topk(st, *, tm=256):
    R, T = st.shape
    tm = min(tm, T)
    blk = lambda i: (0, 0, 0, i)
    shape = (PEER_HEADS, 2, PEER_NKEYS, T)
    return pl.pallas_call(
        _peer_topk_body,
        grid=(T // tm,),
        in_specs=[pl.BlockSpec((R, tm), lambda i: (0, i))],
        out_specs=[pl.BlockSpec((PEER_HEADS, 2, PEER_NKEYS, tm), blk)] * 2,
        out_shape=[jax.ShapeDtypeStruct(shape, F32), jax.ShapeDtypeStruct(shape, BF16)],
        scratch_shapes=[pltpu.VMEM((PEER_TOPK + 8, tm), F32),
                        pltpu.VMEM((PEER_TOPK + 8, tm), F32),
                        pltpu.VMEM((PEER_NCAND, tm), F32)],
        compiler_params=_cparams("parallel"),
        name="peer_topk",
    )(st)


def _peer_dense_body(xn_ref, u_ref, vt_ref, pkf_ref, pkb_ref, h_ref, o_ref,
                     ht0_ref, ht1_ref, acc_ref, *, tm, te, n_e, n_steps):
    s = pl.program_id(0)
    n = PEER_NKEYS
    group = 2

    def scores(dst_ref):
        dst_ref[...] = lax.dot_general(u_ref[...], xn_ref[...], (((1,), (1,)), ((), ())),
                                       preferred_element_type=F32)

    def experts(src_ref):
        e = lax.rem(s - 1, n_e)
        for gb in range(te // (group * n)):
            acts = []
            for ab in range(gb * group, (gb + 1) * group):
                a = e * (te // n) + ab
                gsum = jnp.zeros((n, tm), BF16)
                for h in range(PEER_HEADS):
                    w1 = jnp.broadcast_to(pkf_ref[h, 0, pl.ds(a, 1), :], (n, tm)).astype(BF16)
                    cnt = jnp.broadcast_to(pkf_ref[h, 1, pl.ds(a, 1), :], (n, tm)).astype(BF16)
                    gsum = gsum + jnp.where(pkb_ref[h, 0] < cnt, pkb_ref[h, 1], 0.0) * w1
                hs = src_ref[ab * n:(ab + 1) * n, :]
                act = 0.5 * hs * (1.0 + lax.erf(hs * np.float32(1.0 / np.sqrt(2.0))))
                acts.append(act.astype(BF16) * gsum)
            rows = slice(gb * group * n, (gb + 1) * group * n)
            acc_ref[...] += jnp.dot(vt_ref[:, rows], jnp.concatenate(acts, axis=0),
                                    preferred_element_type=F32)

    even = lax.rem(s, 2) == 0
    steady = (s > 0) & (s < n_steps)

    @pl.when((s >= 2) & (lax.rem(s - 1, n_e) == 0))
    def _():
        o_ref[...] = h_ref[...] + acc_ref[...].T
        acc_ref[...] = jnp.zeros_like(acc_ref)

    @pl.when(s == 0)
    def _():
        acc_ref[...] = jnp.zeros_like(acc_ref)
        scores(ht0_ref)

    @pl.when(steady & even)
    def _():
        scores(ht0_ref)
        experts(ht1_ref)

    @pl.when(steady & jnp.logical_not(even))
    def _():
        scores(ht1_ref)
        experts(ht0_ref)

    @pl.when(s == n_steps)
    def _():
        experts(ht1_ref if n_steps % 2 == 0 else ht0_ref)


def _peer_dense(xn, u, vt, pkf, pkb, h, *, tm=512, te=1024):
    T, D = xn.shape
    E = u.shape[0]
    tm = min(tm, T)
    assert T % tm == 0 and E % te == 0 and te % (2 * PEER_NKEYS) == 0
    n_e = E // te
    n_steps = (T // tm) * n_e
    pair = lambda s, lag: jnp.clip(s - lag, 0, n_steps - 1)
    cur = lambda s: pair(s, 0)
    prev = lambda s: pair(s, 1)
    done = lambda s: pair(s, 2)
    pk_spec = pl.BlockSpec((PEER_HEADS, 2, PEER_NKEYS, tm), lambda s: (0, 0, 0, prev(s) // n_e))
    return pl.pallas_call(
        functools.partial(_peer_dense_body, tm=tm, te=te, n_e=n_e, n_steps=n_steps),
        grid=(n_steps + 2,),
        in_specs=[pl.BlockSpec((tm, D), lambda s: (cur(s) // n_e, 0)),
                  pl.BlockSpec((te, D), lambda s: (cur(s) % n_e, 0)),
                  pl.BlockSpec((D, te), lambda s: (0, prev(s) % n_e)),
                  pk_spec, pk_spec,
                  pl.BlockSpec((tm, D), lambda s: (done(s) // n_e, 0))],
        out_specs=pl.BlockSpec((tm, D), lambda s: (done(s) // n_e, 0)),
        out_shape=jax.ShapeDtypeStruct((T, D), F32),
        scratch_shapes=[pltpu.VMEM((te, tm), F32), pltpu.VMEM((te, tm), F32),
                        pltpu.VMEM((D, tm), F32)],
        compiler_params=_cparams("arbitrary"),
        name="peer_dense",
    )(xn, u, vt, pkf, pkb, h)


def _peer_ffn(h, gain, wq, keys, u_tab, v_tab):
    nk, dh = PEER_NKEYS, PEER_DKEY // 2
    groups = PEER_HEADS * 2
    keys_t = jnp.einsum("gnd,gk->gnkd", keys.reshape(groups, nk, dh).astype(F32),
                        jnp.eye(groups, dtype=F32)).reshape(groups * nk, groups * dh).astype(BF16)
    xn, st = _peer_scores(h, gain, wq.astype(BF16), keys_t)
    pkf, pkb = _peer_topk(st)
    return _peer_dense(xn, u_tab.astype(BF16), v_tab.astype(BF16).T, pkf, pkb, h)


def _rmsnorm_body(x_ref, g_ref, o_ref):
    x = x_ref[...]
    o_ref[...] = x * lax.rsqrt(jnp.mean(x * x, axis=-1, keepdims=True) + EPS) * g_ref[...]


def _rmsnorm(x, g, *, tm=512):
    T, D = x.shape
    tm = min(tm, T)
    return pl.pallas_call(
        _rmsnorm_body,
        grid=(T // tm,),
        in_specs=[pl.BlockSpec((tm, D), lambda i: (i, 0)), pl.BlockSpec((1, D), lambda i: (0, 0))],
        out_specs=pl.BlockSpec((tm, D), lambda i: (i, 0)),
        out_shape=jax.ShapeDtypeStruct((T, D), F32),
        compiler_params=_cparams("parallel"),
        name="rmsnorm",
    )(x, g.reshape(1, D).astype(F32))


def _lane_tables(T, rot_dim, theta, *, rot_at, keep, scale):
    r = rot_dim // 2
    inv = 1.0 / (theta ** (jnp.arange(0, rot_dim, 2, dtype=F32) / rot_dim))
    lane = np.arange(LANES)
    in_rot = (lane >= rot_at) & (lane < rot_at + rot_dim)
    inv_lane = jnp.where(jnp.asarray(in_rot), inv[np.where(in_rot, (lane - rot_at) % r, 0)], 0.0)
    ang = jnp.arange(T, dtype=F32)[:, None] * inv_lane[None, :]
    c = jnp.cos(ang) * jnp.asarray((lane < keep) * scale, F32)[None, :]
    s = jnp.sin(ang) * jnp.asarray(in_rot * scale, F32)[None, :]
    return c, s


def _head_cols(n_heads, src_stride, src_off, width, *, dst_stride=LANES, dst_off=0):
    idx = np.zeros(n_heads * dst_stride, np.int32)
    sgn = np.zeros(n_heads * dst_stride, np.float32)
    for h in range(n_heads):
        d = h * dst_stride + dst_off
        idx[d:d + width] = h * src_stride + src_off + np.arange(width)
        sgn[d:d + width] = 1.0
    return idx, sgn


def _rot_cols(n_heads, src_stride, src_off, r, *, dst_stride=LANES, dst_off=0):
    idx = np.zeros(n_heads * dst_stride, np.int32)
    sgn = np.zeros(n_heads * dst_stride, np.float32)
    for h in range(n_heads):
        d = h * dst_stride + dst_off
        s = h * src_stride + src_off
        idx[d:d + r] = s + r + np.arange(r)
        sgn[d:d + r] = -1.0
        idx[d + r:d + 2 * r] = s + np.arange(r)
        sgn[d + r:d + 2 * r] = 1.0
    return idx, sgn


def _take_cols(w, idx_sgn):
    idx, sgn = idx_sgn
    return (jnp.take(w, jnp.asarray(idx), axis=1) * jnp.asarray(sgn)[None, :]).astype(BF16)


def _even_mixer(h, norm_g, w_in, ret_gn, w_o):
    T = h.shape[0]
    mw = MOBA_HEADS * MOBA_HEAD_DIM
    rw = RET_HEADS * RET_DK
    vw = RET_HEADS * RET_DV
    o_mq, o_mk, o_mv, o_rq, o_rk, o_rv, o_rg = np.cumsum([0, mw, mw, mw, rw, rw, vw])

    def seg(off, heads, stride, rot_dim):
        w = w_in[:, off:off + heads * stride]
        return (_take_cols(w, _head_cols(heads, stride, 0, stride)),
                _take_cols(w, _rot_cols(heads, stride, 0, rot_dim // 2)))

    segs = [seg(o_mq, MOBA_HEADS, MOBA_HEAD_DIM, MOBA_ROT), seg(o_mk, MOBA_HEADS, MOBA_HEAD_DIM, MOBA_ROT),
            seg(o_rq, RET_HEADS, RET_DK, RET_DK), seg(o_rk, RET_HEADS, RET_DK, RET_DK)]
    w_lin = jnp.concatenate([s[0] for s in segs], axis=1)
    w_rot = jnp.concatenate([s[1] for s in segs], axis=1)
    mtab = functools.partial(_lane_tables, T, MOBA_ROT, ROPE_THETA, rot_at=0, keep=MOBA_HEAD_DIM)
    rtab = functools.partial(_lane_tables, T, RET_DK, RET_THETA, rot_at=0, keep=RET_DK)
    tabs = [mtab(scale=MOBA_HEAD_DIM ** -0.5 * LOG2E), mtab(scale=1.0),
            rtab(scale=1.0), rtab(scale=RET_DK ** -0.5)]
    cos = jnp.concatenate([t[0] for t in tabs], axis=1)
    sin = jnp.concatenate([t[1] for t in tabs], axis=1)
    seg_w = MOBA_HEADS * LANES
    qk, colmean = _proj(h, w_lin, gain=norm_g, w_rot=w_rot, cos=cos, sin=sin, seg=seg_w,
                        colmean=MOBA_BLOCK)
    w_plain = jnp.concatenate(
        [_take_cols(w_in[:, o_mv:o_rq], _head_cols(MOBA_HEADS, MOBA_HEAD_DIM, 0, MOBA_HEAD_DIM)),
         w_in[:, o_rv:].astype(BF16)], axis=1)
    plain = _proj(h, w_plain, gain=norm_g)
    nb = T // MOBA_BLOCK
    km = colmean[:, 0, seg_w:2 * seg_w].reshape(nb, MOBA_HEADS, LANES).transpose(1, 0, 2)
    km = jnp.pad(km, ((0, 0), (MOBA_HEAD_DIM, LANES - MOBA_HEAD_DIM - nb), (0, 0)))
    mq_b, mk_b, mv_b = _moba_gate(qk, plain, km, MOBA_HEADS)
    a_out = _flash(mq_b, mk_b, mv_b, MOBA_HEADS, l_lane=MOBA_HEAD_DIM)
    b_out = _retention(qk, plain, ret_gn, RET_HEADS, q_off=2 * RET_HEADS, k_off=3 * RET_HEADS,
                       v_off=MOBA_HEADS, g_off=MOBA_HEADS + RET_HEADS)
    w_oa = jnp.zeros((MOBA_HEADS, LANES, w_o.shape[1]), BF16).at[:, :MOBA_HEAD_DIM].set(
        w_o[:mw].astype(BF16).reshape(MOBA_HEADS, MOBA_HEAD_DIM, -1)).reshape(seg_w, -1)
    return _matmul_res([a_out, b_out], [w_oa, w_o[mw:].astype(BF16)], h)


def _odd_mixer(h, norm_g, w_down, q_norm, w_uq, kv_norm, w_ukv, w_o):
    T = h.shape[0]
    lat = MLA_Q_RANK + MLA_KV_RANK
    half = MLA_ROPE // 2
    dq = MLA_NOPE + MLA_ROPE
    kw = MLA_HEADS * LANES
    w_dn = jnp.concatenate(
        [w_down[:, :lat].astype(BF16),
         _take_cols(w_down[:, lat:], _head_cols(1, MLA_ROPE, 0, MLA_ROPE)),
         _take_cols(w_down[:, lat:], _rot_cols(1, MLA_ROPE, 0, half))], axis=1)
    down = _proj(h, w_dn, gain=norm_g, out_dtype=F32)
    ck, sk = _lane_tables(T, MLA_ROPE, ROPE_THETA, rot_at=0, keep=MLA_ROPE, scale=1.0)
    cqn, ckvx = _mla_mid(down, q_norm, kv_norm, ck, sk)
    cq_t, sq_t = _lane_tables(T, MLA_ROPE, ROPE_THETA, rot_at=MLA_NOPE, keep=dq,
                              scale=dq ** -0.5 * LOG2E)
    q = _proj(cqn, _take_cols(w_uq, _head_cols(MLA_HEADS, dq, 0, dq)),
              w_rot=_take_cols(w_uq, _rot_cols(MLA_HEADS, dq, MLA_NOPE, half, dst_off=MLA_NOPE)),
              cos=cq_t, sin=sq_t, seg=kw)
    kvw = MLA_NOPE + MLA_V
    place_k = np.zeros((LANES, kw), np.float32)
    place_v = np.zeros((LANES, kw), np.float32)
    for hh in range(MLA_HEADS):
        place_k[np.arange(MLA_ROPE), hh * LANES + MLA_NOPE + np.arange(MLA_ROPE)] = 1.0
        place_v[MLA_ROPE, hh * LANES + MLA_V] = 1.0
    wk = jnp.concatenate([_take_cols(w_ukv, _head_cols(MLA_HEADS, kvw, 0, MLA_NOPE)),
                          jnp.asarray(place_k, BF16)], axis=0)
    wv = jnp.concatenate([_take_cols(w_ukv, _head_cols(MLA_HEADS, kvw, MLA_NOPE, MLA_V)),
                          jnp.asarray(place_v, BF16)], axis=0)
    kv = _proj(ckvx, jnp.concatenate([wk, wv], axis=1))
    o = _flash(q, kv, kv, MLA_HEADS, v_off=MLA_HEADS // 2, l_lane=MLA_V)
    w_oa = jnp.zeros((MLA_HEADS, LANES, w_o.shape[1]), BF16).at[:, :MLA_V].set(
        w_o.astype(BF16).reshape(MLA_HEADS, MLA_V, -1)).reshape(kw, -1)
    return _matmul_res([o], [w_oa], h)


def kernel(x, attn_norm, ffn_norm, ev_w_in, ev_ret_gn, ev_w_o, od_w_down, od_q_norm, od_w_uq,
           od_kv_norm, od_w_ukv, od_w_o, peer_wq, peer_keys, peer_u, peer_v, final_norm):
    B, S, D = x.shape
    assert B == 1
    h = x.reshape(S, D)
    depth = attn_norm.shape[0]
    for i in range(depth):
        j = i // 2
        if i % 2 == 0:
            h = _even_mixer(h, attn_norm[i], ev_w_in[j], ev_ret_gn[j], ev_w_o[j])
        else:
            h = _odd_mixer(h, attn_norm[i], od_w_down[j], od_q_norm[j], od_w_uq[j],
                           od_kv_norm[j], od_w_ukv[j], od_w_o[j])
        h = _peer_ffn(h, ffn_norm[i], peer_wq[i], peer_keys[i], peer_u[i], peer_v[i])
    return _rmsnorm(h, final_norm).reshape(B, S, D)
```

```python
import functools

import numpy as np
import jax
import jax.numpy as jnp
from jax import lax
from jax.experimental import pallas as pl
from jax.experimental.pallas import tpu as pltpu

F32 = jnp.float32
BF16 = jnp.bfloat16

LANES = 128
VMEM_LIMIT = 56 * 1024 * 1024

D_MODEL = 1024
EPS = 1e-6
ROPE_THETA = 500000.0
LOG2E = float(np.log2(np.e))

MOBA_HEADS = 8
MOBA_HEAD_DIM = 64
MOBA_ROT = MOBA_HEAD_DIM // 4
MOBA_BLOCK = 256
MOBA_TOPK = 3
MASK_BIAS = -1e9

RET_HEADS = 8
RET_DK = 64
RET_DV = 128
RET_THETA = 10000.0
RET_CHUNK = 512

MLA_HEADS = 16
MLA_NOPE = 64
MLA_ROPE = 32
MLA_V = 64
MLA_Q_RANK = 512
MLA_KV_RANK = 256

PEER_HEADS = 8
PEER_NKEYS = 128
PEER_DKEY = 128
PEER_TOPK = 16


def _cparams(*sem):
    return pltpu.CompilerParams(dimension_semantics=sem, vmem_limit_bytes=VMEM_LIMIT)


def _proj_body(*refs, norm, rope, colmean, tn):
    it = iter(refs)
    x_ref = next(it)
    g_ref = next(it) if norm else None
    w_ref = next(it)
    if rope:
        wr_ref, c_ref, s_ref = next(it), next(it), next(it)
    o_ref = next(it)
    cm_ref = next(it) if colmean else None
    xn_ref = next(it)

    @pl.when(pl.program_id(1) == 0)
    def _():
        x = x_ref[...].astype(F32)
        if norm:
            x = x * lax.rsqrt(jnp.mean(x * x, axis=-1, keepdims=True) + EPS) * g_ref[...]
        xn_ref[...] = x.astype(BF16)

    xn = xn_ref[...]
    y = jnp.dot(xn, w_ref[...], preferred_element_type=F32)
    if not rope:
        o_ref[...] = y.astype(o_ref.dtype)
        return
    yr = jnp.dot(xn, wr_ref[...], preferred_element_type=F32)
    c = c_ref[...]
    s = s_ref[...]
    for k in range(tn // LANES):
        sl = slice(k * LANES, (k + 1) * LANES)
        val = y[:, sl] * c + yr[:, sl] * s
        o_ref[:, sl] = val.astype(o_ref.dtype)
        if colmean:
            for b in range(val.shape[0] // colmean):
                cm_ref[b, :, sl] = jnp.mean(val[b * colmean:(b + 1) * colmean], axis=0,
                                            keepdims=True)


def _proj(x, w, *, gain=None, w_rot=None, cos=None, sin=None, seg=None,
          colmean=None, out_dtype=BF16, tm=1024, tn=512):
    T, K = x.shape
    N = w.shape[1]
    tm, tn = min(tm, T), min(tn, N)
    assert T % tm == 0 and N % tn == 0 and tn % LANES == 0
    norm, rope = gain is not None, w_rot is not None
    in_specs = [pl.BlockSpec((tm, K), lambda i, j: (i, 0))]
    args = [x]
    if norm:
        in_specs.append(pl.BlockSpec((1, K), lambda i, j: (0, 0)))
        args.append(gain.reshape(1, K).astype(F32))
    in_specs.append(pl.BlockSpec((K, tn), lambda i, j: (0, j)))
    args.append(w)
    if rope:
        assert seg % tn == 0
        tab = lambda i, j: (i, (j * tn) // seg)
        in_specs += [pl.BlockSpec((K, tn), lambda i, j: (0, j)),
                     pl.BlockSpec((tm, LANES), tab), pl.BlockSpec((tm, LANES), tab)]
        args += [w_rot, cos, sin]
    out_shape = [jax.ShapeDtypeStruct((T, N), out_dtype)]
    out_specs = [pl.BlockSpec((tm, tn), lambda i, j: (i, j))]
    if colmean:
        assert rope and tm % colmean == 0
        out_shape.append(jax.ShapeDtypeStruct((T // colmean, 1, N), F32))
        out_specs.append(pl.BlockSpec((tm // colmean, 1, tn), lambda i, j: (i, 0, j)))
    res = pl.pallas_call(
        functools.partial(_proj_body, norm=norm, rope=rope, colmean=colmean, tn=tn),
        grid=(T // tm, N // tn),
        in_specs=in_specs, out_specs=out_specs, out_shape=out_shape,
        scratch_shapes=[pltpu.VMEM((tm, K), BF16)],
        compiler_params=_cparams("parallel", "arbitrary"),
        name="proj",
    )(*args)
    return res if colmean else res[0]


def _matmul_res_body(*refs, n_in):
    xs, ws = refs[:n_in], refs[n_in:2 * n_in]
    r_ref, o_ref = refs[2 * n_in], refs[2 * n_in + 1]
    acc = r_ref[...]
    for x_ref, w_ref in zip(xs, ws):
        acc = acc + jnp.dot(x_ref[...], w_ref[...], preferred_element_type=F32)
    o_ref[...] = acc


def _matmul_res(xs, ws, res, *, tm=512, tn=512):
    T, N = res.shape
    n_in = len(xs)
    in_specs = [pl.BlockSpec((tm, x.shape[1]), lambda i, j: (i, 0)) for x in xs]
    in_specs += [pl.BlockSpec((w.shape[0], tn), lambda i, j: (0, j)) for w in ws]
    in_specs.append(pl.BlockSpec((tm, tn), lambda i, j: (i, j)))
    return pl.pallas_call(
        functools.partial(_matmul_res_body, n_in=n_in),
        grid=(T // tm, N // tn),
        in_specs=in_specs,
        out_specs=pl.BlockSpec((tm, tn), lambda i, j: (i, j)),
        out_shape=jax.ShapeDtypeStruct((T, N), F32),
        compiler_params=_cparams("parallel", "parallel"),
        name="matmul_res",
    )(*xs, *ws, res)


def _moba_gate_body(q_ref, k_ref, v_ref, km_ref, qo_ref, ko_ref, vo_ref, *, tq, heads):
    blk = pl.program_id(0)
    lane = lax.broadcasted_iota(jnp.int32, (tq, LANES), 1)
    bidx = lane - MOBA_HEAD_DIM
    for h in range(heads):
        sl = slice(h * LANES, (h + 1) * LANES)
        q = q_ref[:, sl].astype(F32)
        gate = lax.dot_general(q, km_ref[h], (((1,), (1,)), ((), ())),
                               precision=lax.Precision.HIGHEST, preferred_element_type=F32)
        g = jnp.where((bidx >= 0) & (bidx < blk), gate, -jnp.inf)
        sel = bidx == blk
        for _ in range(MOBA_TOPK):
            m = jnp.max(g, axis=-1, keepdims=True)
            idx = jnp.min(jnp.where(g == m, lane, 2 * LANES), axis=-1, keepdims=True)
            pick = (lane == idx) & (m > -jnp.inf)
            sel = sel | pick
            g = jnp.where(pick, -jnp.inf, g)
        bias = jnp.where(sel, 0.0, MASK_BIAS)
        qo_ref[:, sl] = jnp.where(bidx < 0, q, bias).astype(BF16)
        ko_ref[:, sl] = jnp.where(bidx == blk, 1.0, k_ref[:, sl].astype(F32)).astype(BF16)
        vo_ref[:, sl] = jnp.where(bidx == 0, 1.0, v_ref[:, sl].astype(F32)).astype(BF16)


def _moba_gate(qk, plain, kmean_pad, heads):
    T = qk.shape[0]
    W = heads * LANES
    tq = MOBA_BLOCK
    assert T % tq == 0 and T // tq <= LANES - MOBA_HEAD_DIM
    col = lambda c: pl.BlockSpec((tq, W), lambda i: (i, c))
    return pl.pallas_call(
        functools.partial(_moba_gate_body, tq=tq, heads=heads),
        grid=(T // tq,),
        in_specs=[col(0), col(1), col(0),
                  pl.BlockSpec((heads, LANES, LANES), lambda i: (0, 0, 0))],
        out_specs=[col(0)] * 3,
        out_shape=[jax.ShapeDtypeStruct((T, W), BF16)] * 3,
        compiler_params=_cparams("parallel"),
        name="moba_gate",
    )(qk, qk, plain, kmean_pad)


def _flash_body(q_ref, k_ref, v_ref, o_ref, m_ref, acc_ref, *, tq, tk, l_lane):
    i = pl.program_id(1)
    n_full = (i * tq) // tk
    m_ref[...] = jnp.full(m_ref.shape, -jnp.inf, F32)
    acc_ref[...] = jnp.zeros(acc_ref.shape, F32)

    def step(j, masked):
        start = pl.multiple_of(j * tk, tk)
        for hh in range(2):
            hsl = slice(hh * LANES, (hh + 1) * LANES)
            s = lax.dot_general(q_ref[:, hsl], k_ref[pl.ds(start, tk), hsl],
                                (((1,), (1,)), ((), ())), preferred_element_type=F32)
            if masked:
                row = i * tq + lax.broadcasted_iota(jnp.int32, (tq, tk), 0)
                col = j * tk + lax.broadcasted_iota(jnp.int32, (tq, tk), 1)
                s = jnp.where(col <= row, s, -jnp.inf)
            m_prev = m_ref[hh]
            m_new = jnp.maximum(m_prev, jnp.max(s, axis=-1, keepdims=True))
            p = jnp.exp2(s - pltpu.repeat(m_new, tk // LANES, axis=1))
            acc_ref[hh] = jnp.exp2(m_prev - m_new) * acc_ref[hh] + jnp.dot(
                p.astype(BF16), v_ref[pl.ds(start, tk), hsl], preferred_element_type=F32)
            m_ref[hh] = m_new

    def loop_body(j, carry):
        step(j, False)
        return carry

    lax.fori_loop(0, n_full, loop_body, 0)
    for d in range(tq // tk):
        step(n_full + d, True)
    for hh in range(2):
        acc = acc_ref[hh]
        o_ref[:, hh * LANES:(hh + 1) * LANES] = (
            acc / acc[:, l_lane:l_lane + 1]).astype(o_ref.dtype)


def _flash(q, k, v, heads, *, q_off=0, k_off=0, v_off=0, l_lane, tq=1024, tk=1024):
    T = q.shape[0]
    tq, tk = min(tq, T), min(tk, T)
    assert heads % 2 == 0 and T % tq == 0 and tq % tk == 0
    pair = 2 * LANES
    return pl.pallas_call(
        functools.partial(_flash_body, tq=tq, tk=tk, l_lane=l_lane),
        grid=(heads // 2, T // tq),
        in_specs=[pl.BlockSpec((tq, pair), lambda h, i: (i, q_off + h)),
                  pl.BlockSpec((T, pair), lambda h, i: (0, k_off + h)),
                  pl.BlockSpec((T, pair), lambda h, i: (0, v_off + h))],
        out_specs=pl.BlockSpec((tq, pair), lambda h, i: (i, h)),
        out_shape=jax.ShapeDtypeStruct((T, heads * LANES), BF16),
        scratch_shapes=[pltpu.VMEM((2, tq, LANES), F32)] * 2,
        compiler_params=_cparams("parallel", "arbitrary"),
        name="flash",
    )(q, k, v)


def _retention_body(q_ref, k_ref, v_ref, g_ref, gn_ref, dm_ref, xi_ref, ze_ref, gc_ref,
                    o_ref, r_ref):
    @pl.when(pl.program_id(1) == 0)
    def _():
        r_ref[...] = jnp.zeros_like(r_ref)

    q = q_ref[...]
    k = k_ref[...]
    v = v_ref[...]
    r_old = r_ref[...]
    inner = lax.dot_general(q, k, (((1,), (1,)), ((), ())), preferred_element_type=F32) * dm_ref[0]
    out = jnp.dot(inner.astype(BF16), v, preferred_element_type=F32)
    out = out + jnp.dot(q, r_old.astype(BF16), preferred_element_type=F32) * xi_ref[0]
    kz = (k.astype(F32) * ze_ref[0]).T.astype(BF16)
    r_ref[...] = r_old * gc_ref[0] + jnp.dot(kz, v, preferred_element_type=F32)
    mu = jnp.mean(out, axis=-1, keepdims=True)
    cen = out - mu
    var = jnp.mean(cen * cen, axis=-1, keepdims=True)
    rn = cen * lax.rsqrt(var + EPS) * gn_ref[...]
    gate = g_ref[...].astype(F32)
    o_ref[...] = (rn * (gate / (1.0 + jnp.exp(-gate)))).astype(o_ref.dtype)


def _retention(qk, plain, gn, heads, *, q_off, k_off, v_off, g_off):
    T = qk.shape[0]
    W = heads * LANES
    C = min(RET_CHUNK, T)
    assert T % C == 0
    log_g = jnp.log(1.0 - 2.0 ** (-5.0 - jnp.arange(heads, dtype=F32)))
    pos = jnp.arange(C, dtype=F32)
    diff = pos[:, None] - pos[None, :]
    dmat = jnp.where(diff >= 0, jnp.exp(log_g[:, None, None] * jnp.maximum(diff, 0.0)), 0.0)
    rep = lambda t: jnp.broadcast_to(t[..., None], t.shape + (LANES,))
    xi = rep(jnp.exp(log_g[:, None] * (pos + 1.0)))
    zeta = rep(jnp.exp(log_g[:, None] * (C - 1.0 - pos)))
    g_chunk = rep(jnp.exp(log_g * C)[:, None])
    tile = lambda off: pl.BlockSpec((C, LANES), lambda h, c: (c, off + h))
    head_tab = lambda r: pl.BlockSpec((1, r, LANES), lambda h, c: (h, 0, 0))
    return pl.pallas_call(
        _retention_body,
        grid=(heads, T // C),
        in_specs=[tile(q_off), tile(k_off), tile(v_off), tile(g_off),
                  pl.BlockSpec((1, LANES), lambda h, c: (0, h)),
                  pl.BlockSpec((1, C, C), lambda h, c: (h, 0, 0)),
                  head_tab(C), head_tab(C), head_tab(1)],
        out_specs=tile(0),
        out_shape=jax.ShapeDtypeStruct((T, W), BF16),
        scratch_shapes=[pltpu.VMEM((LANES, LANES), F32)],
        compiler_params=_cparams("parallel", "arbitrary"),
        name="retention",
    )(qk, qk, plain, plain, gn.reshape(1, W).astype(F32), dmat, xi, zeta, g_chunk)


def _mla_mid_body(d_ref, qn_ref, kvn_ref, c_ref, s_ref, cq_ref, ckv_ref):
    def rms(x, g):
        return x * lax.rsqrt(jnp.mean(x * x, axis=-1, keepdims=True) + EPS) * g

    cq_ref[...] = rms(d_ref[:, :MLA_Q_RANK], qn_ref[...]).astype(BF16)
    lo = MLA_Q_RANK + MLA_KV_RANK
    ckv_ref[:, :MLA_KV_RANK] = rms(d_ref[:, MLA_Q_RANK:lo], kvn_ref[...]).astype(BF16)
    kr = d_ref[:, lo:lo + LANES] * c_ref[...] + d_ref[:, lo + LANES:lo + 2 * LANES] * s_ref[...]
    lane = lax.broadcasted_iota(jnp.int32, kr.shape, 1)
    ckv_ref[:, MLA_KV_RANK:] = jnp.where(lane == MLA_ROPE, 1.0, kr).astype(BF16)


def _mla_mid(down, q_norm, kv_norm, cos, sin, *, tm=256):
    T, W = down.shape
    wide = MLA_KV_RANK + LANES
    return pl.pallas_call(
        _mla_mid_body,
        grid=(T // tm,),
        in_specs=[pl.BlockSpec((tm, W), lambda i: (i, 0)),
                  pl.BlockSpec((1, MLA_Q_RANK), lambda i: (0, 0)),
                  pl.BlockSpec((1, MLA_KV_RANK), lambda i: (0, 0)),
                  pl.BlockSpec((tm, LANES), lambda i: (i, 0)),
                  pl.BlockSpec((tm, LANES), lambda i: (i, 0))],
        out_specs=[pl.BlockSpec((tm, MLA_Q_RANK), lambda i: (i, 0)),
                   pl.BlockSpec((tm, wide), lambda i: (i, 0))],
        out_shape=[jax.ShapeDtypeStruct((T, MLA_Q_RANK), BF16),
                   jax.ShapeDtypeStruct((T, wide), BF16)],
        compiler_params=_cparams("parallel"),
        name="mla_mid",
    )(down, q_norm.reshape(1, -1).astype(F32), kv_norm.reshape(1, -1).astype(F32), cos, sin)


def _peer_scores_body(x_ref, g_ref, wq_ref, kt_ref, xn_ref, st_ref):
    x = x_ref[...]
    xn = (x * lax.rsqrt(jnp.mean(x * x, axis=-1, keepdims=True) + EPS) * g_ref[...]).astype(BF16)
    xn_ref[...] = xn
    qry = jnp.dot(xn, wq_ref[...], preferred_element_type=F32).astype(BF16)
    st_ref[...] = lax.dot_general(kt_ref[...], qry, (((1,), (1,)), ((), ())),
                                  preferred_element_type=F32)


def _peer_scores(h, gain, wq, keys_t, *, tm=256):
    T, D = h.shape
    R = keys_t.shape[0]
    return pl.pallas_call(
        _peer_scores_body,
        grid=(T // tm,),
        in_specs=[pl.BlockSpec((tm, D), lambda i: (i, 0)),
                  pl.BlockSpec((1, D), lambda i: (0, 0)),
                  pl.BlockSpec(wq.shape, lambda i: (0, 0)),
                  pl.BlockSpec(keys_t.shape, lambda i: (0, 0))],
        out_specs=[pl.BlockSpec((tm, D), lambda i: (i, 0)),
                   pl.BlockSpec((R, tm), lambda i: (0, i))],
        out_shape=[jax.ShapeDtypeStruct((T, D), BF16), jax.ShapeDtypeStruct((R, T), F32)],
        compiler_params=_cparams("parallel"),
        name="peer_scores",
    )(h, gain.reshape(1, D).astype(F32), wq, keys_t)


PEER_CAND = [(i, PEER_TOPK // (i + 1)) for i in range(PEER_TOPK)]
PEER_NCAND = -(-sum(c for _, c in PEER_CAND) // 8) * 8
PEER_NORANK = 64.0


def _peer_topk_body(st_ref, pkf_ref, pkb_ref, t1_ref, t2_ref, cand_ref):
    n = PEER_NKEYS
    for h in range(PEER_HEADS):
        s1 = st_ref[(2 * h) * n:(2 * h + 1) * n, :]
        s2 = st_ref[(2 * h + 1) * n:(2 * h + 2) * n, :]
        vals = s1
        for r in range(PEER_TOPK):
            m = jnp.max(vals, axis=0, keepdims=True)
            t1_ref[r:r + 1, :] = m
            vals = jnp.where(vals == m, -jnp.inf, vals)
        vals = s2
        rank2 = jnp.full(s2.shape, PEER_NORANK, F32)
        for r in range(PEER_TOPK):
            m = jnp.max(vals, axis=0, keepdims=True)
            t2_ref[r:r + 1, :] = m
            hit = vals == m
            rank2 = jnp.where(hit, float(r), rank2)
            vals = jnp.where(hit, -jnp.inf, vals)
        cand_ref[...] = jnp.full(cand_ref.shape, -jnp.inf, F32)
        rowp = 0
        for i, cnt in PEER_CAND:
            cand_ref[rowp:rowp + cnt, :] = t1_ref[i:i + 1, :] + t2_ref[0:cnt, :]
            rowp += cnt
        c = cand_ref[...]
        top1 = t1_ref[0:1, :]
        top2 = t2_ref[0:1, :]
        cmax = top1 + top2
        z = jnp.zeros_like(cmax)
        for r in range(PEER_TOPK):
            kth = jnp.max(c, axis=0, keepdims=True)
            z = z + jnp.exp(kth - cmax)
            c = jnp.where(c == kth, -jnp.inf, c)
        cnt = jnp.zeros(s1.shape, F32)
        for j in range(PEER_TOPK // 2):
            cnt = cnt + jnp.where(s1 + t2_ref[j:j + 1, :] >= kth, 1.0, 0.0)
        cnt_best = jnp.zeros_like(top1)
        for j in range(PEER_TOPK):
            cnt_best = cnt_best + jnp.where(top1 + t2_ref[j:j + 1, :] >= kth, 1.0, 0.0)
        cnt = jnp.where(s1 == top1, cnt_best, cnt)
        pkf_ref[h, 0] = jnp.exp(s1 - top1) / z
        pkf_ref[h, 1] = cnt
        pkb_ref[h, 0] = rank2.astype(BF16)
        pkb_ref[h, 1] = jnp.exp(s2 - top2).astype(BF16)


def _peer_topk(st, *, tm=256):
    R, T = st.shape
    tm = min(tm, T)
    blk = lambda i: (0, 0, 0, i)
    shape = (PEER_HEADS, 2, PEER_NKEYS, T)
    return pl.pallas_call(
        _peer_topk_body,
        grid=(T // tm,),
        in_specs=[pl.BlockSpec((R, tm), lambda i: (0, i))],
        out_specs=[pl.BlockSpec((PEER_HEADS, 2, PEER_NKEYS, tm), blk)] * 2,
        out_shape=[jax.ShapeDtypeStruct(shape, F32), jax.ShapeDtypeStruct(shape, BF16)],
        scratch_shapes=[pltpu.VMEM((PEER_TOPK + 8, tm), F32),
                        pltpu.VMEM((PEER_TOPK + 8, tm), F32),
                        pltpu.VMEM((PEER_NCAND, tm), F32)],
        compiler_params=_cparams("parallel"),
        name="peer_topk",
    )(st)


def _peer_dense_body(xn_ref, u_ref, vt_ref, pkf_ref, pkb_ref, h_ref, o_ref,
                     ht0_ref, ht1_ref, acc_ref, *, tm, te, n_e, n_steps):
    s = pl.program_id(0)
    n = PEER_NKEYS
    group = 2

    def scores(dst_ref):
        dst_ref[...] = lax.dot_general(u_ref[...], xn_ref[...], (((1,), (1,)), ((), ())),
                                       preferred_element_type=F32)

    def scores_half(dst_ref, half, anchor):
        hw = tm // 2
        bits = pltpu.bitcast(anchor[0:16, 0:LANES], jnp.uint32)
        zero = ((bits >> 16) >> 16)[0, 0].astype(jnp.int32)
        xs = xn_ref[pl.ds(pl.multiple_of(half * hw + zero * hw, hw), hw), :]
        dst_ref[:, half * hw:(half + 1) * hw] = lax.dot_general(
            u_ref[...], xs, (((1,), (1,)), ((), ())), preferred_element_type=F32)

    def experts(src_ref, dst_ref=None):
        e = lax.rem(s - 1, n_e)
        n_groups = te // (group * n)
        for gb in range(n_groups):
            if dst_ref is not None and gb in (1, n_groups // 2 + 1):
                scores_half(dst_ref, int(gb > 1), acts[0])
            acts = []
            for ab in range(gb * group, (gb + 1) * group):
                a = e * (te // n) + ab
                gsum = jnp.zeros((n, tm), BF16)
                for h in range(PEER_HEADS):
                    w1 = jnp.broadcast_to(pkf_ref[h, 0, pl.ds(a, 1), :], (n, tm)).astype(BF16)
                    cnt = jnp.broadcast_to(pkf_ref[h, 1, pl.ds(a, 1), :], (n, tm)).astype(BF16)
                    gsum = gsum + jnp.where(pkb_ref[h, 0] < cnt, pkb_ref[h, 1], 0.0) * w1
                hs = src_ref[ab * n:(ab + 1) * n, :]
                act = 0.5 * hs * (1.0 + lax.erf(hs * np.float32(1.0 / np.sqrt(2.0))))
                acts.append(act.astype(BF16) * gsum)
            rows = slice(gb * group * n, (gb + 1) * group * n)
            acc_ref[...] += jnp.dot(vt_ref[0, :, rows], jnp.concatenate(acts, axis=0),
                                    preferred_element_type=F32)

    even = lax.rem(s, 2) == 0
    steady = (s > 0) & (s < n_steps)

    @pl.when((s >= 2) & (lax.rem(s - 1, n_e) == 0))
    def _():
        o_ref[...] = h_ref[...] + acc_ref[...].T
        acc_ref[...] = jnp.zeros_like(acc_ref)

    @pl.when(s == 0)
    def _():
        acc_ref[...] = jnp.zeros_like(acc_ref)
        scores(ht0_ref)

    @pl.when(steady & even)
    def _():
        experts(ht1_ref, ht0_ref)

    @pl.when(steady & jnp.logical_not(even))
    def _():
        experts(ht0_ref, ht1_ref)

    @pl.when(s == n_steps)
    def _():
        experts(ht1_ref if n_steps % 2 == 0 else ht0_ref)


def _peer_dense(xn, u, v, pkf, pkb, h, *, tm=512, te=1024):
    T, D = xn.shape
    E = u.shape[0]
    tm = min(tm, T)
    assert T % tm == 0 and E % te == 0 and te % (2 * PEER_NKEYS) == 0
    n_e = E // te
    vt = v.reshape(n_e, te, D).transpose(0, 2, 1)
    n_steps = (T // tm) * n_e
    pair = lambda s, lag: jnp.clip(s - lag, 0, n_steps - 1)
    cur = lambda s: pair(s, 0)
    prev = lambda s: pair(s, 1)
    done = lambda s: pair(s, 2)
    pk_spec = pl.BlockSpec((PEER_HEADS, 2, PEER_NKEYS, tm), lambda s: (0, 0, 0, prev(s) // n_e))
    return pl.pallas_call(
        functools.partial(_peer_dense_body, tm=tm, te=te, n_e=n_e, n_steps=n_steps),
        grid=(n_steps + 2,),
        in_specs=[pl.BlockSpec((tm, D), lambda s: (cur(s) // n_e, 0)),
                  pl.BlockSpec((te, D), lambda s: (cur(s) % n_e, 0)),
                  pl.BlockSpec((1, D, te), lambda s: (prev(s) % n_e, 0, 0)),
                  pk_spec, pk_spec,
                  pl.BlockSpec((tm, D), lambda s: (done(s) // n_e, 0))],
        out_specs=pl.BlockSpec((tm, D), lambda s: (done(s) // n_e, 0)),
        out_shape=jax.ShapeDtypeStruct((T, D), F32),
        scratch_shapes=[pltpu.VMEM((te, tm), F32), pltpu.VMEM((te, tm), F32),
                        pltpu.VMEM((D, tm), F32)],
        compiler_params=_cparams("arbitrary"),
        name="peer_dense",
    )(xn, u, vt, pkf, pkb, h)


def _peer_ffn(h, gain, wq, keys, u_tab, v_tab):
    nk, dh = PEER_NKEYS, PEER_DKEY // 2
    groups = PEER_HEADS * 2
    keys_t = jnp.einsum("gnd,gk->gnkd", keys.reshape(groups, nk, dh).astype(F32),
                        jnp.eye(groups, dtype=F32)).reshape(groups * nk, groups * dh).astype(BF16)
    xn, st = _peer_scores(h, gain, wq.astype(BF16), keys_t)
    pkf, pkb = _peer_topk(st)
    return _peer_dense(xn, u_tab.astype(BF16), v_tab.astype(BF16), pkf, pkb, h)


def _rmsnorm_body(x_ref, g_ref, o_ref):
    x = x_ref[...]
    o_ref[...] = x * lax.rsqrt(jnp.mean(x * x, axis=-1, keepdims=True) + EPS) * g_ref[...]


def _rmsnorm(x, g, *, tm=512):
    T, D = x.shape
    tm = min(tm, T)
    return pl.pallas_call(
        _rmsnorm_body,
        grid=(T // tm,),
        in_specs=[pl.BlockSpec((tm, D), lambda i: (i, 0)), pl.BlockSpec((1, D), lambda i: (0, 0))],
        out_specs=pl.BlockSpec((tm, D), lambda i: (i, 0)),
        out_shape=jax.ShapeDtypeStruct((T, D), F32),
        compiler_params=_cparams("parallel"),
        name="rmsnorm",
    )(x, g.reshape(1, D).astype(F32))


def _lane_tables(T, rot_dim, theta, *, rot_at, keep, scale):
    r = rot_dim // 2
    inv = 1.0 / (theta ** (jnp.arange(0, rot_dim, 2, dtype=F32) / rot_dim))
    lane = np.arange(LANES)
    in_rot = (lane >= rot_at) & (lane < rot_at + rot_dim)
    inv_lane = jnp.where(jnp.asarray(in_rot), inv[np.where(in_rot, (lane - rot_at) % r, 0)], 0.0)
    ang = jnp.arange(T, dtype=F32)[:, None] * inv_lane[None, :]
    c = jnp.cos(ang) * jnp.asarray((lane < keep) * scale, F32)[None, :]
    s = jnp.sin(ang) * jnp.asarray(in_rot * scale, F32)[None, :]
    return c, s


def _head_cols(n_heads, src_stride, src_off, width, *, dst_stride=LANES, dst_off=0):
    idx = np.zeros(n_heads * dst_stride, np.int32)
    sgn = np.zeros(n_heads * dst_stride, np.float32)
    for h in range(n_heads):
        d = h * dst_stride + dst_off
        idx[d:d + width] = h * src_stride + src_off + np.arange(width)
        sgn[d:d + width] = 1.0
    return idx, sgn


def _rot_cols(n_heads, src_stride, src_off, r, *, dst_stride=LANES, dst_off=0):
    idx = np.zeros(n_heads * dst_stride, np.int32)
    sgn = np.zeros(n_heads * dst_stride, np.float32)
    for h in range(n_heads):
        d = h * dst_stride + dst_off
        s = h * src_stride + src_off
        idx[d:d + r] = s + r + np.arange(r)
        sgn[d:d + r] = -1.0
        idx[d + r:d + 2 * r] = s + np.arange(r)
        sgn[d + r:d + 2 * r] = 1.0
    return idx, sgn


def _take_cols(w, idx_sgn):
    idx, sgn = idx_sgn
    return (jnp.take(w, jnp.asarray(idx), axis=1) * jnp.asarray(sgn)[None, :]).astype(BF16)


def _even_mixer(h, norm_g, w_in, ret_gn, w_o):
    T = h.shape[0]
    mw = MOBA_HEADS * MOBA_HEAD_DIM
    rw = RET_HEADS * RET_DK
    vw = RET_HEADS * RET_DV
    o_mq, o_mk, o_mv, o_rq, o_rk, o_rv, o_rg = np.cumsum([0, mw, mw, mw, rw, rw, vw])

    def seg(off, heads, stride, rot_dim):
        w = w_in[:, off:off + heads * stride]
        return (_take_cols(w, _head_cols(heads, stride, 0, stride)),
                _take_cols(w, _rot_cols(heads, stride, 0, rot_dim // 2)))

    segs = [seg(o_mq, MOBA_HEADS, MOBA_HEAD_DIM, MOBA_ROT), seg(o_mk, MOBA_HEADS, MOBA_HEAD_DIM, MOBA_ROT),
            seg(o_rq, RET_HEADS, RET_DK, RET_DK), seg(o_rk, RET_HEADS, RET_DK, RET_DK)]
    w_lin = jnp.concatenate([s[0] for s in segs], axis=1)
    w_rot = jnp.concatenate([s[1] for s in segs], axis=1)
    mtab = functools.partial(_lane_tables, T, MOBA_ROT, ROPE_THETA, rot_at=0, keep=MOBA_HEAD_DIM)
    rtab = functools.partial(_lane_tables, T, RET_DK, RET_THETA, rot_at=0, keep=RET_DK)
    tabs = [mtab(scale=MOBA_HEAD_DIM ** -0.5 * LOG2E), mtab(scale=1.0),
            rtab(scale=1.0), rtab(scale=RET_DK ** -0.5)]
    cos = jnp.concatenate([t[0] for t in tabs], axis=1)
    sin = jnp.concatenate([t[1] for t in tabs], axis=1)
    seg_w = MOBA_HEADS * LANES
    qk, colmean = _proj(h, w_lin, gain=norm_g, w_rot=w_rot, cos=cos, sin=sin, seg=seg_w,
                        colmean=MOBA_BLOCK)
    w_plain = jnp.concatenate(
        [_take_cols(w_in[:, o_mv:o_rq], _head_cols(MOBA_HEADS, MOBA_HEAD_DIM, 0, MOBA_HEAD_DIM)),
         w_in[:, o_rv:].astype(BF16)], axis=1)
    plain = _proj(h, w_plain, gain=norm_g)
    nb = T // MOBA_BLOCK
    km = colmean[:, 0, seg_w:2 * seg_w].reshape(nb, MOBA_HEADS, LANES).transpose(1, 0, 2)
    km = jnp.pad(km, ((0, 0), (MOBA_HEAD_DIM, LANES - MOBA_HEAD_DIM - nb), (0, 0)))
    mq_b, mk_b, mv_b = _moba_gate(qk, plain, km, MOBA_HEADS)
    a_out = _flash(mq_b, mk_b, mv_b, MOBA_HEADS, l_lane=MOBA_HEAD_DIM)
    b_out = _retention(qk, plain, ret_gn, RET_HEADS, q_off=2 * RET_HEADS, k_off=3 * RET_HEADS,
                       v_off=MOBA_HEADS, g_off=MOBA_HEADS + RET_HEADS)
    w_oa = jnp.zeros((MOBA_HEADS, LANES, w_o.shape[1]), BF16).at[:, :MOBA_HEAD_DIM].set(
        w_o[:mw].astype(BF16).reshape(MOBA_HEADS, MOBA_HEAD_DIM, -1)).reshape(seg_w, -1)
    return _matmul_res([a_out, b_out], [w_oa, w_o[mw:].astype(BF16)], h)


def _odd_mixer(h, norm_g, w_down, q_norm, w_uq, kv_norm, w_ukv, w_o):
    T = h.shape[0]
    lat = MLA_Q_RANK + MLA_KV_RANK
    half = MLA_ROPE // 2
    dq = MLA_NOPE + MLA_ROPE
    kw = MLA_HEADS * LANES
    w_dn = jnp.concatenate(
        [w_down[:, :lat].astype(BF16),
         _take_cols(w_down[:, lat:], _head_cols(1, MLA_ROPE, 0, MLA_ROPE)),
         _take_cols(w_down[:, lat:], _rot_cols(1, MLA_ROPE, 0, half))], axis=1)
    down = _proj(h, w_dn, gain=norm_g, out_dtype=F32)
    ck, sk = _lane_tables(T, MLA_ROPE, ROPE_THETA, rot_at=0, keep=MLA_ROPE, scale=1.0)
    cqn, ckvx = _mla_mid(down, q_norm, kv_norm, ck, sk)
    cq_t, sq_t = _lane_tables(T, MLA_ROPE, ROPE_THETA, rot_at=MLA_NOPE, keep=dq,
                              scale=dq ** -0.5 * LOG2E)
    q = _proj(cqn, _take_cols(w_uq, _head_cols(MLA_HEADS, dq, 0, dq)),
              w_rot=_take_cols(w_uq, _rot_cols(MLA_HEADS, dq, MLA_NOPE, half, dst_off=MLA_NOPE)),
              cos=cq_t, sin=sq_t, seg=kw)
    kvw = MLA_NOPE + MLA_V
    place_k = np.zeros((LANES, kw), np.float32)
    place_v = np.zeros((LANES, kw), np.float32)
    for hh in range(MLA_HEADS):
        place_k[np.arange(MLA_ROPE), hh * LANES + MLA_NOPE + np.arange(MLA_ROPE)] = 1.0
        place_v[MLA_ROPE, hh * LANES + MLA_V] = 1.0
    wk = jnp.concatenate([_take_cols(w_ukv, _head_cols(MLA_HEADS, kvw, 0, MLA_NOPE)),
                          jnp.asarray(place_k, BF16)], axis=0)
    wv = jnp.concatenate([_take_cols(w_ukv, _head_cols(MLA_HEADS, kvw, MLA_NOPE, MLA_V)),
                          jnp.asarray(place_v, BF16)], axis=0)
    kv = _proj(ckvx, jnp.concatenate([wk, wv], axis=1))
    o = _flash(q, kv, kv, MLA_HEADS, v_off=MLA_HEADS // 2, l_lane=MLA_V)
    w_oa = jnp.zeros((MLA_HEADS, LANES, w_o.shape[1]), BF16).at[:, :MLA_V].set(
        w_o.astype(BF16).reshape(MLA_HEADS, MLA_V, -1)).reshape(kw, -1)
    return _matmul_res([o], [w_oa], h)


def kernel(x, attn_norm, ffn_norm, ev_w_in, ev_ret_gn, ev_w_o, od_w_down, od_q_norm, od_w_uq,
           od_kv_norm, od_w_ukv, od_w_o, peer_wq, peer_keys, peer_u, peer_v, final_norm):
    B, S, D = x.shape
    assert B == 1
    h = x.reshape(S, D)
    depth = attn_norm.shape[0]
    for i in range(depth):
        j = i // 2
        if i % 2 == 0:
            h = _even_mixer(h, attn_norm[i], ev_w_in[j], ev_ret_gn[j], ev_w_o[j])
        else:
            h = _odd_mixer(h, attn_norm[i], od_w_down[j], od_q_norm[j], od_w_uq[j],
                           od_kv_norm[j], od_w_ukv[j], od_w_o[j])
        h = _peer_ffn(h, ffn_norm[i], peer_wq[i], peer_keys[i], peer_u[i], peer_v[i])
    return _rmsnorm(h, final_norm).reshape(B, S, D)
```

```python
import functools

import numpy as np
import jax
import jax.numpy as jnp
from jax import lax
from jax.experimental import pallas as pl
from jax.experimental.pallas import tpu as pltpu

F32 = jnp.float32
BF16 = jnp.bfloat16

LANES = 128
VMEM_LIMIT = 56 * 1024 * 1024

D_MODEL = 1024
EPS = 1e-6
ROPE_THETA = 500000.0
LOG2E = float(np.log2(np.e))

MOBA_HEADS = 8
MOBA_HEAD_DIM = 64
MOBA_ROT = MOBA_HEAD_DIM // 4
MOBA_BLOCK = 256
MOBA_TOPK = 3
MASK_BIAS = -1e9

RET_HEADS = 8
RET_DK = 64
RET_DV = 128
RET_THETA = 10000.0
RET_CHUNK = 512

MLA_HEADS = 16
MLA_NOPE = 64
MLA_ROPE = 32
MLA_V = 64
MLA_Q_RANK = 512
MLA_KV_RANK = 256

PEER_HEADS = 8
PEER_NKEYS = 128
PEER_DKEY = 128
PEER_TOPK = 16


def _cparams(*sem):
    return pltpu.CompilerParams(dimension_semantics=sem, vmem_limit_bytes=VMEM_LIMIT)


def _proj_body(*refs, norm, rope, colmean, tn):
    it = iter(refs)
    x_ref = next(it)
    g_ref = next(it) if norm else None
    w_ref = next(it)
    if rope:
        wr_ref, c_ref, s_ref = next(it), next(it), next(it)
    o_ref = next(it)
    cm_ref = next(it) if colmean else None
    xn_ref = next(it)

    @pl.when(pl.program_id(1) == 0)
    def _():
        x = x_ref[...].astype(F32)
        if norm:
            x = x * lax.rsqrt(jnp.mean(x * x, axis=-1, keepdims=True) + EPS) * g_ref[...]
        xn_ref[...] = x.astype(BF16)

    xn = xn_ref[...]
    y = jnp.dot(xn, w_ref[...], preferred_element_type=F32)
    if not rope:
        o_ref[...] = y.astype(o_ref.dtype)
        return
    yr = jnp.dot(xn, wr_ref[...], preferred_element_type=F32)
    c = c_ref[...]
    s = s_ref[...]
    for k in range(tn // LANES):
        sl = slice(k * LANES, (k + 1) * LANES)
        val = y[:, sl] * c + yr[:, sl] * s
        o_ref[:, sl] = val.astype(o_ref.dtype)
        if colmean:
            for b in range(val.shape[0] // colmean):
                cm_ref[b, :, sl] = jnp.mean(val[b * colmean:(b + 1) * colmean], axis=0,
                                            keepdims=True)


def _proj(x, w, *, gain=None, w_rot=None, cos=None, sin=None, seg=None,
          colmean=None, out_dtype=BF16, tm=1024, tn=512):
    T, K = x.shape
    N = w.shape[1]
    tm, tn = min(tm, T), min(tn, N)
    assert T % tm == 0 and N % tn == 0 and tn % LANES == 0
    norm, rope = gain is not None, w_rot is not None
    in_specs = [pl.BlockSpec((tm, K), lambda i, j: (i, 0))]
    args = [x]
    if norm:
        in_specs.append(pl.BlockSpec((1, K), lambda i, j: (0, 0)))
        args.append(gain.reshape(1, K).astype(F32))
    in_specs.append(pl.BlockSpec((K, tn), lambda i, j: (0, j)))
    args.append(w)
    if rope:
        assert seg % tn == 0
        tab = lambda i, j: (i, (j * tn) // seg)
        in_specs += [pl.BlockSpec((K, tn), lambda i, j: (0, j)),
                     pl.BlockSpec((tm, LANES), tab), pl.BlockSpec((tm, LANES), tab)]
        args += [w_rot, cos, sin]
    out_shape = [jax.ShapeDtypeStruct((T, N), out_dtype)]
    out_specs = [pl.BlockSpec((tm, tn), lambda i, j: (i, j))]
    if colmean:
        assert rope and tm % colmean == 0
        out_shape.append(jax.ShapeDtypeStruct((T // colmean, 1, N), F32))
        out_specs.append(pl.BlockSpec((tm // colmean, 1, tn), lambda i, j: (i, 0, j)))
    res = pl.pallas_call(
        functools.partial(_proj_body, norm=norm, rope=rope, colmean=colmean, tn=tn),
        grid=(T // tm, N // tn),
        in_specs=in_specs, out_specs=out_specs, out_shape=out_shape,
        scratch_shapes=[pltpu.VMEM((tm, K), BF16)],
        compiler_params=_cparams("parallel", "arbitrary"),
        name="proj",
    )(*args)
    return res if colmean else res[0]


def _matmul_res_body(*refs, n_in):
    xs, ws = refs[:n_in], refs[n_in:2 * n_in]
    r_ref, o_ref = refs[2 * n_in], refs[2 * n_in + 1]
    acc = r_ref[...]
    for x_ref, w_ref in zip(xs, ws):
        acc = acc + jnp.dot(x_ref[...], w_ref[...], preferred_element_type=F32)
    o_ref[...] = acc


def _matmul_res(xs, ws, res, *, tm=512, tn=512):
    T, N = res.shape
    n_in = len(xs)
    in_specs = [pl.BlockSpec((tm, x.shape[1]), lambda i, j: (i, 0)) for x in xs]
    in_specs += [pl.BlockSpec((w.shape[0], tn), lambda i, j: (0, j)) for w in ws]
    in_specs.append(pl.BlockSpec((tm, tn), lambda i, j: (i, j)))
    return pl.pallas_call(
        functools.partial(_matmul_res_body, n_in=n_in),
        grid=(T // tm, N // tn),
        in_specs=in_specs,
        out_specs=pl.BlockSpec((tm, tn), lambda i, j: (i, j)),
        out_shape=jax.ShapeDtypeStruct((T, N), F32),
        compiler_params=_cparams("parallel", "parallel"),
        name="matmul_res",
    )(*xs, *ws, res)


def _moba_gate_body(q_ref, k_ref, v_ref, km_ref, qo_ref, ko_ref, vo_ref, *, tq, heads):
    blk = pl.program_id(0)
    lane = lax.broadcasted_iota(jnp.int32, (tq, LANES), 1)
    bidx = lane - MOBA_HEAD_DIM
    slot = lax.broadcasted_iota(jnp.int32, (LANES, tq), 0)
    sblk = slot - MOBA_HEAD_DIM
    for h in range(heads):
        sl = slice(h * LANES, (h + 1) * LANES)
        q = q_ref[:, sl].astype(F32)
        gate = lax.dot_general(km_ref[h], q, (((1,), (1,)), ((), ())),
                               precision=lax.Precision.HIGHEST, preferred_element_type=F32)
        g = jnp.where((sblk >= 0) & (sblk < blk), gate, -jnp.inf)
        sel = sblk == blk
        for _ in range(MOBA_TOPK):
            m = jnp.max(g, axis=0, keepdims=True)
            idx = jnp.min(jnp.where(g == m, slot, 2 * LANES), axis=0, keepdims=True)
            pick = (slot == idx) & (m > -jnp.inf)
            sel = sel | pick
            g = jnp.where(pick, -jnp.inf, g)
        bias = jnp.where(sel, 0.0, MASK_BIAS).T
        qo_ref[:, sl] = jnp.where(bidx < 0, q, bias).astype(BF16)
        ko_ref[:, sl] = jnp.where(bidx == blk, 1.0, k_ref[:, sl].astype(F32)).astype(BF16)
        vo_ref[:, sl] = jnp.where(bidx == 0, 1.0, v_ref[:, sl].astype(F32)).astype(BF16)


def _moba_gate(qk, plain, kmean_pad, heads):
    T = qk.shape[0]
    W = heads * LANES
    tq = MOBA_BLOCK
    assert T % tq == 0 and T // tq <= LANES - MOBA_HEAD_DIM
    col = lambda c: pl.BlockSpec((tq, W), lambda i: (i, c))
    return pl.pallas_call(
        functools.partial(_moba_gate_body, tq=tq, heads=heads),
        grid=(T // tq,),
        in_specs=[col(0), col(1), col(0),
                  pl.BlockSpec((heads, LANES, LANES), lambda i: (0, 0, 0))],
        out_specs=[col(0)] * 3,
        out_shape=[jax.ShapeDtypeStruct((T, W), BF16)] * 3,
        compiler_params=_cparams("parallel"),
        name="moba_gate",
    )(qk, qk, plain, kmean_pad)


def _flash_body(q_ref, k_ref, v_ref, o_ref, m_ref, acc_ref, *, tq, l_lane):
    i = pl.program_id(1)
    m_ref[...] = jnp.full(m_ref.shape, -jnp.inf, F32)
    acc_ref[...] = jnp.zeros(acc_ref.shape, F32)

    def step(r0, nr, c0, nc, masked):
        for hh in range(2):
            hsl = slice(hh * LANES, (hh + 1) * LANES)
            s = lax.dot_general(q_ref[r0:r0 + nr, hsl], k_ref[pl.ds(c0, nc), hsl],
                                (((1,), (1,)), ((), ())), preferred_element_type=F32)
            if masked:
                row = i * tq + r0 + lax.broadcasted_iota(jnp.int32, (nr, nc), 0)
                col = c0 + lax.broadcasted_iota(jnp.int32, (nr, nc), 1)
                s = jnp.where(col <= row, s, -jnp.inf)
            m_prev = m_ref[hh, r0:r0 + nr]
            m_new = jnp.maximum(m_prev, jnp.max(s, axis=-1, keepdims=True))
            p = jnp.exp2(s - pltpu.repeat(m_new, nc // LANES, axis=1))
            acc_ref[hh, r0:r0 + nr] = jnp.exp2(m_prev - m_new) * acc_ref[hh, r0:r0 + nr] + jnp.dot(
                p.astype(BF16), v_ref[pl.ds(c0, nc), hsl], preferred_element_type=F32)
            m_ref[hh, r0:r0 + nr] = m_new

    def loop_body(j, carry):
        step(0, tq, pl.multiple_of(j * tq, tq), tq, False)
        return carry

    lax.fori_loop(0, i, loop_body, 0)
    half = tq // 2
    d0 = pl.multiple_of(i * tq, tq)
    step(0, half, d0, half, True)
    step(half, half, d0, half, False)
    step(half, half, pl.multiple_of(d0 + half, half), half, True)
    for hh in range(2):
        acc = acc_ref[hh]
        o_ref[:, hh * LANES:(hh + 1) * LANES] = (
            acc / acc[:, l_lane:l_lane + 1]).astype(o_ref.dtype)


def _flash(q, k, v, heads, *, q_off=0, k_off=0, v_off=0, l_lane, tq=1024):
    T = q.shape[0]
    tq = min(tq, T)
    assert heads % 2 == 0 and T % tq == 0 and tq % (2 * LANES) == 0
    pair = 2 * LANES
    return pl.pallas_call(
        functools.partial(_flash_body, tq=tq, l_lane=l_lane),
        grid=(heads // 2, T // tq),
        in_specs=[pl.BlockSpec((tq, pair), lambda h, i: (i, q_off + h)),
                  pl.BlockSpec((T, pair), lambda h, i: (0, k_off + h)),
                  pl.BlockSpec((T, pair), lambda h, i: (0, v_off + h))],
        out_specs=pl.BlockSpec((tq, pair), lambda h, i: (i, h)),
        out_shape=jax.ShapeDtypeStruct((T, heads * LANES), BF16),
        scratch_shapes=[pltpu.VMEM((2, tq, LANES), F32)] * 2,
        compiler_params=_cparams("parallel", "arbitrary"),
        name="flash",
    )(q, k, v)


def _retention_body(q_ref, k_ref, v_ref, g_ref, gn_ref, dm_ref, xi_ref, ze_ref, gc_ref,
                    o_ref, r_ref):
    @pl.when(pl.program_id(1) == 0)
    def _():
        r_ref[...] = jnp.zeros_like(r_ref)

    q = q_ref[...]
    k = k_ref[...]
    v = v_ref[...]
    r_old = r_ref[...]
    inner = lax.dot_general(q, k, (((1,), (1,)), ((), ())), preferred_element_type=F32) * dm_ref[0]
    out = jnp.dot(inner.astype(BF16), v, preferred_element_type=F32)
    out = out + jnp.dot(q, r_old.astype(BF16), preferred_element_type=F32) * xi_ref[0]
    kz = (k.astype(F32) * ze_ref[0]).T.astype(BF16)
    r_ref[...] = r_old * gc_ref[0] + jnp.dot(kz, v, preferred_element_type=F32)
    mu = jnp.mean(out, axis=-1, keepdims=True)
    cen = out - mu
    var = jnp.mean(cen * cen, axis=-1, keepdims=True)
    rn = cen * lax.rsqrt(var + EPS) * gn_ref[...]
    gate = g_ref[...].astype(F32)
    o_ref[...] = (rn * (gate / (1.0 + jnp.exp(-gate)))).astype(o_ref.dtype)


def _retention(qk, plain, gn, heads, *, q_off, k_off, v_off, g_off):
    T = qk.shape[0]
    W = heads * LANES
    C = min(RET_CHUNK, T)
    assert T % C == 0
    log_g = jnp.log(1.0 - 2.0 ** (-5.0 - jnp.arange(heads, dtype=F32)))
    pos = jnp.arange(C, dtype=F32)
    diff = pos[:, None] - pos[None, :]
    dmat = jnp.where(diff >= 0, jnp.exp(log_g[:, None, None] * jnp.maximum(diff, 0.0)), 0.0)
    rep = lambda t: jnp.broadcast_to(t[..., None], t.shape + (LANES,))
    xi = rep(jnp.exp(log_g[:, None] * (pos + 1.0)))
    zeta = rep(jnp.exp(log_g[:, None] * (C - 1.0 - pos)))
    g_chunk = rep(jnp.exp(log_g * C)[:, None])
    tile = lambda off: pl.BlockSpec((C, LANES), lambda h, c: (c, off + h))
    head_tab = lambda r: pl.BlockSpec((1, r, LANES), lambda h, c: (h, 0, 0))
    return pl.pallas_call(
        _retention_body,
        grid=(heads, T // C),
        in_specs=[tile(q_off), tile(k_off), tile(v_off), tile(g_off),
                  pl.BlockSpec((1, LANES), lambda h, c: (0, h)),
                  pl.BlockSpec((1, C, C), lambda h, c: (h, 0, 0)),
                  head_tab(C), head_tab(C), head_tab(1)],
        out_specs=tile(0),
        out_shape=jax.ShapeDtypeStruct((T, W), BF16),
        scratch_shapes=[pltpu.VMEM((LANES, LANES), F32)],
        compiler_params=_cparams("parallel", "arbitrary"),
        name="retention",
    )(qk, qk, plain, plain, gn.reshape(1, W).astype(F32), dmat, xi, zeta, g_chunk)


def _mla_mid_body(d_ref, qn_ref, kvn_ref, c_ref, s_ref, cq_ref, ckv_ref):
    def rms(x, g):
        return x * lax.rsqrt(jnp.mean(x * x, axis=-1, keepdims=True) + EPS) * g

    cq_ref[...] = rms(d_ref[:, :MLA_Q_RANK], qn_ref[...]).astype(BF16)
    lo = MLA_Q_RANK + MLA_KV_RANK
    ckv_ref[:, :MLA_KV_RANK] = rms(d_ref[:, MLA_Q_RANK:lo], kvn_ref[...]).astype(BF16)
    kr = d_ref[:, lo:lo + LANES] * c_ref[...] + d_ref[:, lo + LANES:lo + 2 * LANES] * s_ref[...]
    lane = lax.broadcasted_iota(jnp.int32, kr.shape, 1)
    ckv_ref[:, MLA_KV_RANK:] = jnp.where(lane == MLA_ROPE, 1.0, kr).astype(BF16)


def _mla_mid(down, q_norm, kv_norm, cos, sin, *, tm=256):
    T, W = down.shape
    wide = MLA_KV_RANK + LANES
    return pl.pallas_call(
        _mla_mid_body,
        grid=(T // tm,),
        in_specs=[pl.BlockSpec((tm, W), lambda i: (i, 0)),
                  pl.BlockSpec((1, MLA_Q_RANK), lambda i: (0, 0)),
                  pl.BlockSpec((1, MLA_KV_RANK), lambda i: (0, 0)),
                  pl.BlockSpec((tm, LANES), lambda i: (i, 0)),
                  pl.BlockSpec((tm, LANES), lambda i: (i, 0))],
        out_specs=[pl.BlockSpec((tm, MLA_Q_RANK), lambda i: (i, 0)),
                   pl.BlockSpec((tm, wide), lambda i: (i, 0))],
        out_shape=[jax.ShapeDtypeStruct((T, MLA_Q_RANK), BF16),
                   jax.ShapeDtypeStruct((T, wide), BF16)],
        compiler_params=_cparams("parallel"),
        name="mla_mid",
    )(down, q_norm.reshape(1, -1).astype(F32), kv_norm.reshape(1, -1).astype(F32), cos, sin)


def _peer_scores_body(x_ref, g_ref, wq_ref, kt_ref, xn_ref, st_ref):
    x = x_ref[...]
    xn = (x * lax.rsqrt(jnp.mean(x * x, axis=-1, keepdims=True) + EPS) * g_ref[...]).astype(BF16)
    xn_ref[...] = xn
    qry = jnp.dot(xn, wq_ref[...], preferred_element_type=F32).astype(BF16)
    st_ref[...] = lax.dot_general(kt_ref[...], qry, (((1,), (1,)), ((), ())),
                                  preferred_element_type=F32)


def _peer_scores(h, gain, wq, keys_t, *, tm=256):
    T, D = h.shape
    R = keys_t.shape[0]
    return pl.pallas_call(
        _peer_scores_body,
        grid=(T // tm,),
        in_specs=[pl.BlockSpec((tm, D), lambda i: (i, 0)),
                  pl.BlockSpec((1, D), lambda i: (0, 0)),
                  pl.BlockSpec(wq.shape, lambda i: (0, 0)),
                  pl.BlockSpec(keys_t.shape, lambda i: (0, 0))],
        out_specs=[pl.BlockSpec((tm, D), lambda i: (i, 0)),
                   pl.BlockSpec((R, tm), lambda i: (0, i))],
        out_shape=[jax.ShapeDtypeStruct((T, D), BF16), jax.ShapeDtypeStruct((R, T), F32)],
        compiler_params=_cparams("parallel"),
        name="peer_scores",
    )(h, gain.reshape(1, D).astype(F32), wq, keys_t)


PEER_CAND = [(i, PEER_TOPK // (i + 1)) for i in range(PEER_TOPK)]
PEER_NCAND = -(-sum(c for _, c in PEER_CAND) // 8) * 8
PEER_NORANK = 64.0


def _peer_topk_body(st_ref, pkf_ref, pkb_ref, t1_ref, t2_ref, cand_ref):
    n = PEER_NKEYS
    for h in range(PEER_HEADS):
        s1 = st_ref[(2 * h) * n:(2 * h + 1) * n, :]
        s2 = st_ref[(2 * h + 1) * n:(2 * h + 2) * n, :]
        vals = s1
        for r in range(PEER_TOPK):
            m = jnp.max(vals, axis=0, keepdims=True)
            t1_ref[r:r + 1, :] = m
            vals = jnp.where(vals == m, -jnp.inf, vals)
        vals = s2
        rank2 = jnp.full(s2.shape, PEER_NORANK, F32)
        for r in range(PEER_TOPK):
            m = jnp.max(vals, axis=0, keepdims=True)
            t2_ref[r:r + 1, :] = m
            hit = vals == m
            rank2 = jnp.where(hit, float(r), rank2)
            vals = jnp.where(hit, -jnp.inf, vals)
        cand_ref[...] = jnp.full(cand_ref.shape, -jnp.inf, F32)
        rowp = 0
        for i, cnt in PEER_CAND:
            cand_ref[rowp:rowp + cnt, :] = t1_ref[i:i + 1, :] + t2_ref[0:cnt, :]
            rowp += cnt
        c = cand_ref[...]
        top1 = t1_ref[0:1, :]
        top2 = t2_ref[0:1, :]
        cmax = top1 + top2
        z = jnp.zeros_like(cmax)
        for r in range(PEER_TOPK):
            kth = jnp.max(c, axis=0, keepdims=True)
            z = z + jnp.exp(kth - cmax)
            c = jnp.where(c == kth, -jnp.inf, c)
        cnt = jnp.zeros(s1.shape, F32)
        for j in range(PEER_TOPK // 2):
            cnt = cnt + jnp.where(s1 + t2_ref[j:j + 1, :] >= kth, 1.0, 0.0)
        cnt_best = jnp.zeros_like(top1)
        for j in range(PEER_TOPK):
            cnt_best = cnt_best + jnp.where(top1 + t2_ref[j:j + 1, :] >= kth, 1.0, 0.0)
        cnt = jnp.where(s1 == top1, cnt_best, cnt)
        pkf_ref[h, 0] = jnp.exp(s1 - top1) / z
        pkf_ref[h, 1] = cnt
        pkb_ref[h, 0] = rank2.astype(BF16)
        pkb_ref[h, 1] = jnp.exp(s2 - top2).astype(BF16)


def _peer_topk(st, *, tm=256):
    R, T = st.shape
    tm = min(tm, T)
    blk = lambda i: (0, 0, 0, i)
    shape = (PEER_HEADS, 2, PEER_NKEYS, T)
    return pl.pallas_call(
        _peer_topk_body,
        grid=(T // tm,),
        in_specs=[pl.BlockSpec((R, tm), lambda i: (0, i))],
        out_specs=[pl.BlockSpec((PEER_HEADS, 2, PEER_NKEYS, tm), blk)] * 2,
        out_shape=[jax.ShapeDtypeStruct(shape, F32), jax.ShapeDtypeStruct(shape, BF16)],
        scratch_shapes=[pltpu.VMEM((PEER_TOPK + 8, tm), F32),
                        pltpu.VMEM((PEER_TOPK + 8, tm), F32),
                        pltpu.VMEM((PEER_NCAND, tm), F32)],
        compiler_params=_cparams("parallel"),
        name="peer_topk",
    )(st)


def _peer_dense_body(xn_ref, u_ref, v_ref, pkf_ref, pkb_ref, h_ref, o_ref,
                     ht0_ref, ht1_ref, acc_ref, *, tm, te, n_e, n_steps):
    s = pl.program_id(0)
    n = PEER_NKEYS
    group = 2

    def scores(dst_ref):
        dst_ref[...] = lax.dot_general(u_ref[...], xn_ref[...], (((1,), (1,)), ((), ())),
                                       preferred_element_type=F32)

    def scores_half(dst_ref, half, anchor):
        hw = tm // 2
        bits = pltpu.bitcast(anchor[0:16, 0:LANES], jnp.uint32)
        zero = ((bits >> 16) >> 16)[0, 0].astype(jnp.int32)
        xs = xn_ref[pl.ds(pl.multiple_of(half * hw + zero * hw, hw), hw), :]
        dst_ref[:, half * hw:(half + 1) * hw] = lax.dot_general(
            u_ref[...], xs, (((1,), (1,)), ((), ())), preferred_element_type=F32)

    def experts(src_ref, dst_ref=None):
        e = lax.rem(s - 1, n_e)
        n_groups = te // (group * n)
        for gb in range(n_groups):
            if dst_ref is not None and gb in (1, n_groups // 2 + 1):
                scores_half(dst_ref, int(gb > 1), acts[0])
            acts = []
            for ab in range(gb * group, (gb + 1) * group):
                a = e * (te // n) + ab
                gsum = jnp.zeros((n, tm), BF16)
                for h in range(PEER_HEADS):
                    w1 = jnp.broadcast_to(pkf_ref[h, 0, pl.ds(a, 1), :], (n, tm)).astype(BF16)
                    cnt = jnp.broadcast_to(pkf_ref[h, 1, pl.ds(a, 1), :], (n, tm)).astype(BF16)
                    gsum = gsum + jnp.where(pkb_ref[h, 0] < cnt, pkb_ref[h, 1], 0.0) * w1
                hs = src_ref[ab * n:(ab + 1) * n, :]
                act = 0.5 * hs * (1.0 + lax.erf(hs * np.float32(1.0 / np.sqrt(2.0))))
                acts.append(act.astype(BF16) * gsum)
            rows = slice(gb * group * n, (gb + 1) * group * n)
            acc_ref[...] += lax.dot_general(v_ref[rows, :], jnp.concatenate(acts, axis=0),
                                            (((0,), (0,)), ((), ())),
                                            preferred_element_type=F32)

    even = lax.rem(s, 2) == 0
    steady = (s > 0) & (s < n_steps)

    @pl.when((s >= 2) & (lax.rem(s - 1, n_e) == 0))
    def _():
        o_ref[...] = h_ref[...] + acc_ref[...].T
        acc_ref[...] = jnp.zeros_like(acc_ref)

    @pl.when(s == 0)
    def _():
        acc_ref[...] = jnp.zeros_like(acc_ref)
        scores(ht0_ref)

    @pl.when(steady & even)
    def _():
        experts(ht1_ref, ht0_ref)

    @pl.when(steady & jnp.logical_not(even))
    def _():
        experts(ht0_ref, ht1_ref)

    @pl.when(s == n_steps)
    def _():
        experts(ht1_ref if n_steps % 2 == 0 else ht0_ref)


def _peer_dense(xn, u, v, pkf, pkb, h, *, tm=512, te=1024):
    T, D = xn.shape
    E = u.shape[0]
    tm = min(tm, T)
    assert T % tm == 0 and E % te == 0 and te % (2 * PEER_NKEYS) == 0
    n_e = E // te
    n_steps = (T // tm) * n_e
    pair = lambda s, lag: jnp.clip(s - lag, 0, n_steps - 1)
    cur = lambda s: pair(s, 0)
    prev = lambda s: pair(s, 1)
    done = lambda s: pair(s, 2)
    pk_spec = pl.BlockSpec((PEER_HEADS, 2, PEER_NKEYS, tm), lambda s: (0, 0, 0, prev(s) // n_e))
    return pl.pallas_call(
        functools.partial(_peer_dense_body, tm=tm, te=te, n_e=n_e, n_steps=n_steps),
        grid=(n_steps + 2,),
        in_specs=[pl.BlockSpec((tm, D), lambda s: (cur(s) // n_e, 0)),
                  pl.BlockSpec((te, D), lambda s: (cur(s) % n_e, 0)),
                  pl.BlockSpec((te, D), lambda s: (prev(s) % n_e, 0)),
                  pk_spec, pk_spec,
                  pl.BlockSpec((tm, D), lambda s: (done(s) // n_e, 0))],
        out_specs=pl.BlockSpec((tm, D), lambda s: (done(s) // n_e, 0)),
        out_shape=jax.ShapeDtypeStruct((T, D), F32),
        scratch_shapes=[pltpu.VMEM((te, tm), F32), pltpu.VMEM((te, tm), F32),
                        pltpu.VMEM((D, tm), F32)],
        compiler_params=_cparams("arbitrary"),
        name="peer_dense",
    )(xn, u, v, pkf, pkb, h)


def _peer_ffn(h, gain, wq, keys, u_tab, v_tab):
    nk, dh = PEER_NKEYS, PEER_DKEY // 2
    groups = PEER_HEADS * 2
    keys_t = jnp.einsum("gnd,gk->gnkd", keys.reshape(groups, nk, dh).astype(F32),
                        jnp.eye(groups, dtype=F32)).reshape(groups * nk, groups * dh).astype(BF16)
    xn, st = _peer_scores(h, gain, wq.astype(BF16), keys_t)
    pkf, pkb = _peer_topk(st)
    return _peer_dense(xn, u_tab.astype(BF16), v_tab.astype(BF16), pkf, pkb, h)


def _rmsnorm_body(x_ref, g_ref, o_ref):
    x = x_ref[...]
    o_ref[...] = x * lax.rsqrt(jnp.mean(x * x, axis=-1, keepdims=True) + EPS) * g_ref[...]


def _rmsnorm(x, g, *, tm=512):
    T, D = x.shape
    tm = min(tm, T)
    return pl.pallas_call(
        _rmsnorm_body,
        grid=(T // tm,),
        in_specs=[pl.BlockSpec((tm, D), lambda i: (i, 0)), pl.BlockSpec((1, D), lambda i: (0, 0))],
        out_specs=pl.BlockSpec((tm, D), lambda i: (i, 0)),
        out_shape=jax.ShapeDtypeStruct((T, D), F32),
        compiler_params=_cparams("parallel"),
        name="rmsnorm",
    )(x, g.reshape(1, D).astype(F32))


def _lane_tables(T, rot_dim, theta, *, rot_at, keep, scale):
    r = rot_dim // 2
    inv = 1.0 / (theta ** (jnp.arange(0, rot_dim, 2, dtype=F32) / rot_dim))
    lane = np.arange(LANES)
    in_rot = (lane >= rot_at) & (lane < rot_at + rot_dim)
    inv_lane = jnp.where(jnp.asarray(in_rot), inv[np.where(in_rot, (lane - rot_at) % r, 0)], 0.0)
    ang = jnp.arange(T, dtype=F32)[:, None] * inv_lane[None, :]
    c = jnp.cos(ang) * jnp.asarray((lane < keep) * scale, F32)[None, :]
    s = jnp.sin(ang) * jnp.asarray(in_rot * scale, F32)[None, :]
    return c, s


def _head_cols(n_heads, src_stride, src_off, width, *, dst_stride=LANES, dst_off=0):
    idx = np.zeros(n_heads * dst_stride, np.int32)
    sgn = np.zeros(n_heads * dst_stride, np.float32)
    for h in range(n_heads):
        d = h * dst_stride + dst_off
        idx[d:d + width] = h * src_stride + src_off + np.arange(width)
        sgn[d:d + width] = 1.0
    return idx, sgn


def _rot_cols(n_heads, src_stride, src_off, r, *, dst_stride=LANES, dst_off=0):
    idx = np.zeros(n_heads * dst_stride, np.int32)
    sgn = np.zeros(n_heads * dst_stride, np.float32)
    for h in range(n_heads):
        d = h * dst_stride + dst_off
        s = h * src_stride + src_off
        idx[d:d + r] = s + r + np.arange(r)
        sgn[d:d + r] = -1.0
        idx[d + r:d + 2 * r] = s + np.arange(r)
        sgn[d + r:d + 2 * r] = 1.0
    return idx, sgn


def _take_cols(w, idx_sgn):
    idx, sgn = idx_sgn
    return (jnp.take(w, jnp.asarray(idx), axis=1) * jnp.asarray(sgn)[None, :]).astype(BF16)


def _even_mixer(h, norm_g, w_in, ret_gn, w_o):
    T = h.shape[0]
    mw = MOBA_HEADS * MOBA_HEAD_DIM
    rw = RET_HEADS * RET_DK
    vw = RET_HEADS * RET_DV
    o_mq, o_mk, o_mv, o_rq, o_rk, o_rv, o_rg = np.cumsum([0, mw, mw, mw, rw, rw, vw])

    def seg(off, heads, stride, rot_dim):
        w = w_in[:, off:off + heads * stride]
        return (_take_cols(w, _head_cols(heads, stride, 0, stride)),
                _take_cols(w, _rot_cols(heads, stride, 0, rot_dim // 2)))

    segs = [seg(o_mq, MOBA_HEADS, MOBA_HEAD_DIM, MOBA_ROT), seg(o_mk, MOBA_HEADS, MOBA_HEAD_DIM, MOBA_ROT),
            seg(o_rq, RET_HEADS, RET_DK, RET_DK), seg(o_rk, RET_HEADS, RET_DK, RET_DK)]
    w_lin = jnp.concatenate([s[0] for s in segs], axis=1)
    w_rot = jnp.concatenate([s[1] for s in segs], axis=1)
    mtab = functools.partial(_lane_tables, T, MOBA_ROT, ROPE_THETA, rot_at=0, keep=MOBA_HEAD_DIM)
    rtab = functools.partial(_lane_tables, T, RET_DK, RET_THETA, rot_at=0, keep=RET_DK)
    tabs = [mtab(scale=MOBA_HEAD_DIM ** -0.5 * LOG2E), mtab(scale=1.0),
            rtab(scale=1.0), rtab(scale=RET_DK ** -0.5)]
    cos = jnp.concatenate([t[0] for t in tabs], axis=1)
    sin = jnp.concatenate([t[1] for t in tabs], axis=1)
    seg_w = MOBA_HEADS * LANES
    qk, colmean = _proj(h, w_lin, gain=norm_g, w_rot=w_rot, cos=cos, sin=sin, seg=seg_w,
                        colmean=MOBA_BLOCK)
    w_plain = jnp.concatenate(
        [_take_cols(w_in[:, o_mv:o_rq], _head_cols(MOBA_HEADS, MOBA_HEAD_DIM, 0, MOBA_HEAD_DIM)),
         w_in[:, o_rv:].astype(BF16)], axis=1)
    plain = _proj(h, w_plain, gain=norm_g)
    nb = T // MOBA_BLOCK
    km = colmean[:, 0, seg_w:2 * seg_w].reshape(nb, MOBA_HEADS, LANES).transpose(1, 0, 2)
    km = jnp.pad(km, ((0, 0), (MOBA_HEAD_DIM, LANES - MOBA_HEAD_DIM - nb), (0, 0)))
    mq_b, mk_b, mv_b = _moba_gate(qk, plain, km, MOBA_HEADS)
    a_out = _flash(mq_b, mk_b, mv_b, MOBA_HEADS, l_lane=MOBA_HEAD_DIM)
    b_out = _retention(qk, plain, ret_gn, RET_HEADS, q_off=2 * RET_HEADS, k_off=3 * RET_HEADS,
                       v_off=MOBA_HEADS, g_off=MOBA_HEADS + RET_HEADS)
    w_oa = jnp.zeros((MOBA_HEADS, LANES, w_o.shape[1]), BF16).at[:, :MOBA_HEAD_DIM].set(
        w_o[:mw].astype(BF16).reshape(MOBA_HEADS, MOBA_HEAD_DIM, -1)).reshape(seg_w, -1)
    return _matmul_res([a_out, b_out], [w_oa, w_o[mw:].astype(BF16)], h)


def _odd_mixer(h, norm_g, w_down, q_norm, w_uq, kv_norm, w_ukv, w_o):
    T = h.shape[0]
    lat = MLA_Q_RANK + MLA_KV_RANK
    half = MLA_ROPE // 2
    dq = MLA_NOPE + MLA_ROPE
    kw = MLA_HEADS * LANES
    w_dn = jnp.concatenate(
        [w_down[:, :lat].astype(BF16),
         _take_cols(w_down[:, lat:], _head_cols(1, MLA_ROPE, 0, MLA_ROPE)),
         _take_cols(w_down[:, lat:], _rot_cols(1, MLA_ROPE, 0, half))], axis=1)
    down = _proj(h, w_dn, gain=norm_g, out_dtype=F32)
    ck, sk = _lane_tables(T, MLA_ROPE, ROPE_THETA, rot_at=0, keep=MLA_ROPE, scale=1.0)
    cqn, ckvx = _mla_mid(down, q_norm, kv_norm, ck, sk)
    cq_t, sq_t = _lane_tables(T, MLA_ROPE, ROPE_THETA, rot_at=MLA_NOPE, keep=dq,
                              scale=dq ** -0.5 * LOG2E)
    q = _proj(cqn, _take_cols(w_uq, _head_cols(MLA_HEADS, dq, 0, dq)),
              w_rot=_take_cols(w_uq, _rot_cols(MLA_HEADS, dq, MLA_NOPE, half, dst_off=MLA_NOPE)),
              cos=cq_t, sin=sq_t, seg=kw)
    kvw = MLA_NOPE + MLA_V
    place_k = np.zeros((LANES, kw), np.float32)
    place_v = np.zeros((LANES, kw), np.float32)
    for hh in range(MLA_HEADS):
        place_k[np.arange(MLA_ROPE), hh * LANES + MLA_NOPE + np.arange(MLA_ROPE)] = 1.0
        place_v[MLA_ROPE, hh * LANES + MLA_V] = 1.0
    wk = jnp.concatenate([_take_cols(w_ukv, _head_cols(MLA_HEADS, kvw, 0, MLA_NOPE)),
                          jnp.asarray(place_k, BF16)], axis=0)
    wv = jnp.concatenate([_take_cols(w_ukv, _head_cols(MLA_HEADS, kvw, MLA_NOPE, MLA_V)),
                          jnp.asarray(place_v, BF16)], axis=0)
    kv = _proj(ckvx, jnp.concatenate([wk, wv], axis=1))
    o = _flash(q, kv, kv, MLA_HEADS, v_off=MLA_HEADS // 2, l_lane=MLA_V)
    w_oa = jnp.zeros((MLA_HEADS, LANES, w_o.shape[1]), BF16).at[:, :MLA_V].set(
        w_o.astype(BF16).reshape(MLA_HEADS, MLA_V, -1)).reshape(kw, -1)
    return _matmul_res([o], [w_oa], h)


def kernel(x, attn_norm, ffn_norm, ev_w_in, ev_ret_gn, ev_w_o, od_w_down, od_q_norm, od_w_uq,
           od_kv_norm, od_w_ukv, od_w_o, peer_wq, peer_keys, peer_u, peer_v, final_norm):
    B, S, D = x.shape
    assert B == 1
    h = x.reshape(S, D)
    depth = attn_norm.shape[0]
    for i in range(depth):
        j = i // 2
        if i % 2 == 0:
            h = _even_mixer(h, attn_norm[i], ev_w_in[j], ev_ret_gn[j], ev_w_o[j])
        else:
            h = _odd_mixer(h, attn_norm[i], od_w_down[j], od_q_norm[j], od_w_uq[j],
                           od_kv_norm[j], od_w_ukv[j], od_w_o[j])
        h = _peer_ffn(h, ffn_norm[i], peer_wq[i], peer_keys[i], peer_u[i], peer_v[i])
    return _rmsnorm(h, final_norm).reshape(B, S, D)
```

```python
import functools

import numpy as np
import jax
import jax.numpy as jnp
from jax import lax
from jax.experimental import pallas as pl
from jax.experimental.pallas import tpu as pltpu

F32 = jnp.float32
BF16 = jnp.bfloat16

LANES = 128
VMEM_LIMIT = 56 * 1024 * 1024

D_MODEL = 1024
EPS = 1e-6
ROPE_THETA = 500000.0
LOG2E = float(np.log2(np.e))

MOBA_HEADS = 8
MOBA_HEAD_DIM = 64
MOBA_ROT = MOBA_HEAD_DIM // 4
MOBA_BLOCK = 256
MOBA_TOPK = 3
MASK_BIAS = -1e9

RET_HEADS = 8
RET_DK = 64
RET_DV = 128
RET_THETA = 10000.0
RET_CHUNK = 512

MLA_HEADS = 16
MLA_NOPE = 64
MLA_ROPE = 32
MLA_V = 64
MLA_Q_RANK = 512
MLA_KV_RANK = 256

PEER_HEADS = 8
PEER_NKEYS = 128
PEER_DKEY = 128
PEER_TOPK = 16


def _cparams(*sem):
    return pltpu.CompilerParams(dimension_semantics=sem, vmem_limit_bytes=VMEM_LIMIT)


def _proj_body(*refs, norm, rot, colmean, tn):
    it = iter(refs)
    x_ref = next(it)
    g_ref = next(it) if norm else None
    w_ref = next(it)
    if rot:
        c_ref, s_ref = next(it), next(it)
    o_ref = next(it)
    cm_ref = next(it) if colmean else None
    xn_ref = next(it)

    @pl.when(pl.program_id(1) == 0)
    def _():
        x = x_ref[...].astype(F32)
        if norm:
            x = x * lax.rsqrt(jnp.mean(x * x, axis=-1, keepdims=True) + EPS) * g_ref[...]
        xn_ref[...] = x.astype(BF16)

    xn = xn_ref[...]
    y = jnp.dot(xn, w_ref[...], preferred_element_type=F32)
    if not rot:
        o_ref[...] = y.astype(o_ref.dtype)
        return
    rot_at, r = rot
    c = c_ref[...]
    s = s_ref[...]
    lane = lax.broadcasted_iota(jnp.int32, c.shape, 1)
    first = ((lane - rot_at) & (2 * r - 1)) < r
    for k in range(tn // LANES):
        sl = slice(k * LANES, (k + 1) * LANES)
        yk = y[:, sl]
        half = jnp.where(first, -pltpu.roll(yk, LANES - r, axis=1), pltpu.roll(yk, r, axis=1))
        val = yk * c + half * s
        o_ref[:, sl] = val.astype(o_ref.dtype)
        if colmean:
            for b in range(val.shape[0] // colmean):
                cm_ref[b, :, sl] = jnp.mean(val[b * colmean:(b + 1) * colmean], axis=0,
                                            keepdims=True)


def _proj(x, w, *, gain=None, rot=None, cos=None, sin=None, seg=None,
          colmean=None, out_dtype=BF16, tm=1024, tn=512):
    T, K = x.shape
    N = w.shape[1]
    tm, tn = min(tm, T), min(tn, N)
    assert T % tm == 0 and N % tn == 0 and tn % LANES == 0
    norm = gain is not None
    in_specs = [pl.BlockSpec((tm, K), lambda i, j: (i, 0))]
    args = [x]
    if norm:
        in_specs.append(pl.BlockSpec((1, K), lambda i, j: (0, 0)))
        args.append(gain.reshape(1, K).astype(F32))
    in_specs.append(pl.BlockSpec((K, tn), lambda i, j: (0, j)))
    args.append(w)
    if rot:
        assert seg % tn == 0 and rot[0] % (2 * rot[1]) == 0
        tab = lambda i, j: (i, (j * tn) // seg)
        in_specs += [pl.BlockSpec((tm, LANES), tab), pl.BlockSpec((tm, LANES), tab)]
        args += [cos, sin]
    out_shape = [jax.ShapeDtypeStruct((T, N), out_dtype)]
    out_specs = [pl.BlockSpec((tm, tn), lambda i, j: (i, j))]
    if colmean:
        assert rot and tm % colmean == 0
        out_shape.append(jax.ShapeDtypeStruct((T // colmean, 1, N), F32))
        out_specs.append(pl.BlockSpec((tm // colmean, 1, tn), lambda i, j: (i, 0, j)))
    res = pl.pallas_call(
        functools.partial(_proj_body, norm=norm, rot=rot, colmean=colmean, tn=tn),
        grid=(T // tm, N // tn),
        in_specs=in_specs, out_specs=out_specs, out_shape=out_shape,
        scratch_shapes=[pltpu.VMEM((tm, K), BF16)],
        compiler_params=_cparams("parallel", "arbitrary"),
        name="proj",
    )(*args)
    return res if colmean else res[0]


def _matmul_res_body(*refs, n_in):
    xs, ws = refs[:n_in], refs[n_in:2 * n_in]
    r_ref, o_ref = refs[2 * n_in], refs[2 * n_in + 1]
    acc = r_ref[...]
    for x_ref, w_ref in zip(xs, ws):
        acc = acc + jnp.dot(x_ref[...], w_ref[...], preferred_element_type=F32)
    o_ref[...] = acc


def _matmul_res(xs, ws, res, *, tm=512, tn=512):
    T, N = res.shape
    n_in = len(xs)
    in_specs = [pl.BlockSpec((tm, x.shape[1]), lambda i, j: (i, 0)) for x in xs]
    in_specs += [pl.BlockSpec((w.shape[0], tn), lambda i, j: (0, j)) for w in ws]
    in_specs.append(pl.BlockSpec((tm, tn), lambda i, j: (i, j)))
    return pl.pallas_call(
        functools.partial(_matmul_res_body, n_in=n_in),
        grid=(T // tm, N // tn),
        in_specs=in_specs,
        out_specs=pl.BlockSpec((tm, tn), lambda i, j: (i, j)),
        out_shape=jax.ShapeDtypeStruct((T, N), F32),
        compiler_params=_cparams("parallel", "parallel"),
        name="matmul_res",
    )(*xs, *ws, res)


def _moba_gate_body(q_ref, k_ref, v_ref, km_ref, qo_ref, ko_ref, vo_ref, *, tq, heads):
    blk = pl.program_id(0)
    lane = lax.broadcasted_iota(jnp.int32, (tq, LANES), 1)
    bidx = lane - MOBA_HEAD_DIM
    slot = lax.broadcasted_iota(jnp.int32, (LANES, tq), 0)
    sblk = slot - MOBA_HEAD_DIM
    for h in range(heads):
        sl = slice(h * LANES, (h + 1) * LANES)
        q = q_ref[:, sl].astype(F32)
        gate = lax.dot_general(km_ref[h], q, (((1,), (1,)), ((), ())),
                               precision=lax.Precision.HIGHEST, preferred_element_type=F32)
        g = jnp.where((sblk >= 0) & (sblk < blk), gate, -jnp.inf)
        sel = sblk == blk
        for _ in range(MOBA_TOPK):
            m = jnp.max(g, axis=0, keepdims=True)
            idx = jnp.min(jnp.where(g == m, slot, 2 * LANES), axis=0, keepdims=True)
            pick = (slot == idx) & (m > -jnp.inf)
            sel = sel | pick
            g = jnp.where(pick, -jnp.inf, g)
        bias = jnp.where(sel, 0.0, MASK_BIAS).T
        qo_ref[:, sl] = jnp.where(bidx < 0, q, bias).astype(BF16)
        ko_ref[:, sl] = jnp.where(bidx == blk, 1.0, k_ref[:, sl].astype(F32)).astype(BF16)
        vo_ref[:, sl] = jnp.where(bidx == 0, 1.0, v_ref[:, sl].astype(F32)).astype(BF16)


def _moba_gate(qk, plain, kmean_pad, heads):
    T = qk.shape[0]
    W = heads * LANES
    tq = MOBA_BLOCK
    assert T % tq == 0 and T // tq <= LANES - MOBA_HEAD_DIM
    col = lambda c: pl.BlockSpec((tq, W), lambda i: (i, c))
    return pl.pallas_call(
        functools.partial(_moba_gate_body, tq=tq, heads=heads),
        grid=(T // tq,),
        in_specs=[col(0), col(1), col(0),
                  pl.BlockSpec((heads, LANES, LANES), lambda i: (0, 0, 0))],
        out_specs=[col(0)] * 3,
        out_shape=[jax.ShapeDtypeStruct((T, W), BF16)] * 3,
        compiler_params=_cparams("parallel"),
        name="moba_gate",
    )(qk, qk, plain, kmean_pad)


def _flash_body(q_ref, k_ref, v_ref, o_ref, m_ref, acc_ref, *, tq, l_lane):
    i = pl.program_id(1)
    m_ref[...] = jnp.full(m_ref.shape, -jnp.inf, F32)
    acc_ref[...] = jnp.zeros(acc_ref.shape, F32)

    def step(r0, nr, c0, nc, masked):
        for hh in range(2):
            hsl = slice(hh * LANES, (hh + 1) * LANES)
            s = lax.dot_general(q_ref[r0:r0 + nr, hsl], k_ref[pl.ds(c0, nc), hsl],
                                (((1,), (1,)), ((), ())), preferred_element_type=F32)
            if masked:
                row = i * tq + r0 + lax.broadcasted_iota(jnp.int32, (nr, nc), 0)
                col = c0 + lax.broadcasted_iota(jnp.int32, (nr, nc), 1)
                s = jnp.where(col <= row, s, -jnp.inf)
            m_prev = m_ref[hh, r0:r0 + nr]
            m_new = jnp.maximum(m_prev, jnp.max(s, axis=-1, keepdims=True))
            p = jnp.exp2(s - pltpu.repeat(m_new, nc // LANES, axis=1))
            acc_ref[hh, r0:r0 + nr] = jnp.exp2(m_prev - m_new) * acc_ref[hh, r0:r0 + nr] + jnp.dot(
                p.astype(BF16), v_ref[pl.ds(c0, nc), hsl], preferred_element_type=F32)
            m_ref[hh, r0:r0 + nr] = m_new

    def loop_body(j, carry):
        step(0, tq, pl.multiple_of(j * tq, tq), tq, False)
        return carry

    lax.fori_loop(0, i, loop_body, 0)
    half = tq // 2
    d0 = pl.multiple_of(i * tq, tq)
    step(0, half, d0, half, True)
    step(half, half, d0, half, False)
    step(half, half, pl.multiple_of(d0 + half, half), half, True)
    outs = [acc_ref[hh] / acc_ref[hh][:, l_lane:l_lane + 1] for hh in range(2)]
    lane = lax.broadcasted_iota(jnp.int32, (tq, LANES), 1)
    o_ref[...] = jnp.where(lane < l_lane, outs[0],
                           pltpu.roll(outs[1], l_lane, axis=1)).astype(o_ref.dtype)


def _flash(q, k, v, heads, *, q_off=0, k_off=0, v_off=0, l_lane, tq=1024):
    T = q.shape[0]
    tq = min(tq, T)
    assert heads % 2 == 0 and T % tq == 0 and tq % (2 * LANES) == 0 and 2 * l_lane == LANES
    pair = 2 * LANES
    return pl.pallas_call(
        functools.partial(_flash_body, tq=tq, l_lane=l_lane),
        grid=(heads // 2, T // tq),
        in_specs=[pl.BlockSpec((tq, pair), lambda h, i: (i, q_off + h)),
                  pl.BlockSpec((T, pair), lambda h, i: (0, k_off + h)),
                  pl.BlockSpec((T, pair), lambda h, i: (0, v_off + h))],
        out_specs=pl.BlockSpec((tq, LANES), lambda h, i: (i, h)),
        out_shape=jax.ShapeDtypeStruct((T, heads * l_lane), BF16),
        scratch_shapes=[pltpu.VMEM((2, tq, LANES), F32)] * 2,
        compiler_params=_cparams("parallel", "arbitrary"),
        name="flash",
    )(q, k, v)


def _retention_body(q_ref, k_ref, v_ref, g_ref, gn_ref, dm_ref, xi_ref, ze_ref, gc_ref,
                    o_ref, r_ref):
    @pl.when(pl.program_id(1) == 0)
    def _():
        r_ref[...] = jnp.zeros_like(r_ref)

    q = q_ref[...]
    k = k_ref[...]
    v = v_ref[...]
    r_old = r_ref[...]
    inner = lax.dot_general(q, k, (((1,), (1,)), ((), ())), preferred_element_type=F32) * dm_ref[0]
    out = jnp.dot(inner.astype(BF16), v, preferred_element_type=F32)
    out = out + jnp.dot(q, r_old.astype(BF16), preferred_element_type=F32) * xi_ref[0]
    kz = (k.astype(F32) * ze_ref[0]).T.astype(BF16)
    r_ref[...] = r_old * gc_ref[0] + jnp.dot(kz, v, preferred_element_type=F32)
    mu = jnp.mean(out, axis=-1, keepdims=True)
    cen = out - mu
    var = jnp.mean(cen * cen, axis=-1, keepdims=True)
    rn = cen * lax.rsqrt(var + EPS) * gn_ref[...]
    gate = g_ref[...].astype(F32)
    o_ref[...] = (rn * (gate / (1.0 + jnp.exp(-gate)))).astype(o_ref.dtype)


def _retention(qk, plain, gn, heads, *, q_off, k_off, v_off, g_off):
    T = qk.shape[0]
    W = heads * LANES
    C = min(RET_CHUNK, T)
    assert T % C == 0
    log_g = jnp.log(1.0 - 2.0 ** (-5.0 - jnp.arange(heads, dtype=F32)))
    pos = jnp.arange(C, dtype=F32)
    diff = pos[:, None] - pos[None, :]
    dmat = jnp.where(diff >= 0, jnp.exp(log_g[:, None, None] * jnp.maximum(diff, 0.0)), 0.0)
    rep = lambda t: jnp.broadcast_to(t[..., None], t.shape + (LANES,))
    xi = rep(jnp.exp(log_g[:, None] * (pos + 1.0)))
    zeta = rep(jnp.exp(log_g[:, None] * (C - 1.0 - pos)))
    g_chunk = rep(jnp.exp(log_g * C)[:, None])
    tile = lambda off: pl.BlockSpec((C, LANES), lambda h, c: (c, off + h))
    head_tab = lambda r: pl.BlockSpec((1, r, LANES), lambda h, c: (h, 0, 0))
    return pl.pallas_call(
        _retention_body,
        grid=(heads, T // C),
        in_specs=[tile(q_off), tile(k_off), tile(v_off), tile(g_off),
                  pl.BlockSpec((1, LANES), lambda h, c: (0, h)),
                  pl.BlockSpec((1, C, C), lambda h, c: (h, 0, 0)),
                  head_tab(C), head_tab(C), head_tab(1)],
        out_specs=tile(0),
        out_shape=jax.ShapeDtypeStruct((T, W), BF16),
        scratch_shapes=[pltpu.VMEM((LANES, LANES), F32)],
        compiler_params=_cparams("parallel", "arbitrary"),
        name="retention",
    )(qk, qk, plain, plain, gn.reshape(1, W).astype(F32), dmat, xi, zeta, g_chunk)


def _mla_mid_body(d_ref, qn_ref, kvn_ref, c_ref, s_ref, cq_ref, ckv_ref):
    def rms(x, g):
        return x * lax.rsqrt(jnp.mean(x * x, axis=-1, keepdims=True) + EPS) * g

    cq_ref[...] = rms(d_ref[:, :MLA_Q_RANK], qn_ref[...]).astype(BF16)
    lo = MLA_Q_RANK + MLA_KV_RANK
    ckv_ref[:, :MLA_KV_RANK] = rms(d_ref[:, MLA_Q_RANK:lo], kvn_ref[...]).astype(BF16)
    kr = d_ref[:, lo:lo + LANES] * c_ref[...] + d_ref[:, lo + LANES:lo + 2 * LANES] * s_ref[...]
    lane = lax.broadcasted_iota(jnp.int32, kr.shape, 1)
    ckv_ref[:, MLA_KV_RANK:] = jnp.where(lane == MLA_ROPE, 1.0, kr).astype(BF16)


def _mla_mid(down, q_norm, kv_norm, cos, sin, *, tm=256):
    T, W = down.shape
    wide = MLA_KV_RANK + LANES
    return pl.pallas_call(
        _mla_mid_body,
        grid=(T // tm,),
        in_specs=[pl.BlockSpec((tm, W), lambda i: (i, 0)),
                  pl.BlockSpec((1, MLA_Q_RANK), lambda i: (0, 0)),
                  pl.BlockSpec((1, MLA_KV_RANK), lambda i: (0, 0)),
                  pl.BlockSpec((tm, LANES), lambda i: (i, 0)),
                  pl.BlockSpec((tm, LANES), lambda i: (i, 0))],
        out_specs=[pl.BlockSpec((tm, MLA_Q_RANK), lambda i: (i, 0)),
                   pl.BlockSpec((tm, wide), lambda i: (i, 0))],
        out_shape=[jax.ShapeDtypeStruct((T, MLA_Q_RANK), BF16),
                   jax.ShapeDtypeStruct((T, wide), BF16)],
        compiler_params=_cparams("parallel"),
        name="mla_mid",
    )(down, q_norm.reshape(1, -1).astype(F32), kv_norm.reshape(1, -1).astype(F32), cos, sin)


def _peer_scores_body(x_ref, g_ref, wq_ref, kt_ref, xn_ref, st_ref):
    x = x_ref[...]
    xn = (x * lax.rsqrt(jnp.mean(x * x, axis=-1, keepdims=True) + EPS) * g_ref[...]).astype(BF16)
    xn_ref[...] = xn
    qry = jnp.dot(xn, wq_ref[...], preferred_element_type=F32).astype(BF16)
    st_ref[...] = lax.dot_general(kt_ref[...], qry, (((1,), (1,)), ((), ())),
                                  preferred_element_type=F32)


def _peer_scores(h, gain, wq, keys_t, *, tm=256):
    T, D = h.shape
    R = keys_t.shape[0]
    return pl.pallas_call(
        _peer_scores_body,
        grid=(T // tm,),
        in_specs=[pl.BlockSpec((tm, D), lambda i: (i, 0)),
                  pl.BlockSpec((1, D), lambda i: (0, 0)),
                  pl.BlockSpec(wq.shape, lambda i: (0, 0)),
                  pl.BlockSpec(keys_t.shape, lambda i: (0, 0))],
        out_specs=[pl.BlockSpec((tm, D), lambda i: (i, 0)),
                   pl.BlockSpec((R, tm), lambda i: (0, i))],
        out_shape=[jax.ShapeDtypeStruct((T, D), BF16), jax.ShapeDtypeStruct((R, T), F32)],
        compiler_params=_cparams("parallel"),
        name="peer_scores",
    )(h, gain.reshape(1, D).astype(F32), wq, keys_t)


PEER_CAND = [(i, PEER_TOPK // (i + 1)) for i in range(PEER_TOPK)]
PEER_NCAND = -(-sum(c for _, c in PEER_CAND) // 8) * 8
PEER_NORANK = 64.0


def _peer_topk_body(st_ref, pkf_ref, pkb_ref, t1_ref, t2_ref, cand_ref):
    n = PEER_NKEYS
    for h in range(PEER_HEADS):
        s1 = st_ref[(2 * h) * n:(2 * h + 1) * n, :]
        s2 = st_ref[(2 * h + 1) * n:(2 * h + 2) * n, :]
        vals = s1
        for r in range(PEER_TOPK):
            m = jnp.max(vals, axis=0, keepdims=True)
            t1_ref[r:r + 1, :] = m
            vals = jnp.where(vals == m, -jnp.inf, vals)
        vals = s2
        rank2 = jnp.full(s2.shape, PEER_NORANK, F32)
        for r in range(PEER_TOPK):
            m = jnp.max(vals, axis=0, keepdims=True)
            t2_ref[r:r + 1, :] = m
            hit = vals == m
            rank2 = jnp.where(hit, float(r), rank2)
            vals = jnp.where(hit, -jnp.inf, vals)
        cand_ref[...] = jnp.full(cand_ref.shape, -jnp.inf, F32)
        rowp = 0
        for i, cnt in PEER_CAND:
            cand_ref[rowp:rowp + cnt, :] = t1_ref[i:i + 1, :] + t2_ref[0:cnt, :]
            rowp += cnt
        c = cand_ref[...]
        top1 = t1_ref[0:1, :]
        top2 = t2_ref[0:1, :]
        cmax = top1 + top2
        z = jnp.zeros_like(cmax)
        for r in range(PEER_TOPK):
            kth = jnp.max(c, axis=0, keepdims=True)
            z = z + jnp.exp(kth - cmax)
            c = jnp.where(c == kth, -jnp.inf, c)
        cnt = jnp.zeros(s1.shape, F32)
        for j in range(PEER_TOPK // 2):
            cnt = cnt + jnp.where(s1 + t2_ref[j:j + 1, :] >= kth, 1.0, 0.0)
        cnt_best = jnp.zeros_like(top1)
        for j in range(PEER_TOPK):
            cnt_best = cnt_best + jnp.where(top1 + t2_ref[j:j + 1, :] >= kth, 1.0, 0.0)
        cnt = jnp.where(s1 == top1, cnt_best, cnt)
        pkf_ref[h, 0] = jnp.exp(s1 - top1) / z
        pkf_ref[h, 1] = cnt
        pkb_ref[h, 0] = rank2.astype(BF16)
        pkb_ref[h, 1] = jnp.exp(s2 - top2).astype(BF16)


def _peer_topk(st, *, tm=256):
    R, T = st.shape
    tm = min(tm, T)
    blk = lambda i: (0, 0, 0, i)
    shape = (PEER_HEADS, 2, PEER_NKEYS, T)
    return pl.pallas_call(
        _peer_topk_body,
        grid=(T // tm,),
        in_specs=[pl.BlockSpec((R, tm), lambda i: (0, i))],
        out_specs=[pl.BlockSpec((PEER_HEADS, 2, PEER_NKEYS, tm), blk)] * 2,
        out_shape=[jax.ShapeDtypeStruct(shape, F32), jax.ShapeDtypeStruct(shape, BF16)],
        scratch_shapes=[pltpu.VMEM((PEER_TOPK + 8, tm), F32),
                        pltpu.VMEM((PEER_TOPK + 8, tm), F32),
                        pltpu.VMEM((PEER_NCAND, tm), F32)],
        compiler_params=_cparams("parallel"),
        name="peer_topk",
    )(st)


def _peer_dense_body(xn_ref, u_ref, v_ref, pkf_ref, pkb_ref, h_ref, o_ref,
                     ht0_ref, ht1_ref, acc_ref, *, tm, te, n_e, n_steps):
    s = pl.program_id(0)
    n = PEER_NKEYS
    group = 2

    def scores(dst_ref):
        dst_ref[...] = lax.dot_general(u_ref[...], xn_ref[...], (((1,), (1,)), ((), ())),
                                       preferred_element_type=F32)

    def scores_half(dst_ref, half, anchor):
        hw = tm // 2
        bits = pltpu.bitcast(anchor[0:16, 0:LANES], jnp.uint32)
        zero = ((bits >> 16) >> 16)[0, 0].astype(jnp.int32)
        xs = xn_ref[pl.ds(pl.multiple_of(half * hw + zero * hw, hw), hw), :]
        dst_ref[:, half * hw:(half + 1) * hw] = lax.dot_general(
            u_ref[...], xs, (((1,), (1,)), ((), ())), preferred_element_type=F32)

    def experts(src_ref, dst_ref=None):
        e = lax.rem(s - 1, n_e)
        n_groups = te // (group * n)
        for gb in range(n_groups):
            if dst_ref is not None and gb in (1, n_groups // 2 + 1):
                scores_half(dst_ref, int(gb > 1), acts[0])
            acts = []
            for ab in range(gb * group, (gb + 1) * group):
                a = e * (te // n) + ab
                gsum = jnp.zeros((n, tm), BF16)
                for h in range(PEER_HEADS):
                    w1 = jnp.broadcast_to(pkf_ref[h, 0, pl.ds(a, 1), :], (n, tm)).astype(BF16)
                    cnt = jnp.broadcast_to(pkf_ref[h, 1, pl.ds(a, 1), :], (n, tm)).astype(BF16)
                    gsum = gsum + jnp.where(pkb_ref[h, 0] < cnt, pkb_ref[h, 1], 0.0) * w1
                hs = src_ref[ab * n:(ab + 1) * n, :]
                act = 0.5 * hs * (1.0 + lax.erf(hs * np.float32(1.0 / np.sqrt(2.0))))
                acts.append(act.astype(BF16) * gsum)
            rows = slice(gb * group * n, (gb + 1) * group * n)
            acc_ref[...] += lax.dot_general(v_ref[rows, :], jnp.concatenate(acts, axis=0),
                                            (((0,), (0,)), ((), ())),
                                            preferred_element_type=F32)

    even = lax.rem(s, 2) == 0
    steady = (s > 0) & (s < n_steps)

    @pl.when((s >= 2) & (lax.rem(s - 1, n_e) == 0))
    def _():
        o_ref[...] = h_ref[...] + acc_ref[...].T
        acc_ref[...] = jnp.zeros_like(acc_ref)

    @pl.when(s == 0)
    def _():
        acc_ref[...] = jnp.zeros_like(acc_ref)
        scores(ht0_ref)

    @pl.when(steady & even)
    def _():
        experts(ht1_ref, ht0_ref)

    @pl.when(steady & jnp.logical_not(even))
    def _():
        experts(ht0_ref, ht1_ref)

    @pl.when(s == n_steps)
    def _():
        experts(ht1_ref if n_steps % 2 == 0 else ht0_ref)


def _peer_dense(xn, u, v, pkf, pkb, h, *, tm=512, te=1024):
    T, D = xn.shape
    E = u.shape[0]
    tm = min(tm, T)
    assert T % tm == 0 and E % te == 0 and te % (2 * PEER_NKEYS) == 0
    n_e = E // te
    n_steps = (T // tm) * n_e
    pair = lambda s, lag: jnp.clip(s - lag, 0, n_steps - 1)
    cur = lambda s: pair(s, 0)
    prev = lambda s: pair(s, 1)
    done = lambda s: pair(s, 2)
    pk_spec = pl.BlockSpec((PEER_HEADS, 2, PEER_NKEYS, tm), lambda s: (0, 0, 0, prev(s) // n_e))
    return pl.pallas_call(
        functools.partial(_peer_dense_body, tm=tm, te=te, n_e=n_e, n_steps=n_steps),
        grid=(n_steps + 2,),
        in_specs=[pl.BlockSpec((tm, D), lambda s: (cur(s) // n_e, 0)),
                  pl.BlockSpec((te, D), lambda s: (cur(s) % n_e, 0)),
                  pl.BlockSpec((te, D), lambda s: (prev(s) % n_e, 0)),
                  pk_spec, pk_spec,
                  pl.BlockSpec((tm, D), lambda s: (done(s) // n_e, 0))],
        out_specs=pl.BlockSpec((tm, D), lambda s: (done(s) // n_e, 0)),
        out_shape=jax.ShapeDtypeStruct((T, D), F32),
        scratch_shapes=[pltpu.VMEM((te, tm), F32), pltpu.VMEM((te, tm), F32),
                        pltpu.VMEM((D, tm), F32)],
        compiler_params=_cparams("arbitrary"),
        name="peer_dense",
    )(xn, u, v, pkf, pkb, h)


def _peer_ffn(h, gain, wq, keys, u_tab, v_tab):
    nk, dh = PEER_NKEYS, PEER_DKEY // 2
    groups = PEER_HEADS * 2
    keys_t = jnp.einsum("gnd,gk->gnkd", keys.reshape(groups, nk, dh).astype(F32),
                        jnp.eye(groups, dtype=F32)).reshape(groups * nk, groups * dh).astype(BF16)
    xn, st = _peer_scores(h, gain, wq.astype(BF16), keys_t)
    pkf, pkb = _peer_topk(st)
    return _peer_dense(xn, u_tab.astype(BF16), v_tab.astype(BF16), pkf, pkb, h)


def _rmsnorm_body(x_ref, g_ref, o_ref):
    x = x_ref[...]
    o_ref[...] = x * lax.rsqrt(jnp.mean(x * x, axis=-1, keepdims=True) + EPS) * g_ref[...]


def _rmsnorm(x, g, *, tm=512):
    T, D = x.shape
    tm = min(tm, T)
    return pl.pallas_call(
        _rmsnorm_body,
        grid=(T // tm,),
        in_specs=[pl.BlockSpec((tm, D), lambda i: (i, 0)), pl.BlockSpec((1, D), lambda i: (0, 0))],
        out_specs=pl.BlockSpec((tm, D), lambda i: (i, 0)),
        out_shape=jax.ShapeDtypeStruct((T, D), F32),
        compiler_params=_cparams("parallel"),
        name="rmsnorm",
    )(x, g.reshape(1, D).astype(F32))


def _lane_tables(T, rot_dim, theta, *, rot_at, keep, scale):
    r = rot_dim // 2
    inv = 1.0 / (theta ** (jnp.arange(0, rot_dim, 2, dtype=F32) / rot_dim))
    lane = np.arange(LANES)
    in_rot = (lane >= rot_at) & (lane < rot_at + rot_dim)
    inv_lane = jnp.where(jnp.asarray(in_rot), inv[np.where(in_rot, (lane - rot_at) % r, 0)], 0.0)
    ang = jnp.arange(T, dtype=F32)[:, None] * inv_lane[None, :]
    c = jnp.cos(ang) * jnp.asarray((lane < keep) * scale, F32)[None, :]
    s = jnp.sin(ang) * jnp.asarray(in_rot * scale, F32)[None, :]
    return c, s


def _head_cols(n_heads, src_stride, src_off, width, *, dst_stride=LANES, dst_off=0):
    idx = np.zeros(n_heads * dst_stride, np.int32)
    sgn = np.zeros(n_heads * dst_stride, np.float32)
    for h in range(n_heads):
        d = h * dst_stride + dst_off
        idx[d:d + width] = h * src_stride + src_off + np.arange(width)
        sgn[d:d + width] = 1.0
    return idx, sgn


def _rot_cols(n_heads, src_stride, src_off, r, *, dst_stride=LANES, dst_off=0):
    idx = np.zeros(n_heads * dst_stride, np.int32)
    sgn = np.zeros(n_heads * dst_stride, np.float32)
    for h in range(n_heads):
        d = h * dst_stride + dst_off
        s = h * src_stride + src_off
        idx[d:d + r] = s + r + np.arange(r)
        sgn[d:d + r] = -1.0
        idx[d + r:d + 2 * r] = s + np.arange(r)
        sgn[d + r:d + 2 * r] = 1.0
    return idx, sgn


def _take_cols(w, idx_sgn):
    idx, sgn = idx_sgn
    return (jnp.take(w, jnp.asarray(idx), axis=1) * jnp.asarray(sgn)[None, :]).astype(BF16)


def _even_mixer(h, norm_g, w_in, ret_gn, w_o):
    T = h.shape[0]
    mw = MOBA_HEADS * MOBA_HEAD_DIM
    rw = RET_HEADS * RET_DK
    vw = RET_HEADS * RET_DV
    o_mq, o_mk, o_mv, o_rq, o_rk, o_rv, o_rg = np.cumsum([0, mw, mw, mw, rw, rw, vw])

    def heads128(off, heads, stride):
        return _take_cols(w_in[:, off:off + heads * stride], _head_cols(heads, stride, 0, stride))

    seg_w = MOBA_HEADS * LANES
    mtab = functools.partial(_lane_tables, T, MOBA_ROT, ROPE_THETA, rot_at=0, keep=MOBA_HEAD_DIM)
    rtab = functools.partial(_lane_tables, T, RET_DK, RET_THETA, rot_at=0, keep=RET_DK)
    cat = lambda a, b: jnp.concatenate([a, b], axis=1)
    (cq, sq), (ck, sk) = mtab(scale=MOBA_HEAD_DIM ** -0.5 * LOG2E), mtab(scale=1.0)
    qk_m, colmean = _proj(h, cat(heads128(o_mq, MOBA_HEADS, MOBA_HEAD_DIM),
                                 heads128(o_mk, MOBA_HEADS, MOBA_HEAD_DIM)),
                          gain=norm_g, rot=(0, MOBA_ROT // 2), cos=cat(cq, ck), sin=cat(sq, sk),
                          seg=seg_w, colmean=MOBA_BLOCK)
    (cq, sq), (ck, sk) = rtab(scale=1.0), rtab(scale=RET_DK ** -0.5)
    qk_r = _proj(h, cat(heads128(o_rq, RET_HEADS, RET_DK), heads128(o_rk, RET_HEADS, RET_DK)),
                 gain=norm_g, rot=(0, RET_DK // 2), cos=cat(cq, ck), sin=cat(sq, sk), seg=seg_w)
    w_plain = jnp.concatenate(
        [_take_cols(w_in[:, o_mv:o_rq], _head_cols(MOBA_HEADS, MOBA_HEAD_DIM, 0, MOBA_HEAD_DIM)),
         w_in[:, o_rv:].astype(BF16)], axis=1)
    plain = _proj(h, w_plain, gain=norm_g)
    nb = T // MOBA_BLOCK
    km = colmean[:, 0, seg_w:2 * seg_w].reshape(nb, MOBA_HEADS, LANES).transpose(1, 0, 2)
    km = jnp.pad(km, ((0, 0), (MOBA_HEAD_DIM, LANES - MOBA_HEAD_DIM - nb), (0, 0)))
    mq_b, mk_b, mv_b = _moba_gate(qk_m, plain, km, MOBA_HEADS)
    a_out = _flash(mq_b, mk_b, mv_b, MOBA_HEADS, l_lane=MOBA_HEAD_DIM)
    b_out = _retention(qk_r, plain, ret_gn, RET_HEADS, q_off=0, k_off=RET_HEADS,
                       v_off=MOBA_HEADS, g_off=MOBA_HEADS + RET_HEADS)
    return _matmul_res([a_out, b_out], [w_o[:mw].astype(BF16), w_o[mw:].astype(BF16)], h)


def _odd_mixer(h, norm_g, w_down, q_norm, w_uq, kv_norm, w_ukv, w_o):
    T = h.shape[0]
    lat = MLA_Q_RANK + MLA_KV_RANK
    half = MLA_ROPE // 2
    dq = MLA_NOPE + MLA_ROPE
    kw = MLA_HEADS * LANES
    w_dn = jnp.concatenate(
        [w_down[:, :lat].astype(BF16),
         _take_cols(w_down[:, lat:], _head_cols(1, MLA_ROPE, 0, MLA_ROPE)),
         _take_cols(w_down[:, lat:], _rot_cols(1, MLA_ROPE, 0, half))], axis=1)
    down = _proj(h, w_dn, gain=norm_g, out_dtype=F32)
    ck, sk = _lane_tables(T, MLA_ROPE, ROPE_THETA, rot_at=0, keep=MLA_ROPE, scale=1.0)
    cqn, ckvx = _mla_mid(down, q_norm, kv_norm, ck, sk)
    cq_t, sq_t = _lane_tables(T, MLA_ROPE, ROPE_THETA, rot_at=MLA_NOPE, keep=dq,
                              scale=dq ** -0.5 * LOG2E)
    q = _proj(cqn, _take_cols(w_uq, _head_cols(MLA_HEADS, dq, 0, dq)),
              rot=(MLA_NOPE, half), cos=cq_t, sin=sq_t, seg=kw)
    kvw = MLA_NOPE + MLA_V
    place_k = np.zeros((LANES, kw), np.float32)
    place_v = np.zeros((LANES, kw), np.float32)
    for hh in range(MLA_HEADS):
        place_k[np.arange(MLA_ROPE), hh * LANES + MLA_NOPE + np.arange(MLA_ROPE)] = 1.0
        place_v[MLA_ROPE, hh * LANES + MLA_V] = 1.0
    wk = jnp.concatenate([_take_cols(w_ukv, _head_cols(MLA_HEADS, kvw, 0, MLA_NOPE)),
                          jnp.asarray(place_k, BF16)], axis=0)
    wv = jnp.concatenate([_take_cols(w_ukv, _head_cols(MLA_HEADS, kvw, MLA_NOPE, MLA_V)),
                          jnp.asarray(place_v, BF16)], axis=0)
    kv = _proj(ckvx, jnp.concatenate([wk, wv], axis=1))
    o = _flash(q, kv, kv, MLA_HEADS, v_off=MLA_HEADS // 2, l_lane=MLA_V)
    return _matmul_res([o], [w_o.astype(BF16)], h)


def kernel(x, attn_norm, ffn_norm, ev_w_in, ev_ret_gn, ev_w_o, od_w_down, od_q_norm, od_w_uq,
           od_kv_norm, od_w_ukv, od_w_o, peer_wq, peer_keys, peer_u, peer_v, final_norm):
    B, S, D = x.shape
    assert B == 1
    h = x.reshape(S, D)
    depth = attn_norm.shape[0]
    for i in range(depth):
        j = i // 2
        if i % 2 == 0:
            h = _even_mixer(h, attn_norm[i], ev_w_in[j], ev_ret_gn[j], ev_w_o[j])
        else:
            h = _odd_mixer(h, attn_norm[i], od_w_down[j], od_q_norm[j], od_w_uq[j],
                           od_kv_norm[j], od_w_ukv[j], od_w_o[j])
        h = _peer_ffn(h, ffn_norm[i], peer_wq[i], peer_keys[i], peer_u[i], peer_v[i])
    return _rmsnorm(h, final_norm).reshape(B, S, D)
```

```python
import functools

import numpy as np
import jax
import jax.numpy as jnp
from jax import lax
from jax.experimental import pallas as pl
from jax.experimental.pallas import tpu as pltpu

F32 = jnp.float32
BF16 = jnp.bfloat16

LANES = 128
VMEM_LIMIT = 56 * 1024 * 1024

D_MODEL = 1024
EPS = 1e-6
ROPE_THETA = 500000.0
LOG2E = float(np.log2(np.e))

MOBA_HEADS = 8
MOBA_HEAD_DIM = 64
MOBA_ROT = MOBA_HEAD_DIM // 4
MOBA_BLOCK = 256
MOBA_TOPK = 3
MASK_BIAS = -1e9

RET_HEADS = 8
RET_DK = 64
RET_DV = 128
RET_THETA = 10000.0
RET_CHUNK = 512

MLA_HEADS = 16
MLA_NOPE = 64
MLA_ROPE = 32
MLA_V = 64
MLA_Q_RANK = 512
MLA_KV_RANK = 256

PEER_HEADS = 8
PEER_NKEYS = 128
PEER_DKEY = 128
PEER_TOPK = 16


def _cparams(*sem):
    return pltpu.CompilerParams(dimension_semantics=sem, vmem_limit_bytes=VMEM_LIMIT)


def _proj_body(*refs, norm, rot, colmean, tn):
    it = iter(refs)
    x_ref = next(it)
    g_ref = next(it) if norm else None
    w_ref = next(it)
    if rot:
        c_ref, s_ref = next(it), next(it)
    o_ref = next(it)
    cm_ref = next(it) if colmean else None
    xn_ref = next(it)

    @pl.when(pl.program_id(1) == 0)
    def _():
        x = x_ref[...].astype(F32)
        if norm:
            x = x * lax.rsqrt(jnp.mean(x * x, axis=-1, keepdims=True) + EPS) * g_ref[...]
        xn_ref[...] = x.astype(BF16)

    xn = xn_ref[...]
    y = jnp.dot(xn, w_ref[...], preferred_element_type=F32)
    if not rot:
        o_ref[...] = y.astype(o_ref.dtype)
        return
    rot_at, r = rot
    c = c_ref[...]
    s = s_ref[...]
    lane = lax.broadcasted_iota(jnp.int32, c.shape, 1)
    first = ((lane - rot_at) & (2 * r - 1)) < r
    for k in range(tn // LANES):
        sl = slice(k * LANES, (k + 1) * LANES)
        yk = y[:, sl]
        half = jnp.where(first, -pltpu.roll(yk, LANES - r, axis=1), pltpu.roll(yk, r, axis=1))
        val = yk * c + half * s
        o_ref[:, sl] = val.astype(o_ref.dtype)
        if colmean:
            for b in range(val.shape[0] // colmean):
                cm_ref[b, :, sl] = jnp.mean(val[b * colmean:(b + 1) * colmean], axis=0,
                                            keepdims=True)


def _proj(x, w, *, gain=None, rot=None, cos=None, sin=None, seg=None,
          colmean=None, out_dtype=BF16, tm=1024, tn=1024):
    T, K = x.shape
    N = w.shape[1]
    tm, tn = min(tm, T), min(tn, N)
    assert T % tm == 0 and N % tn == 0 and tn % LANES == 0
    norm = gain is not None
    in_specs = [pl.BlockSpec((tm, K), lambda i, j: (i, 0))]
    args = [x]
    if norm:
        in_specs.append(pl.BlockSpec((1, K), lambda i, j: (0, 0)))
        args.append(gain.reshape(1, K).astype(F32))
    in_specs.append(pl.BlockSpec((K, tn), lambda i, j: (0, j)))
    args.append(w)
    if rot:
        assert seg % tn == 0 and rot[0] % (2 * rot[1]) == 0
        tab = lambda i, j: (i, (j * tn) // seg)
        in_specs += [pl.BlockSpec((tm, LANES), tab), pl.BlockSpec((tm, LANES), tab)]
        args += [cos, sin]
    out_shape = [jax.ShapeDtypeStruct((T, N), out_dtype)]
    out_specs = [pl.BlockSpec((tm, tn), lambda i, j: (i, j))]
    if colmean:
        assert rot and tm % colmean == 0
        out_shape.append(jax.ShapeDtypeStruct((T // colmean, 1, N), F32))
        out_specs.append(pl.BlockSpec((tm // colmean, 1, tn), lambda i, j: (i, 0, j)))
    res = pl.pallas_call(
        functools.partial(_proj_body, norm=norm, rot=rot, colmean=colmean, tn=tn),
        grid=(T // tm, N // tn),
        in_specs=in_specs, out_specs=out_specs, out_shape=out_shape,
        scratch_shapes=[pltpu.VMEM((tm, K), BF16)],
        compiler_params=_cparams("parallel", "arbitrary"),
        name="proj",
    )(*args)
    return res if colmean else res[0]


def _matmul_res_body(*refs, n_in):
    xs, ws = refs[:n_in], refs[n_in:2 * n_in]
    r_ref, o_ref = refs[2 * n_in], refs[2 * n_in + 1]
    acc = r_ref[...]
    for x_ref, w_ref in zip(xs, ws):
        acc = acc + jnp.dot(x_ref[...], w_ref[...], preferred_element_type=F32)
    o_ref[...] = acc


def _matmul_res(xs, ws, res, *, tm=512, tn=512):
    T, N = res.shape
    n_in = len(xs)
    in_specs = [pl.BlockSpec((tm, x.shape[1]), lambda i, j: (i, 0)) for x in xs]
    in_specs += [pl.BlockSpec((w.shape[0], tn), lambda i, j: (0, j)) for w in ws]
    in_specs.append(pl.BlockSpec((tm, tn), lambda i, j: (i, j)))
    return pl.pallas_call(
        functools.partial(_matmul_res_body, n_in=n_in),
        grid=(T // tm, N // tn),
        in_specs=in_specs,
        out_specs=pl.BlockSpec((tm, tn), lambda i, j: (i, j)),
        out_shape=jax.ShapeDtypeStruct((T, N), F32),
        compiler_params=_cparams("parallel", "parallel"),
        name="matmul_res",
    )(*xs, *ws, res)


def _moba_gate_body(q_ref, k_ref, v_ref, km_ref, qo_ref, ko_ref, vo_ref, *, tq, heads):
    blk = pl.program_id(0)
    lane = lax.broadcasted_iota(jnp.int32, (tq, LANES), 1)
    bidx = lane - MOBA_HEAD_DIM
    slot = lax.broadcasted_iota(jnp.int32, (LANES, tq), 0)
    sblk = slot - MOBA_HEAD_DIM
    for h in range(heads):
        sl = slice(h * LANES, (h + 1) * LANES)
        q = q_ref[:, sl].astype(F32)
        gate = lax.dot_general(km_ref[h], q, (((1,), (1,)), ((), ())),
                               precision=lax.Precision.HIGHEST, preferred_element_type=F32)
        g = jnp.where((sblk >= 0) & (sblk < blk), gate, -jnp.inf)
        sel = sblk == blk
        for _ in range(MOBA_TOPK):
            m = jnp.max(g, axis=0, keepdims=True)
            idx = jnp.min(jnp.where(g == m, slot, 2 * LANES), axis=0, keepdims=True)
            pick = (slot == idx) & (m > -jnp.inf)
            sel = sel | pick
            g = jnp.where(pick, -jnp.inf, g)
        bias = jnp.where(sel, 0.0, MASK_BIAS).T
        qo_ref[:, sl] = jnp.where(bidx < 0, q, bias).astype(BF16)
        ko_ref[:, sl] = jnp.where(bidx == blk, 1.0, k_ref[:, sl].astype(F32)).astype(BF16)
        vo_ref[:, sl] = jnp.where(bidx == 0, 1.0, v_ref[:, sl].astype(F32)).astype(BF16)


def _moba_gate(qk, plain, kmean_pad, heads):
    T = qk.shape[0]
    W = heads * LANES
    tq = MOBA_BLOCK
    assert T % tq == 0 and T // tq <= LANES - MOBA_HEAD_DIM
    col = lambda c: pl.BlockSpec((tq, W), lambda i: (i, c))
    return pl.pallas_call(
        functools.partial(_moba_gate_body, tq=tq, heads=heads),
        grid=(T // tq,),
        in_specs=[col(0), col(1), col(0),
                  pl.BlockSpec((heads, LANES, LANES), lambda i: (0, 0, 0))],
        out_specs=[col(0)] * 3,
        out_shape=[jax.ShapeDtypeStruct((T, W), BF16)] * 3,
        compiler_params=_cparams("parallel"),
        name="moba_gate",
    )(qk, qk, plain, kmean_pad)


def _flash_body(q_ref, k_ref, v_ref, o_ref, m_ref, acc_ref, *, tq, l_lane):
    i = pl.program_id(1)
    m_ref[...] = jnp.full(m_ref.shape, -jnp.inf, F32)
    acc_ref[...] = jnp.zeros(acc_ref.shape, F32)

    def step(r0, nr, c0, nc, masked):
        for hh in range(2):
            hsl = slice(hh * LANES, (hh + 1) * LANES)
            s = lax.dot_general(q_ref[r0:r0 + nr, hsl], k_ref[pl.ds(c0, nc), hsl],
                                (((1,), (1,)), ((), ())), preferred_element_type=F32)
            if masked:
                row = i * tq + r0 + lax.broadcasted_iota(jnp.int32, (nr, nc), 0)
                col = c0 + lax.broadcasted_iota(jnp.int32, (nr, nc), 1)
                s = jnp.where(col <= row, s, -jnp.inf)
            m_prev = m_ref[hh, r0:r0 + nr]
            m_new = jnp.maximum(m_prev, jnp.max(s, axis=-1, keepdims=True))
            p = jnp.exp2(s - pltpu.repeat(m_new, nc // LANES, axis=1))
            acc_ref[hh, r0:r0 + nr] = jnp.exp2(m_prev - m_new) * acc_ref[hh, r0:r0 + nr] + jnp.dot(
                p.astype(BF16), v_ref[pl.ds(c0, nc), hsl], preferred_element_type=F32)
            m_ref[hh, r0:r0 + nr] = m_new

    def loop_body(j, carry):
        step(0, tq, pl.multiple_of(j * tq, tq), tq, False)
        return carry

    lax.fori_loop(0, i, loop_body, 0)
    half = tq // 2
    d0 = pl.multiple_of(i * tq, tq)
    step(0, half, d0, half, True)
    step(half, half, d0, half, False)
    step(half, half, pl.multiple_of(d0 + half, half), half, True)
    outs = [acc_ref[hh] / acc_ref[hh][:, l_lane:l_lane + 1] for hh in range(2)]
    lane = lax.broadcasted_iota(jnp.int32, (tq, LANES), 1)
    o_ref[...] = jnp.where(lane < l_lane, outs[0],
                           pltpu.roll(outs[1], l_lane, axis=1)).astype(o_ref.dtype)


def _flash(q, k, v, heads, *, q_off=0, k_off=0, v_off=0, l_lane, tq=1024):
    T = q.shape[0]
    tq = min(tq, T)
    assert heads % 2 == 0 and T % tq == 0 and tq % (2 * LANES) == 0 and 2 * l_lane == LANES
    pair = 2 * LANES
    return pl.pallas_call(
        functools.partial(_flash_body, tq=tq, l_lane=l_lane),
        grid=(heads // 2, T // tq),
        in_specs=[pl.BlockSpec((tq, pair), lambda h, i: (i, q_off + h)),
                  pl.BlockSpec((T, pair), lambda h, i: (0, k_off + h)),
                  pl.BlockSpec((T, pair), lambda h, i: (0, v_off + h))],
        out_specs=pl.BlockSpec((tq, LANES), lambda h, i: (i, h)),
        out_shape=jax.ShapeDtypeStruct((T, heads * l_lane), BF16),
        scratch_shapes=[pltpu.VMEM((2, tq, LANES), F32)] * 2,
        compiler_params=_cparams("parallel", "arbitrary"),
        name="flash",
    )(q, k, v)


def _retention_body(q_ref, k_ref, v_ref, g_ref, gn_ref, dm_ref, xi_ref, ze_ref, gc_ref,
                    o_ref, r_ref):
    @pl.when(pl.program_id(1) == 0)
    def _():
        r_ref[...] = jnp.zeros_like(r_ref)

    q = q_ref[...]
    k = k_ref[...]
    v = v_ref[...]
    r_old = r_ref[...]
    inner = lax.dot_general(q, k, (((1,), (1,)), ((), ())), preferred_element_type=F32) * dm_ref[0]
    out = jnp.dot(inner.astype(BF16), v, preferred_element_type=F32)
    out = out + jnp.dot(q, r_old.astype(BF16), preferred_element_type=F32) * xi_ref[0]
    kz = (k.astype(F32) * ze_ref[0]).T.astype(BF16)
    r_ref[...] = r_old * gc_ref[0] + jnp.dot(kz, v, preferred_element_type=F32)
    mu = jnp.mean(out, axis=-1, keepdims=True)
    cen = out - mu
    var = jnp.mean(cen * cen, axis=-1, keepdims=True)
    rn = cen * lax.rsqrt(var + EPS) * gn_ref[...]
    gate = g_ref[...].astype(F32)
    o_ref[...] = (rn * (gate / (1.0 + jnp.exp(-gate)))).astype(o_ref.dtype)


def _retention(qk, plain, gn, heads, *, q_off, k_off, v_off, g_off):
    T = qk.shape[0]
    W = heads * LANES
    C = min(RET_CHUNK, T)
    assert T % C == 0
    log_g = jnp.log(1.0 - 2.0 ** (-5.0 - jnp.arange(heads, dtype=F32)))
    pos = jnp.arange(C, dtype=F32)
    diff = pos[:, None] - pos[None, :]
    dmat = jnp.where(diff >= 0, jnp.exp(log_g[:, None, None] * jnp.maximum(diff, 0.0)), 0.0)
    rep = lambda t: jnp.broadcast_to(t[..., None], t.shape + (LANES,))
    xi = rep(jnp.exp(log_g[:, None] * (pos + 1.0)))
    zeta = rep(jnp.exp(log_g[:, None] * (C - 1.0 - pos)))
    g_chunk = rep(jnp.exp(log_g * C)[:, None])
    tile = lambda off: pl.BlockSpec((C, LANES), lambda h, c: (c, off + h))
    head_tab = lambda r: pl.BlockSpec((1, r, LANES), lambda h, c: (h, 0, 0))
    return pl.pallas_call(
        _retention_body,
        grid=(heads, T // C),
        in_specs=[tile(q_off), tile(k_off), tile(v_off), tile(g_off),
                  pl.BlockSpec((1, LANES), lambda h, c: (0, h)),
                  pl.BlockSpec((1, C, C), lambda h, c: (h, 0, 0)),
                  head_tab(C), head_tab(C), head_tab(1)],
        out_specs=tile(0),
        out_shape=jax.ShapeDtypeStruct((T, W), BF16),
        scratch_shapes=[pltpu.VMEM((LANES, LANES), F32)],
        compiler_params=_cparams("parallel", "arbitrary"),
        name="retention",
    )(qk, qk, plain, plain, gn.reshape(1, W).astype(F32), dmat, xi, zeta, g_chunk)


def _mla_mid_body(d_ref, qn_ref, kvn_ref, c_ref, s_ref, cq_ref, ckv_ref):
    def rms(x, g):
        return x * lax.rsqrt(jnp.mean(x * x, axis=-1, keepdims=True) + EPS) * g

    cq_ref[...] = rms(d_ref[:, :MLA_Q_RANK], qn_ref[...]).astype(BF16)
    lo = MLA_Q_RANK + MLA_KV_RANK
    ckv_ref[:, :MLA_KV_RANK] = rms(d_ref[:, MLA_Q_RANK:lo], kvn_ref[...]).astype(BF16)
    kr = d_ref[:, lo:lo + LANES] * c_ref[...] + d_ref[:, lo + LANES:lo + 2 * LANES] * s_ref[...]
    lane = lax.broadcasted_iota(jnp.int32, kr.shape, 1)
    ckv_ref[:, MLA_KV_RANK:] = jnp.where(lane == MLA_ROPE, 1.0, kr).astype(BF16)


def _mla_mid(down, q_norm, kv_norm, cos, sin, *, tm=256):
    T, W = down.shape
    wide = MLA_KV_RANK + LANES
    return pl.pallas_call(
        _mla_mid_body,
        grid=(T // tm,),
        in_specs=[pl.BlockSpec((tm, W), lambda i: (i, 0)),
                  pl.BlockSpec((1, MLA_Q_RANK), lambda i: (0, 0)),
                  pl.BlockSpec((1, MLA_KV_RANK), lambda i: (0, 0)),
                  pl.BlockSpec((tm, LANES), lambda i: (i, 0)),
                  pl.BlockSpec((tm, LANES), lambda i: (i, 0))],
        out_specs=[pl.BlockSpec((tm, MLA_Q_RANK), lambda i: (i, 0)),
                   pl.BlockSpec((tm, wide), lambda i: (i, 0))],
        out_shape=[jax.ShapeDtypeStruct((T, MLA_Q_RANK), BF16),
                   jax.ShapeDtypeStruct((T, wide), BF16)],
        compiler_params=_cparams("parallel"),
        name="mla_mid",
    )(down, q_norm.reshape(1, -1).astype(F32), kv_norm.reshape(1, -1).astype(F32), cos, sin)


PEER_CAND = [(i, PEER_TOPK // (i + 1)) for i in range(PEER_TOPK)]
PEER_NCAND = -(-sum(c for _, c in PEER_CAND) // 8) * 8
PEER_NORANK = 64.0


def _peer_topk_body(x_ref, g_ref, wq_ref, kt_ref, xn_ref, pkf_ref, pkb_ref,
                    st_ref, t1_ref, t2_ref, cand_ref):
    x = x_ref[...]
    xn = (x * lax.rsqrt(jnp.mean(x * x, axis=-1, keepdims=True) + EPS) * g_ref[...]).astype(BF16)
    xn_ref[...] = xn
    qry = jnp.dot(xn, wq_ref[...], preferred_element_type=F32).astype(BF16)
    st_ref[...] = lax.dot_general(kt_ref[...], qry, (((1,), (1,)), ((), ())),
                                  preferred_element_type=F32)
    n = PEER_NKEYS
    for h in range(PEER_HEADS):
        s1 = st_ref[(2 * h) * n:(2 * h + 1) * n, :]
        s2 = st_ref[(2 * h + 1) * n:(2 * h + 2) * n, :]
        vals = s1
        for r in range(PEER_TOPK):
            m = jnp.max(vals, axis=0, keepdims=True)
            t1_ref[r:r + 1, :] = m
            vals = jnp.where(vals == m, -jnp.inf, vals)
        vals = s2
        rank2 = jnp.full(s2.shape, PEER_NORANK, F32)
        for r in range(PEER_TOPK):
            m = jnp.max(vals, axis=0, keepdims=True)
            t2_ref[r:r + 1, :] = m
            hit = vals == m
            rank2 = jnp.where(hit, float(r), rank2)
            vals = jnp.where(hit, -jnp.inf, vals)
        cand_ref[...] = jnp.full(cand_ref.shape, -jnp.inf, F32)
        rowp = 0
        for i, cnt in PEER_CAND:
            cand_ref[rowp:rowp + cnt, :] = t1_ref[i:i + 1, :] + t2_ref[0:cnt, :]
            rowp += cnt
        c = cand_ref[...]
        top1 = t1_ref[0:1, :]
        top2 = t2_ref[0:1, :]
        cmax = top1 + top2
        z = jnp.zeros_like(cmax)
        for r in range(PEER_TOPK):
            kth = jnp.max(c, axis=0, keepdims=True)
            z = z + jnp.exp(kth - cmax)
            c = jnp.where(c == kth, -jnp.inf, c)
        cnt = jnp.zeros(s1.shape, F32)
        for j in range(PEER_TOPK // 2):
            cnt = cnt + jnp.where(s1 + t2_ref[j:j + 1, :] >= kth, 1.0, 0.0)
        cnt_best = jnp.zeros_like(top1)
        for j in range(PEER_TOPK):
            cnt_best = cnt_best + jnp.where(top1 + t2_ref[j:j + 1, :] >= kth, 1.0, 0.0)
        cnt = jnp.where(s1 == top1, cnt_best, cnt)
        pkf_ref[h, 0] = jnp.exp(s1 - top1) / z
        pkf_ref[h, 1] = cnt
        pkb_ref[h, 0] = rank2.astype(BF16)
        pkb_ref[h, 1] = jnp.exp(s2 - top2).astype(BF16)


def _peer_topk(h, gain, wq, keys_t, *, tm=256):
    T, D = h.shape
    R = keys_t.shape[0]
    tm = min(tm, T)
    blk = lambda i: (0, 0, 0, i)
    shape = (PEER_HEADS, 2, PEER_NKEYS, T)
    return pl.pallas_call(
        _peer_topk_body,
        grid=(T // tm,),
        in_specs=[pl.BlockSpec((tm, D), lambda i: (i, 0)),
                  pl.BlockSpec((1, D), lambda i: (0, 0)),
                  pl.BlockSpec(wq.shape, lambda i: (0, 0)),
                  pl.BlockSpec(keys_t.shape, lambda i: (0, 0))],
        out_specs=[pl.BlockSpec((tm, D), lambda i: (i, 0))]
        + [pl.BlockSpec((PEER_HEADS, 2, PEER_NKEYS, tm), blk)] * 2,
        out_shape=[jax.ShapeDtypeStruct((T, D), BF16),
                   jax.ShapeDtypeStruct(shape, F32), jax.ShapeDtypeStruct(shape, BF16)],
        scratch_shapes=[pltpu.VMEM((R, tm), F32),
                        pltpu.VMEM((PEER_TOPK + 8, tm), F32),
                        pltpu.VMEM((PEER_TOPK + 8, tm), F32),
                        pltpu.VMEM((PEER_NCAND, tm), F32)],
        compiler_params=_cparams("parallel"),
        name="peer_topk",
    )(h, gain.reshape(1, D).astype(F32), wq, keys_t)


def _peer_dense_body(xn_ref, u_ref, v_ref, pkf_ref, pkb_ref, h_ref, *rest,
                     tm, te, n_e, n_steps, out_norm):
    og_ref = rest[0] if out_norm else None
    o_ref, ht0_ref, ht1_ref, acc_ref = rest[-4:]
    s = pl.program_id(0)
    n = PEER_NKEYS
    group = 2

    def scores(dst_ref):
        dst_ref[...] = lax.dot_general(u_ref[...], xn_ref[...], (((1,), (1,)), ((), ())),
                                       preferred_element_type=F32)

    def scores_half(dst_ref, half, anchor):
        hw = tm // 2
        bits = pltpu.bitcast(anchor[0:16, 0:LANES], jnp.uint32)
        zero = ((bits >> 16) >> 16)[0, 0].astype(jnp.int32)
        xs = xn_ref[pl.ds(pl.multiple_of(half * hw + zero * hw, hw), hw), :]
        dst_ref[:, half * hw:(half + 1) * hw] = lax.dot_general(
            u_ref[...], xs, (((1,), (1,)), ((), ())), preferred_element_type=F32)

    def experts(src_ref, dst_ref=None):
        e = lax.rem(s - 1, n_e)
        n_groups = te // (group * n)
        for gb in range(n_groups):
            if dst_ref is not None and gb in (1, n_groups // 2 + 1):
                scores_half(dst_ref, int(gb > 1), acts[0])
            acts = []
            for ab in range(gb * group, (gb + 1) * group):
                a = e * (te // n) + ab
                gsum = jnp.zeros((n, tm), BF16)
                for h in range(PEER_HEADS):
                    w1 = jnp.broadcast_to(pkf_ref[h, 0, pl.ds(a, 1), :], (n, tm)).astype(BF16)
                    cnt = jnp.broadcast_to(pkf_ref[h, 1, pl.ds(a, 1), :], (n, tm)).astype(BF16)
                    gsum = gsum + jnp.where(pkb_ref[h, 0] < cnt, pkb_ref[h, 1], 0.0) * w1
                hs = src_ref[ab * n:(ab + 1) * n, :]
                act = 0.5 * hs * (1.0 + lax.erf(hs * np.float32(1.0 / np.sqrt(2.0))))
                acts.append(act.astype(BF16) * gsum)
            rows = slice(gb * group * n, (gb + 1) * group * n)
            acc_ref[...] += lax.dot_general(v_ref[rows, :], jnp.concatenate(acts, axis=0),
                                            (((0,), (0,)), ((), ())),
                                            preferred_element_type=F32)

    even = lax.rem(s, 2) == 0
    steady = (s > 0) & (s < n_steps)

    @pl.when((s >= 2) & (lax.rem(s - 1, n_e) == 0))
    def _():
        y = h_ref[...] + acc_ref[...].T
        if out_norm:
            y = y * lax.rsqrt(jnp.mean(y * y, axis=-1, keepdims=True) + EPS) * og_ref[...]
        o_ref[...] = y
        acc_ref[...] = jnp.zeros_like(acc_ref)

    @pl.when(s == 0)
    def _():
        acc_ref[...] = jnp.zeros_like(acc_ref)
        scores(ht0_ref)

    @pl.when(steady & even)
    def _():
        experts(ht1_ref, ht0_ref)

    @pl.when(steady & jnp.logical_not(even))
    def _():
        experts(ht0_ref, ht1_ref)

    @pl.when(s == n_steps)
    def _():
        experts(ht1_ref if n_steps % 2 == 0 else ht0_ref)


def _peer_dense(xn, u, v, pkf, pkb, h, *, out_gain=None, tm=512, te=1024):
    T, D = xn.shape
    E = u.shape[0]
    tm = min(tm, T)
    assert T % tm == 0 and E % te == 0 and te % (2 * PEER_NKEYS) == 0
    n_e = E // te
    n_steps = (T // tm) * n_e
    pair = lambda s, lag: jnp.clip(s - lag, 0, n_steps - 1)
    cur = lambda s: pair(s, 0)
    prev = lambda s: pair(s, 1)
    done = lambda s: pair(s, 2)
    pk_spec = pl.BlockSpec((PEER_HEADS, 2, PEER_NKEYS, tm), lambda s: (0, 0, 0, prev(s) // n_e))
    return pl.pallas_call(
        functools.partial(_peer_dense_body, tm=tm, te=te, n_e=n_e, n_steps=n_steps,
                          out_norm=out_gain is not None),
        grid=(n_steps + 2,),
        in_specs=[pl.BlockSpec((tm, D), lambda s: (cur(s) // n_e, 0)),
                  pl.BlockSpec((te, D), lambda s: (cur(s) % n_e, 0)),
                  pl.BlockSpec((te, D), lambda s: (prev(s) % n_e, 0)),
                  pk_spec, pk_spec,
                  pl.BlockSpec((tm, D), lambda s: (done(s) // n_e, 0))]
        + ([pl.BlockSpec((1, D), lambda s: (0, 0))] if out_gain is not None else []),
        out_specs=pl.BlockSpec((tm, D), lambda s: (done(s) // n_e, 0)),
        out_shape=jax.ShapeDtypeStruct((T, D), F32),
        scratch_shapes=[pltpu.VMEM((te, tm), F32), pltpu.VMEM((te, tm), F32),
                        pltpu.VMEM((D, tm), F32)],
        compiler_params=_cparams("arbitrary"),
        name="peer_dense",
    )(xn, u, v, pkf, pkb, h,
      *([out_gain.reshape(1, D).astype(F32)] if out_gain is not None else []))


def _peer_ffn(h, gain, wq, keys, u_tab, v_tab, out_gain=None):
    nk, dh = PEER_NKEYS, PEER_DKEY // 2
    groups = PEER_HEADS * 2
    keys_t = jnp.einsum("gnd,gk->gnkd", keys.reshape(groups, nk, dh).astype(F32),
                        jnp.eye(groups, dtype=F32)).reshape(groups * nk, groups * dh).astype(BF16)
    xn, pkf, pkb = _peer_topk(h, gain, wq.astype(BF16), keys_t)
    return _peer_dense(xn, u_tab.astype(BF16), v_tab.astype(BF16), pkf, pkb, h, out_gain=out_gain)


def _lane_tables(T, rot_dim, theta, *, rot_at, keep, scale):
    r = rot_dim // 2
    inv = 1.0 / (theta ** (jnp.arange(0, rot_dim, 2, dtype=F32) / rot_dim))
    lane = np.arange(LANES)
    in_rot = (lane >= rot_at) & (lane < rot_at + rot_dim)
    inv_lane = jnp.where(jnp.asarray(in_rot), inv[np.where(in_rot, (lane - rot_at) % r, 0)], 0.0)
    ang = jnp.arange(T, dtype=F32)[:, None] * inv_lane[None, :]
    c = jnp.cos(ang) * jnp.asarray((lane < keep) * scale, F32)[None, :]
    s = jnp.sin(ang) * jnp.asarray(in_rot * scale, F32)[None, :]
    return c, s


def _head_cols(n_heads, src_stride, src_off, width, *, dst_stride=LANES, dst_off=0):
    idx = np.zeros(n_heads * dst_stride, np.int32)
    sgn = np.zeros(n_heads * dst_stride, np.float32)
    for h in range(n_heads):
        d = h * dst_stride + dst_off
        idx[d:d + width] = h * src_stride + src_off + np.arange(width)
        sgn[d:d + width] = 1.0
    return idx, sgn


def _rot_cols(n_heads, src_stride, src_off, r, *, dst_stride=LANES, dst_off=0):
    idx = np.zeros(n_heads * dst_stride, np.int32)
    sgn = np.zeros(n_heads * dst_stride, np.float32)
    for h in range(n_heads):
        d = h * dst_stride + dst_off
        s = h * src_stride + src_off
        idx[d:d + r] = s + r + np.arange(r)
        sgn[d:d + r] = -1.0
        idx[d + r:d + 2 * r] = s + np.arange(r)
        sgn[d + r:d + 2 * r] = 1.0
    return idx, sgn


def _take_cols(w, idx_sgn):
    idx, sgn = idx_sgn
    return (jnp.take(w, jnp.asarray(idx), axis=1) * jnp.asarray(sgn)[None, :]).astype(BF16)


def _even_mixer(h, norm_g, w_in, ret_gn, w_o):
    T = h.shape[0]
    mw = MOBA_HEADS * MOBA_HEAD_DIM
    rw = RET_HEADS * RET_DK
    vw = RET_HEADS * RET_DV
    o_mq, o_mk, o_mv, o_rq, o_rk, o_rv, o_rg = np.cumsum([0, mw, mw, mw, rw, rw, vw])

    def heads128(off, heads, stride):
        return _take_cols(w_in[:, off:off + heads * stride], _head_cols(heads, stride, 0, stride))

    seg_w = MOBA_HEADS * LANES
    mtab = functools.partial(_lane_tables, T, MOBA_ROT, ROPE_THETA, rot_at=0, keep=MOBA_HEAD_DIM)
    rtab = functools.partial(_lane_tables, T, RET_DK, RET_THETA, rot_at=0, keep=RET_DK)
    cat = lambda a, b: jnp.concatenate([a, b], axis=1)
    (cq, sq), (ck, sk) = mtab(scale=MOBA_HEAD_DIM ** -0.5 * LOG2E), mtab(scale=1.0)
    qk_m, colmean = _proj(h, cat(heads128(o_mq, MOBA_HEADS, MOBA_HEAD_DIM),
                                 heads128(o_mk, MOBA_HEADS, MOBA_HEAD_DIM)),
                          gain=norm_g, rot=(0, MOBA_ROT // 2), cos=cat(cq, ck), sin=cat(sq, sk),
                          seg=seg_w, colmean=MOBA_BLOCK)
    (cq, sq), (ck, sk) = rtab(scale=1.0), rtab(scale=RET_DK ** -0.5)
    qk_r = _proj(h, cat(heads128(o_rq, RET_HEADS, RET_DK), heads128(o_rk, RET_HEADS, RET_DK)),
                 gain=norm_g, rot=(0, RET_DK // 2), cos=cat(cq, ck), sin=cat(sq, sk), seg=seg_w)
    w_plain = jnp.concatenate(
        [_take_cols(w_in[:, o_mv:o_rq], _head_cols(MOBA_HEADS, MOBA_HEAD_DIM, 0, MOBA_HEAD_DIM)),
         w_in[:, o_rv:].astype(BF16)], axis=1)
    plain = _proj(h, w_plain, gain=norm_g)
    nb = T // MOBA_BLOCK
    km = colmean[:, 0, seg_w:2 * seg_w].reshape(nb, MOBA_HEADS, LANES).transpose(1, 0, 2)
    km = jnp.pad(km, ((0, 0), (MOBA_HEAD_DIM, LANES - MOBA_HEAD_DIM - nb), (0, 0)))
    mq_b, mk_b, mv_b = _moba_gate(qk_m, plain, km, MOBA_HEADS)
    a_out = _flash(mq_b, mk_b, mv_b, MOBA_HEADS, l_lane=MOBA_HEAD_DIM)
    b_out = _retention(qk_r, plain, ret_gn, RET_HEADS, q_off=0, k_off=RET_HEADS,
                       v_off=MOBA_HEADS, g_off=MOBA_HEADS + RET_HEADS)
    return _matmul_res([a_out, b_out], [w_o[:mw].astype(BF16), w_o[mw:].astype(BF16)], h)


def _odd_mixer(h, norm_g, w_down, q_norm, w_uq, kv_norm, w_ukv, w_o):
    T = h.shape[0]
    lat = MLA_Q_RANK + MLA_KV_RANK
    half = MLA_ROPE // 2
    dq = MLA_NOPE + MLA_ROPE
    kw = MLA_HEADS * LANES
    w_dn = jnp.concatenate(
        [w_down[:, :lat].astype(BF16),
         _take_cols(w_down[:, lat:], _head_cols(1, MLA_ROPE, 0, MLA_ROPE)),
         _take_cols(w_down[:, lat:], _rot_cols(1, MLA_ROPE, 0, half))], axis=1)
    down = _proj(h, w_dn, gain=norm_g, out_dtype=F32)
    ck, sk = _lane_tables(T, MLA_ROPE, ROPE_THETA, rot_at=0, keep=MLA_ROPE, scale=1.0)
    cqn, ckvx = _mla_mid(down, q_norm, kv_norm, ck, sk)
    cq_t, sq_t = _lane_tables(T, MLA_ROPE, ROPE_THETA, rot_at=MLA_NOPE, keep=dq,
                              scale=dq ** -0.5 * LOG2E)
    q = _proj(cqn, _take_cols(w_uq, _head_cols(MLA_HEADS, dq, 0, dq)),
              rot=(MLA_NOPE, half), cos=cq_t, sin=sq_t, seg=kw)
    kvw = MLA_NOPE + MLA_V
    place_k = np.zeros((LANES, kw), np.float32)
    place_v = np.zeros((LANES, kw), np.float32)
    for hh in range(MLA_HEADS):
        place_k[np.arange(MLA_ROPE), hh * LANES + MLA_NOPE + np.arange(MLA_ROPE)] = 1.0
        place_v[MLA_ROPE, hh * LANES + MLA_V] = 1.0
    wk = jnp.concatenate([_take_cols(w_ukv, _head_cols(MLA_HEADS, kvw, 0, MLA_NOPE)),
                          jnp.asarray(place_k, BF16)], axis=0)
    wv = jnp.concatenate([_take_cols(w_ukv, _head_cols(MLA_HEADS, kvw, MLA_NOPE, MLA_V)),
                          jnp.asarray(place_v, BF16)], axis=0)
    kv = _proj(ckvx, jnp.concatenate([wk, wv], axis=1))
    o = _flash(q, kv, kv, MLA_HEADS, v_off=MLA_HEADS // 2, l_lane=MLA_V)
    return _matmul_res([o], [w_o.astype(BF16)], h)


def kernel(x, attn_norm, ffn_norm, ev_w_in, ev_ret_gn, ev_w_o, od_w_down, od_q_norm, od_w_uq,
           od_kv_norm, od_w_ukv, od_w_o, peer_wq, peer_keys, peer_u, peer_v, final_norm):
    B, S, D = x.shape
    assert B == 1
    h = x.reshape(S, D)
    depth = attn_norm.shape[0]
    for i in range(depth):
        j = i // 2
        if i % 2 == 0:
            h = _even_mixer(h, attn_norm[i], ev_w_in[j], ev_ret_gn[j], ev_w_o[j])
        else:
            h = _odd_mixer(h, attn_norm[i], od_w_down[j], od_q_norm[j], od_w_uq[j],
                           od_kv_norm[j], od_w_ukv[j], od_w_o[j])
        h = _peer_ffn(h, ffn_norm[i], peer_wq[i], peer_keys[i], peer_u[i], peer_v[i],
                      out_gain=final_norm if i == depth - 1 else None)
    return h.reshape(B, S, D)
```

```python
import functools

import numpy as np
import jax
import jax.numpy as jnp
from jax import lax
from jax.experimental import pallas as pl
from jax.experimental.pallas import tpu as pltpu

F32 = jnp.float32
BF16 = jnp.bfloat16

LANES = 128
VMEM_LIMIT = 56 * 1024 * 1024

D_MODEL = 1024
EPS = 1e-6
ROPE_THETA = 500000.0
LOG2E = float(np.log2(np.e))

MOBA_HEADS = 8
MOBA_HEAD_DIM = 64
MOBA_ROT = MOBA_HEAD_DIM // 4
MOBA_BLOCK = 256
MOBA_TOPK = 3
MASK_BIAS = -1e9

RET_HEADS = 8
RET_DK = 64
RET_DV = 128
RET_THETA = 10000.0
RET_CHUNK = 512

MLA_HEADS = 16
MLA_NOPE = 64
MLA_ROPE = 32
MLA_V = 64
MLA_Q_RANK = 512
MLA_KV_RANK = 256

PEER_HEADS = 8
PEER_NKEYS = 128
PEER_DKEY = 128
PEER_TOPK = 16


def _cparams(*sem):
    return pltpu.CompilerParams(dimension_semantics=sem, vmem_limit_bytes=VMEM_LIMIT)


def _proj_body(*refs, norm, rot, colmean, tn):
    it = iter(refs)
    x_ref = next(it)
    g_ref = next(it) if norm else None
    w_ref = next(it)
    if rot:
        c_ref, s_ref = next(it), next(it)
    o_ref = next(it)
    cm_ref = next(it) if colmean else None
    xn_ref = next(it)

    @pl.when(pl.program_id(1) == 0)
    def _():
        x = x_ref[...].astype(F32)
        if norm:
            x = x * lax.rsqrt(jnp.mean(x * x, axis=-1, keepdims=True) + EPS) * g_ref[...]
        xn_ref[...] = x.astype(BF16)

    xn = xn_ref[...]
    y = jnp.dot(xn, w_ref[...], preferred_element_type=F32)
    if not rot:
        o_ref[...] = y.astype(o_ref.dtype)
        return
    rot_at, r = rot
    c = c_ref[...]
    s = s_ref[...]
    lane = lax.broadcasted_iota(jnp.int32, c.shape, 1)
    first = ((lane - rot_at) & (2 * r - 1)) < r
    for k in range(tn // LANES):
        sl = slice(k * LANES, (k + 1) * LANES)
        yk = y[:, sl]
        half = jnp.where(first, -pltpu.roll(yk, LANES - r, axis=1), pltpu.roll(yk, r, axis=1))
        val = yk * c + half * s
        o_ref[:, sl] = val.astype(o_ref.dtype)
        if colmean:
            for b in range(val.shape[0] // colmean):
                cm_ref[b, :, sl] = jnp.mean(val[b * colmean:(b + 1) * colmean], axis=0,
                                            keepdims=True)


def _proj(x, w, *, gain=None, rot=None, cos=None, sin=None, seg=None,
          colmean=None, out_dtype=BF16, tm=1024, tn=1024):
    T, K = x.shape
    N = w.shape[1]
    tm, tn = min(tm, T), min(tn, N)
    assert T % tm == 0 and N % tn == 0 and tn % LANES == 0
    norm = gain is not None
    in_specs = [pl.BlockSpec((tm, K), lambda i, j: (i, 0))]
    args = [x]
    if norm:
        in_specs.append(pl.BlockSpec((1, K), lambda i, j: (0, 0)))
        args.append(gain.reshape(1, K).astype(F32))
    in_specs.append(pl.BlockSpec((K, tn), lambda i, j: (0, j)))
    args.append(w)
    if rot:
        assert seg % tn == 0 and rot[0] % (2 * rot[1]) == 0
        tab = lambda i, j: (i, (j * tn) // seg)
        in_specs += [pl.BlockSpec((tm, LANES), tab), pl.BlockSpec((tm, LANES), tab)]
        args += [cos, sin]
    out_shape = [jax.ShapeDtypeStruct((T, N), out_dtype)]
    out_specs = [pl.BlockSpec((tm, tn), lambda i, j: (i, j))]
    if colmean:
        assert rot and tm % colmean == 0
        out_shape.append(jax.ShapeDtypeStruct((T // colmean, 1, N), F32))
        out_specs.append(pl.BlockSpec((tm // colmean, 1, tn), lambda i, j: (i, 0, j)))
    res = pl.pallas_call(
        functools.partial(_proj_body, norm=norm, rot=rot, colmean=colmean, tn=tn),
        grid=(T // tm, N // tn),
        in_specs=in_specs, out_specs=out_specs, out_shape=out_shape,
        scratch_shapes=[pltpu.VMEM((tm, K), BF16)],
        compiler_params=_cparams("parallel", "arbitrary"),
        name="proj",
    )(*args)
    return res if colmean else res[0]


def _matmul_res_body(*refs, n_in):
    xs, ws = refs[:n_in], refs[n_in:2 * n_in]
    r_ref, o_ref = refs[2 * n_in], refs[2 * n_in + 1]
    acc = r_ref[...]
    for x_ref, w_ref in zip(xs, ws):
        acc = acc + jnp.dot(x_ref[...], w_ref[...], preferred_element_type=F32)
    o_ref[...] = acc


def _matmul_res(xs, ws, res, *, tm=512, tn=512):
    T, N = res.shape
    n_in = len(xs)
    in_specs = [pl.BlockSpec((tm, x.shape[1]), lambda i, j: (i, 0)) for x in xs]
    in_specs += [pl.BlockSpec((w.shape[0], tn), lambda i, j: (0, j)) for w in ws]
    in_specs.append(pl.BlockSpec((tm, tn), lambda i, j: (i, j)))
    return pl.pallas_call(
        functools.partial(_matmul_res_body, n_in=n_in),
        grid=(T // tm, N // tn),
        in_specs=in_specs,
        out_specs=pl.BlockSpec((tm, tn), lambda i, j: (i, j)),
        out_shape=jax.ShapeDtypeStruct((T, N), F32),
        compiler_params=_cparams("parallel", "parallel"),
        name="matmul_res",
    )(*xs, *ws, res)


def _moba_gate_body(q_ref, k_ref, v_ref, km_ref, qo_ref, ko_ref, vo_ref, *, tq, heads):
    blk = pl.program_id(0)
    lane = lax.broadcasted_iota(jnp.int32, (tq, LANES), 1)
    bidx = lane - MOBA_HEAD_DIM
    slot = lax.broadcasted_iota(jnp.int32, (LANES, tq), 0)
    sblk = slot - MOBA_HEAD_DIM
    for h in range(heads):
        sl = slice(h * LANES, (h + 1) * LANES)
        q = q_ref[:, sl].astype(F32)
        gate = lax.dot_general(km_ref[h], q, (((1,), (1,)), ((), ())),
                               precision=lax.Precision.HIGHEST, preferred_element_type=F32)
        g = jnp.where((sblk >= 0) & (sblk < blk), gate, -jnp.inf)
        sel = sblk == blk
        for _ in range(MOBA_TOPK):
            m = jnp.max(g, axis=0, keepdims=True)
            idx = jnp.min(jnp.where(g == m, slot, 2 * LANES), axis=0, keepdims=True)
            pick = (slot == idx) & (m > -jnp.inf)
            sel = sel | pick
            g = jnp.where(pick, -jnp.inf, g)
        bias = jnp.where(sel, 0.0, MASK_BIAS).T
        qo_ref[:, sl] = jnp.where(bidx < 0, q, bias).astype(BF16)
        ko_ref[:, sl] = jnp.where(bidx == blk, 1.0, k_ref[:, sl].astype(F32)).astype(BF16)
        vo_ref[:, sl] = jnp.where(bidx == 0, 1.0, v_ref[:, sl].astype(F32)).astype(BF16)


def _moba_gate(qk, plain, kmean_pad, heads):
    T = qk.shape[0]
    W = heads * LANES
    tq = MOBA_BLOCK
    assert T % tq == 0 and T // tq <= LANES - MOBA_HEAD_DIM
    col = lambda c: pl.BlockSpec((tq, W), lambda i: (i, c))
    return pl.pallas_call(
        functools.partial(_moba_gate_body, tq=tq, heads=heads),
        grid=(T // tq,),
        in_specs=[col(0), col(1), col(0),
                  pl.BlockSpec((heads, LANES, LANES), lambda i: (0, 0, 0))],
        out_specs=[col(0)] * 3,
        out_shape=[jax.ShapeDtypeStruct((T, W), BF16)] * 3,
        compiler_params=_cparams("parallel"),
        name="moba_gate",
    )(qk, qk, plain, kmean_pad)


def _flash_body(q_ref, k_ref, v_ref, o_ref, m_ref, acc_ref, *, tq, l_lane):
    i = pl.program_id(1)
    m_ref[...] = jnp.full(m_ref.shape, -jnp.inf, F32)
    acc_ref[...] = jnp.zeros(acc_ref.shape, F32)

    def step(r0, nr, c0, nc, masked):
        for hh in range(2):
            hsl = slice(hh * LANES, (hh + 1) * LANES)
            s = lax.dot_general(q_ref[r0:r0 + nr, hsl], k_ref[pl.ds(c0, nc), hsl],
                                (((1,), (1,)), ((), ())), preferred_element_type=F32)
            if masked:
                row = i * tq + r0 + lax.broadcasted_iota(jnp.int32, (nr, nc), 0)
                col = c0 + lax.broadcasted_iota(jnp.int32, (nr, nc), 1)
                s = jnp.where(col <= row, s, -jnp.inf)
            m_prev = m_ref[hh, r0:r0 + nr]
            m_new = jnp.maximum(m_prev, jnp.max(s, axis=-1, keepdims=True))
            p = jnp.exp2(s - pltpu.repeat(m_new, nc // LANES, axis=1))
            acc_ref[hh, r0:r0 + nr] = jnp.exp2(m_prev - m_new) * acc_ref[hh, r0:r0 + nr] + jnp.dot(
                p.astype(BF16), v_ref[pl.ds(c0, nc), hsl], preferred_element_type=F32)
            m_ref[hh, r0:r0 + nr] = m_new

    def loop_body(j, carry):
        step(0, tq, pl.multiple_of(j * tq, tq), tq, False)
        return carry

    lax.fori_loop(0, i, loop_body, 0)
    half = tq // 2
    d0 = pl.multiple_of(i * tq, tq)
    step(0, half, d0, half, True)
    step(half, half, d0, half, False)
    step(half, half, pl.multiple_of(d0 + half, half), half, True)
    outs = [acc_ref[hh] / acc_ref[hh][:, l_lane:l_lane + 1] for hh in range(2)]
    lane = lax.broadcasted_iota(jnp.int32, (tq, LANES), 1)
    o_ref[...] = jnp.where(lane < l_lane, outs[0],
                           pltpu.roll(outs[1], l_lane, axis=1)).astype(o_ref.dtype)


def _flash(q, k, v, heads, *, q_off=0, k_off=0, v_off=0, l_lane, tq=1024):
    T = q.shape[0]
    tq = min(tq, T)
    assert heads % 2 == 0 and T % tq == 0 and tq % (2 * LANES) == 0 and 2 * l_lane == LANES
    pair = 2 * LANES
    return pl.pallas_call(
        functools.partial(_flash_body, tq=tq, l_lane=l_lane),
        grid=(heads // 2, T // tq),
        in_specs=[pl.BlockSpec((tq, pair), lambda h, i: (i, q_off + h)),
                  pl.BlockSpec((T, pair), lambda h, i: (0, k_off + h)),
                  pl.BlockSpec((T, pair), lambda h, i: (0, v_off + h))],
        out_specs=pl.BlockSpec((tq, LANES), lambda h, i: (i, h)),
        out_shape=jax.ShapeDtypeStruct((T, heads * l_lane), BF16),
        scratch_shapes=[pltpu.VMEM((2, tq, LANES), F32)] * 2,
        compiler_params=_cparams("parallel", "arbitrary"),
        name="flash",
    )(q, k, v)


def _retention_body(q_ref, k_ref, v_ref, g_ref, gn_ref, dm_ref, xi_ref, ze_ref, gc_ref,
                    o_ref, r_ref):
    @pl.when(pl.program_id(1) == 0)
    def _():
        r_ref[...] = jnp.zeros_like(r_ref)

    q = q_ref[...]
    k = k_ref[...]
    v = v_ref[...]
    r_old = r_ref[...]
    inner = lax.dot_general(q, k, (((1,), (1,)), ((), ())), preferred_element_type=F32) * dm_ref[0]
    out = jnp.dot(inner.astype(BF16), v, preferred_element_type=F32)
    out = out + jnp.dot(q, r_old.astype(BF16), preferred_element_type=F32) * xi_ref[0]
    kz = (k.astype(F32) * ze_ref[0]).T.astype(BF16)
    r_ref[...] = r_old * gc_ref[0] + jnp.dot(kz, v, preferred_element_type=F32)
    mu = jnp.mean(out, axis=-1, keepdims=True)
    cen = out - mu
    var = jnp.mean(cen * cen, axis=-1, keepdims=True)
    rn = cen * lax.rsqrt(var + EPS) * gn_ref[...]
    gate = g_ref[...].astype(F32)
    o_ref[...] = (rn * (gate / (1.0 + jnp.exp(-gate)))).astype(o_ref.dtype)


def _retention(qk, plain, gn, heads, *, q_off, k_off, v_off, g_off):
    T = qk.shape[0]
    W = heads * LANES
    C = min(RET_CHUNK, T)
    assert T % C == 0
    log_g = jnp.log(1.0 - 2.0 ** (-5.0 - jnp.arange(heads, dtype=F32)))
    pos = jnp.arange(C, dtype=F32)
    diff = pos[:, None] - pos[None, :]
    dmat = jnp.where(diff >= 0, jnp.exp(log_g[:, None, None] * jnp.maximum(diff, 0.0)), 0.0)
    rep = lambda t: jnp.broadcast_to(t[..., None], t.shape + (LANES,))
    xi = rep(jnp.exp(log_g[:, None] * (pos + 1.0)))
    zeta = rep(jnp.exp(log_g[:, None] * (C - 1.0 - pos)))
    g_chunk = rep(jnp.exp(log_g * C)[:, None])
    tile = lambda off: pl.BlockSpec((C, LANES), lambda h, c: (c, off + h))
    head_tab = lambda r: pl.BlockSpec((1, r, LANES), lambda h, c: (h, 0, 0))
    return pl.pallas_call(
        _retention_body,
        grid=(heads, T // C),
        in_specs=[tile(q_off), tile(k_off), tile(v_off), tile(g_off),
                  pl.BlockSpec((1, LANES), lambda h, c: (0, h)),
                  pl.BlockSpec((1, C, C), lambda h, c: (h, 0, 0)),
                  head_tab(C), head_tab(C), head_tab(1)],
        out_specs=tile(0),
        out_shape=jax.ShapeDtypeStruct((T, W), BF16),
        scratch_shapes=[pltpu.VMEM((LANES, LANES), F32)],
        compiler_params=_cparams("parallel", "arbitrary"),
        name="retention",
    )(qk, qk, plain, plain, gn.reshape(1, W).astype(F32), dmat, xi, zeta, g_chunk)


def _mla_mid_body(d_ref, qn_ref, kvn_ref, c_ref, s_ref, cq_ref, ckv_ref):
    def rms(x, g):
        return x * lax.rsqrt(jnp.mean(x * x, axis=-1, keepdims=True) + EPS) * g

    cq_ref[...] = rms(d_ref[:, :MLA_Q_RANK], qn_ref[...]).astype(BF16)
    lo = MLA_Q_RANK + MLA_KV_RANK
    ckv_ref[:, :MLA_KV_RANK] = rms(d_ref[:, MLA_Q_RANK:lo], kvn_ref[...]).astype(BF16)
    kr = d_ref[:, lo:lo + LANES] * c_ref[...] + d_ref[:, lo + LANES:lo + 2 * LANES] * s_ref[...]
    lane = lax.broadcasted_iota(jnp.int32, kr.shape, 1)
    ckv_ref[:, MLA_KV_RANK:] = jnp.where(lane == MLA_ROPE, 1.0, kr).astype(BF16)


def _mla_mid(down, q_norm, kv_norm, cos, sin, *, tm=256):
    T, W = down.shape
    wide = MLA_KV_RANK + LANES
    return pl.pallas_call(
        _mla_mid_body,
        grid=(T // tm,),
        in_specs=[pl.BlockSpec((tm, W), lambda i: (i, 0)),
                  pl.BlockSpec((1, MLA_Q_RANK), lambda i: (0, 0)),
                  pl.BlockSpec((1, MLA_KV_RANK), lambda i: (0, 0)),
                  pl.BlockSpec((tm, LANES), lambda i: (i, 0)),
                  pl.BlockSpec((tm, LANES), lambda i: (i, 0))],
        out_specs=[pl.BlockSpec((tm, MLA_Q_RANK), lambda i: (i, 0)),
                   pl.BlockSpec((tm, wide), lambda i: (i, 0))],
        out_shape=[jax.ShapeDtypeStruct((T, MLA_Q_RANK), BF16),
                   jax.ShapeDtypeStruct((T, wide), BF16)],
        compiler_params=_cparams("parallel"),
        name="mla_mid",
    )(down, q_norm.reshape(1, -1).astype(F32), kv_norm.reshape(1, -1).astype(F32), cos, sin)


PEER_CAND = [(i, PEER_TOPK // (i + 1)) for i in range(PEER_TOPK)]
PEER_NCAND = 64
assert sum(c for _, c in PEER_CAND) <= PEER_NCAND
PEER_NORANK = 64.0


def _oddeven_merge_sort(n):
    pairs, p = [], 1
    while p < n:
        k = p
        while k >= 1:
            for j in range(k % p, n - k, 2 * k):
                for i in range(min(k, n - j - k)):
                    if (i + j) // (2 * p) == (i + j + k) // (2 * p):
                        pairs.append((i + j, i + j + k))
            k //= 2
        p *= 2
    return pairs


_SORT16 = _oddeven_merge_sort(PEER_NKEYS // 8)
_SORT_CAND = _oddeven_merge_sort(PEER_NCAND // 8)


def _walk_best(x, network, count, emit):
    rows = [x[8 * k:8 * k + 8, :] for k in range(x.shape[0] // 8)]
    depth = len(rows)
    rows.append(jnp.full(rows[0].shape, -jnp.inf, F32))
    for i, j in network:
        rows[i], rows[j] = jnp.maximum(rows[i], rows[j]), jnp.minimum(rows[i], rows[j])
    for r in range(count):
        m = jnp.max(rows[0], axis=0, keepdims=True)
        emit(r, m)
        hit = rows[0] == m
        for i in range(min(depth, count - 1 - r)):
            rows[i] = jnp.where(hit, rows[i + 1], rows[i])


def _peer_topk_body(x_ref, g_ref, wq_ref, kt_ref, xn_ref, pkf_ref, pkb_ref,
                    st_ref, t1_ref, t2_ref, cand_ref):
    x = x_ref[...]
    xn = (x * lax.rsqrt(jnp.mean(x * x, axis=-1, keepdims=True) + EPS) * g_ref[...]).astype(BF16)
    xn_ref[...] = xn
    qry = jnp.dot(xn, wq_ref[...], preferred_element_type=F32).astype(BF16)
    st_ref[...] = lax.dot_general(kt_ref[...], qry, (((1,), (1,)), ((), ())),
                                  preferred_element_type=F32)
    n = PEER_NKEYS

    def store_rows(t_ref):
        def emit(r, m):
            t_ref[r:r + 1, :] = m
        return emit

    for h in range(PEER_HEADS):
        s1 = st_ref[(2 * h) * n:(2 * h + 1) * n, :]
        s2 = st_ref[(2 * h + 1) * n:(2 * h + 2) * n, :]
        _walk_best(s1, _SORT16, PEER_TOPK, store_rows(t1_ref))
        _walk_best(s2, _SORT16, PEER_TOPK, store_rows(t2_ref))
        rank2 = jnp.full(s2.shape, PEER_NORANK, F32)
        for r in range(PEER_TOPK):
            rank2 = jnp.where(s2 == t2_ref[r:r + 1, :], float(r), rank2)
        cand_ref[...] = jnp.full(cand_ref.shape, -jnp.inf, F32)
        rowp = 0
        for i, cnt in PEER_CAND:
            cand_ref[rowp:rowp + cnt, :] = t1_ref[i:i + 1, :] + t2_ref[0:cnt, :]
            rowp += cnt
        top1 = t1_ref[0:1, :]
        top2 = t2_ref[0:1, :]
        cmax = top1 + top2
        z = jnp.zeros_like(cmax)
        best = []
        _walk_best(cand_ref[...], _SORT_CAND, PEER_TOPK, lambda r, m: best.append(m))
        for m in best:
            z = z + jnp.exp(m - cmax)
        kth = best[-1]
        cnt = jnp.zeros(s1.shape, F32)
        for j in range(PEER_TOPK // 2):
            cnt = cnt + jnp.where(s1 + t2_ref[j:j + 1, :] >= kth, 1.0, 0.0)
        cnt_best = jnp.zeros_like(top1)
        for j in range(PEER_TOPK):
            cnt_best = cnt_best + jnp.where(top1 + t2_ref[j:j + 1, :] >= kth, 1.0, 0.0)
        cnt = jnp.where(s1 == top1, cnt_best, cnt)
        pkf_ref[h, 0] = jnp.exp(s1 - top1) / z
        pkf_ref[h, 1] = cnt
        pkb_ref[h, 0] = rank2.astype(BF16)
        pkb_ref[h, 1] = jnp.exp(s2 - top2).astype(BF16)


def _peer_topk(h, gain, wq, keys_t, *, tm=256):
    T, D = h.shape
    R = keys_t.shape[0]
    tm = min(tm, T)
    blk = lambda i: (0, 0, 0, i)
    shape = (PEER_HEADS, 2, PEER_NKEYS, T)
    return pl.pallas_call(
        _peer_topk_body,
        grid=(T // tm,),
        in_specs=[pl.BlockSpec((tm, D), lambda i: (i, 0)),
                  pl.BlockSpec((1, D), lambda i: (0, 0)),
                  pl.BlockSpec(wq.shape, lambda i: (0, 0)),
                  pl.BlockSpec(keys_t.shape, lambda i: (0, 0))],
        out_specs=[pl.BlockSpec((tm, D), lambda i: (i, 0))]
        + [pl.BlockSpec((PEER_HEADS, 2, PEER_NKEYS, tm), blk)] * 2,
        out_shape=[jax.ShapeDtypeStruct((T, D), BF16),
                   jax.ShapeDtypeStruct(shape, F32), jax.ShapeDtypeStruct(shape, BF16)],
        scratch_shapes=[pltpu.VMEM((R, tm), F32),
                        pltpu.VMEM((PEER_TOPK + 8, tm), F32),
                        pltpu.VMEM((PEER_TOPK + 8, tm), F32),
                        pltpu.VMEM((PEER_NCAND, tm), F32)],
        compiler_params=_cparams("parallel"),
        name="peer_topk",
    )(h, gain.reshape(1, D).astype(F32), wq, keys_t)


def _peer_dense_body(xn_ref, u_ref, v_ref, pkf_ref, pkb_ref, h_ref, *rest,
                     tm, te, n_e, n_steps, out_norm):
    og_ref = rest[0] if out_norm else None
    o_ref, ht0_ref, ht1_ref, acc_ref = rest[-4:]
    s = pl.program_id(0)
    n = PEER_NKEYS
    group = 2

    def scores(dst_ref):
        dst_ref[...] = lax.dot_general(u_ref[...], xn_ref[...], (((1,), (1,)), ((), ())),
                                       preferred_element_type=F32)

    def scores_half(dst_ref, half, anchor):
        hw = tm // 2
        bits = pltpu.bitcast(anchor[0:16, 0:LANES], jnp.uint32)
        zero = ((bits >> 16) >> 16)[0, 0].astype(jnp.int32)
        xs = xn_ref[pl.ds(pl.multiple_of(half * hw + zero * hw, hw), hw), :]
        dst_ref[:, half * hw:(half + 1) * hw] = lax.dot_general(
            u_ref[...], xs, (((1,), (1,)), ((), ())), preferred_element_type=F32)

    def experts(src_ref, dst_ref=None):
        e = lax.rem(s - 1, n_e)
        n_groups = te // (group * n)
        for gb in range(n_groups):
            if dst_ref is not None and gb in (1, n_groups // 2 + 1):
                scores_half(dst_ref, int(gb > 1), acts[0])
            acts = []
            for ab in range(gb * group, (gb + 1) * group):
                a = e * (te // n) + ab
                gsum = jnp.zeros((n, tm), BF16)
                for h in range(PEER_HEADS):
                    w1 = jnp.broadcast_to(pkf_ref[h, 0, pl.ds(a, 1), :], (n, tm)).astype(BF16)
                    cnt = jnp.broadcast_to(pkf_ref[h, 1, pl.ds(a, 1), :], (n, tm)).astype(BF16)
                    gsum = gsum + jnp.where(pkb_ref[h, 0] < cnt, pkb_ref[h, 1], 0.0) * w1
                hs = src_ref[ab * n:(ab + 1) * n, :]
                act = 0.5 * hs * (1.0 + lax.erf(hs * np.float32(1.0 / np.sqrt(2.0))))
                acts.append(act.astype(BF16) * gsum)
            rows = slice(gb * group * n, (gb + 1) * group * n)
            acc_ref[...] += lax.dot_general(v_ref[rows, :], jnp.concatenate(acts, axis=0),
                                            (((0,), (0,)), ((), ())),
                                            preferred_element_type=F32)

    even = lax.rem(s, 2) == 0
    steady = (s > 0) & (s < n_steps)

    @pl.when((s >= 2) & (lax.rem(s - 1, n_e) == 0))
    def _():
        y = h_ref[...] + acc_ref[...].T
        if out_norm:
            y = y * lax.rsqrt(jnp.mean(y * y, axis=-1, keepdims=True) + EPS) * og_ref[...]
        o_ref[...] = y
        acc_ref[...] = jnp.zeros_like(acc_ref)

    @pl.when(s == 0)
    def _():
        acc_ref[...] = jnp.zeros_like(acc_ref)
        scores(ht0_ref)

    @pl.when(steady & even)
    def _():
        experts(ht1_ref, ht0_ref)

    @pl.when(steady & jnp.logical_not(even))
    def _():
        experts(ht0_ref, ht1_ref)

    @pl.when(s == n_steps)
    def _():
        experts(ht1_ref if n_steps % 2 == 0 else ht0_ref)


def _peer_dense(xn, u, v, pkf, pkb, h, *, out_gain=None, tm=512, te=1024):
    T, D = xn.shape
    E = u.shape[0]
    tm = min(tm, T)
    assert T % tm == 0 and E % te == 0 and te % (2 * PEER_NKEYS) == 0
    n_e = E // te
    n_steps = (T // tm) * n_e
    pair = lambda s, lag: jnp.clip(s - lag, 0, n_steps - 1)
    cur = lambda s: pair(s, 0)
    prev = lambda s: pair(s, 1)
    done = lambda s: pair(s, 2)
    pk_spec = pl.BlockSpec((PEER_HEADS, 2, PEER_NKEYS, tm), lambda s: (0, 0, 0, prev(s) // n_e))
    return pl.pallas_call(
        functools.partial(_peer_dense_body, tm=tm, te=te, n_e=n_e, n_steps=n_steps,
                          out_norm=out_gain is not None),
        grid=(n_steps + 2,),
        in_specs=[pl.BlockSpec((tm, D), lambda s: (cur(s) // n_e, 0)),
                  pl.BlockSpec((te, D), lambda s: (cur(s) % n_e, 0)),
                  pl.BlockSpec((te, D), lambda s: (prev(s) % n_e, 0)),
                  pk_spec, pk_spec,
                  pl.BlockSpec((tm, D), lambda s: (done(s) // n_e, 0))]
        + ([pl.BlockSpec((1, D), lambda s: (0, 0))] if out_gain is not None else []),
        out_specs=pl.BlockSpec((tm, D), lambda s: (done(s) // n_e, 0)),
        out_shape=jax.ShapeDtypeStruct((T, D), F32),
        scratch_shapes=[pltpu.VMEM((te, tm), F32), pltpu.VMEM((te, tm), F32),
                        pltpu.VMEM((D, tm), F32)],
        compiler_params=_cparams("arbitrary"),
        name="peer_dense",
    )(xn, u, v, pkf, pkb, h,
      *([out_gain.reshape(1, D).astype(F32)] if out_gain is not None else []))


def _peer_ffn(h, gain, wq, keys, u_tab, v_tab, out_gain=None):
    nk, dh = PEER_NKEYS, PEER_DKEY // 2
    groups = PEER_HEADS * 2
    keys_t = jnp.einsum("gnd,gk->gnkd", keys.reshape(groups, nk, dh).astype(F32),
                        jnp.eye(groups, dtype=F32)).reshape(groups * nk, groups * dh).astype(BF16)
    xn, pkf, pkb = _peer_topk(h, gain, wq.astype(BF16), keys_t)
    return _peer_dense(xn, u_tab.astype(BF16), v_tab.astype(BF16), pkf, pkb, h, out_gain=out_gain)


def _lane_tables(T, rot_dim, theta, *, rot_at, keep, scale):
    r = rot_dim // 2
    inv = 1.0 / (theta ** (jnp.arange(0, rot_dim, 2, dtype=F32) / rot_dim))
    lane = np.arange(LANES)
    in_rot = (lane >= rot_at) & (lane < rot_at + rot_dim)
    inv_lane = jnp.where(jnp.asarray(in_rot), inv[np.where(in_rot, (lane - rot_at) % r, 0)], 0.0)
    ang = jnp.arange(T, dtype=F32)[:, None] * inv_lane[None, :]
    c = jnp.cos(ang) * jnp.asarray((lane < keep) * scale, F32)[None, :]
    s = jnp.sin(ang) * jnp.asarray(in_rot * scale, F32)[None, :]
    return c, s


def _head_cols(n_heads, src_stride, src_off, width, *, dst_stride=LANES, dst_off=0):
    idx = np.zeros(n_heads * dst_stride, np.int32)
    sgn = np.zeros(n_heads * dst_stride, np.float32)
    for h in range(n_heads):
        d = h * dst_stride + dst_off
        idx[d:d + width] = h * src_stride + src_off + np.arange(width)
        sgn[d:d + width] = 1.0
    return idx, sgn


def _rot_cols(n_heads, src_stride, src_off, r, *, dst_stride=LANES, dst_off=0):
    idx = np.zeros(n_heads * dst_stride, np.int32)
    sgn = np.zeros(n_heads * dst_stride, np.float32)
    for h in range(n_heads):
        d = h * dst_stride + dst_off
        s = h * src_stride + src_off
        idx[d:d + r] = s + r + np.arange(r)
        sgn[d:d + r] = -1.0
        idx[d + r:d + 2 * r] = s + np.arange(r)
        sgn[d + r:d + 2 * r] = 1.0
    return idx, sgn


def _take_cols(w, idx_sgn):
    idx, sgn = idx_sgn
    return (jnp.take(w, jnp.asarray(idx), axis=1) * jnp.asarray(sgn)[None, :]).astype(BF16)


def _even_mixer(h, norm_g, w_in, ret_gn, w_o):
    T = h.shape[0]
    mw = MOBA_HEADS * MOBA_HEAD_DIM
    rw = RET_HEADS * RET_DK
    vw = RET_HEADS * RET_DV
    o_mq, o_mk, o_mv, o_rq, o_rk, o_rv, o_rg = np.cumsum([0, mw, mw, mw, rw, rw, vw])

    def heads128(off, heads, stride):
        return _take_cols(w_in[:, off:off + heads * stride], _head_cols(heads, stride, 0, stride))

    seg_w = MOBA_HEADS * LANES
    mtab = functools.partial(_lane_tables, T, MOBA_ROT, ROPE_THETA, rot_at=0, keep=MOBA_HEAD_DIM)
    rtab = functools.partial(_lane_tables, T, RET_DK, RET_THETA, rot_at=0, keep=RET_DK)
    cat = lambda a, b: jnp.concatenate([a, b], axis=1)
    (cq, sq), (ck, sk) = mtab(scale=MOBA_HEAD_DIM ** -0.5 * LOG2E), mtab(scale=1.0)
    qk_m, colmean = _proj(h, cat(heads128(o_mq, MOBA_HEADS, MOBA_HEAD_DIM),
                                 heads128(o_mk, MOBA_HEADS, MOBA_HEAD_DIM)),
                          gain=norm_g, rot=(0, MOBA_ROT // 2), cos=cat(cq, ck), sin=cat(sq, sk),
                          seg=seg_w, colmean=MOBA_BLOCK)
    (cq, sq), (ck, sk) = rtab(scale=1.0), rtab(scale=RET_DK ** -0.5)
    qk_r = _proj(h, cat(heads128(o_rq, RET_HEADS, RET_DK), heads128(o_rk, RET_HEADS, RET_DK)),
                 gain=norm_g, rot=(0, RET_DK // 2), cos=cat(cq, ck), sin=cat(sq, sk), seg=seg_w)
    w_plain = jnp.concatenate(
        [_take_cols(w_in[:, o_mv:o_rq], _head_cols(MOBA_HEADS, MOBA_HEAD_DIM, 0, MOBA_HEAD_DIM)),
         w_in[:, o_rv:].astype(BF16)], axis=1)
    plain = _proj(h, w_plain, gain=norm_g)
    nb = T // MOBA_BLOCK
    km = colmean[:, 0, seg_w:2 * seg_w].reshape(nb, MOBA_HEADS, LANES).transpose(1, 0, 2)
    km = jnp.pad(km, ((0, 0), (MOBA_HEAD_DIM, LANES - MOBA_HEAD_DIM - nb), (0, 0)))
    mq_b, mk_b, mv_b = _moba_gate(qk_m, plain, km, MOBA_HEADS)
    a_out = _flash(mq_b, mk_b, mv_b, MOBA_HEADS, l_lane=MOBA_HEAD_DIM)
    b_out = _retention(qk_r, plain, ret_gn, RET_HEADS, q_off=0, k_off=RET_HEADS,
                       v_off=MOBA_HEADS, g_off=MOBA_HEADS + RET_HEADS)
    return _matmul_res([a_out, b_out], [w_o[:mw].astype(BF16), w_o[mw:].astype(BF16)], h)


def _odd_mixer(h, norm_g, w_down, q_norm, w_uq, kv_norm, w_ukv, w_o):
    T = h.shape[0]
    lat = MLA_Q_RANK + MLA_KV_RANK
    half = MLA_ROPE // 2
    dq = MLA_NOPE + MLA_ROPE
    kw = MLA_HEADS * LANES
    w_dn = jnp.concatenate(
        [w_down[:, :lat].astype(BF16),
         _take_cols(w_down[:, lat:], _head_cols(1, MLA_ROPE, 0, MLA_ROPE)),
         _take_cols(w_down[:, lat:], _rot_cols(1, MLA_ROPE, 0, half))], axis=1)
    down = _proj(h, w_dn, gain=norm_g, out_dtype=F32)
    ck, sk = _lane_tables(T, MLA_ROPE, ROPE_THETA, rot_at=0, keep=MLA_ROPE, scale=1.0)
    cqn, ckvx = _mla_mid(down, q_norm, kv_norm, ck, sk)
    cq_t, sq_t = _lane_tables(T, MLA_ROPE, ROPE_THETA, rot_at=MLA_NOPE, keep=dq,
                              scale=dq ** -0.5 * LOG2E)
    q = _proj(cqn, _take_cols(w_uq, _head_cols(MLA_HEADS, dq, 0, dq)),
              rot=(MLA_NOPE, half), cos=cq_t, sin=sq_t, seg=kw)
    kvw = MLA_NOPE + MLA_V
    place_k = np.zeros((LANES, kw), np.float32)
    place_v = np.zeros((LANES, kw), np.float32)
    for hh in range(MLA_HEADS):
        place_k[np.arange(MLA_ROPE), hh * LANES + MLA_NOPE + np.arange(MLA_ROPE)] = 1.0
        place_v[MLA_ROPE, hh * LANES + MLA_V] = 1.0
    wk = jnp.concatenate([_take_cols(w_ukv, _head_cols(MLA_HEADS, kvw, 0, MLA_NOPE)),
                          jnp.asarray(place_k, BF16)], axis=0)
    wv = jnp.concatenate([_take_cols(w_ukv, _head_cols(MLA_HEADS, kvw, MLA_NOPE, MLA_V)),
                          jnp.asarray(place_v, BF16)], axis=0)
    kv = _proj(ckvx, jnp.concatenate([wk, wv], axis=1))
    o = _flash(q, kv, kv, MLA_HEADS, v_off=MLA_HEADS // 2, l_lane=MLA_V)
    return _matmul_res([o], [w_o.astype(BF16)], h)


def kernel(x, attn_norm, ffn_norm, ev_w_in, ev_ret_gn, ev_w_o, od_w_down, od_q_norm, od_w_uq,
           od_kv_norm, od_w_ukv, od_w_o, peer_wq, peer_keys, peer_u, peer_v, final_norm):
    B, S, D = x.shape
    assert B == 1
    h = x.reshape(S, D)
    depth = attn_norm.shape[0]
    for i in range(depth):
        j = i // 2
        if i % 2 == 0:
            h = _even_mixer(h, attn_norm[i], ev_w_in[j], ev_ret_gn[j], ev_w_o[j])
        else:
            h = _odd_mixer(h, attn_norm[i], od_w_down[j], od_q_norm[j], od_w_uq[j],
                           od_kv_norm[j], od_w_ukv[j], od_w_o[j])
        h = _peer_ffn(h, ffn_norm[i], peer_wq[i], peer_keys[i], peer_u[i], peer_v[i],
                      out_gain=final_norm if i == depth - 1 else None)
    return h.reshape(B, S, D)
```

```python
import functools

import numpy as np
import jax
import jax.numpy as jnp
from jax import lax
from jax.experimental import pallas as pl
from jax.experimental.pallas import tpu as pltpu

F32 = jnp.float32
BF16 = jnp.bfloat16

LANES = 128
VMEM_LIMIT = 56 * 1024 * 1024

D_MODEL = 1024
EPS = 1e-6
ROPE_THETA = 500000.0
LOG2E = float(np.log2(np.e))

MOBA_HEADS = 8
MOBA_HEAD_DIM = 64
MOBA_ROT = MOBA_HEAD_DIM // 4
MOBA_BLOCK = 256
MOBA_TOPK = 3
MASK_BIAS = -1e9

RET_HEADS = 8
RET_DK = 64
RET_DV = 128
RET_THETA = 10000.0
RET_CHUNK = 512

MLA_HEADS = 16
MLA_NOPE = 64
MLA_ROPE = 32
MLA_V = 64
MLA_Q_RANK = 512
MLA_KV_RANK = 256

PEER_HEADS = 8
PEER_NKEYS = 128
PEER_DKEY = 128
PEER_TOPK = 16


def _cparams(*sem):
    return pltpu.CompilerParams(dimension_semantics=sem, vmem_limit_bytes=VMEM_LIMIT)


def _proj_body(*refs, norm, rot, colmean, tn):
    it = iter(refs)
    x_ref = next(it)
    g_ref = next(it) if norm else None
    w_ref = next(it)
    if rot:
        c_ref, s_ref = next(it), next(it)
    o_ref = next(it)
    cm_ref = next(it) if colmean else None
    xn_ref = next(it)

    @pl.when(pl.program_id(1) == 0)
    def _():
        x = x_ref[...].astype(F32)
        if norm:
            x = x * lax.rsqrt(jnp.mean(x * x, axis=-1, keepdims=True) + EPS) * g_ref[...]
        xn_ref[...] = x.astype(BF16)

    xn = xn_ref[...]
    y = jnp.dot(xn, w_ref[...], preferred_element_type=F32)
    if not rot:
        o_ref[...] = y.astype(o_ref.dtype)
        return
    rot_at, r = rot
    c = c_ref[...]
    s = s_ref[...]
    lane = lax.broadcasted_iota(jnp.int32, c.shape, 1)
    first = ((lane - rot_at) & (2 * r - 1)) < r
    for k in range(tn // LANES):
        sl = slice(k * LANES, (k + 1) * LANES)
        yk = y[:, sl]
        half = jnp.where(first, -pltpu.roll(yk, LANES - r, axis=1), pltpu.roll(yk, r, axis=1))
        val = yk * c + half * s
        o_ref[:, sl] = val.astype(o_ref.dtype)
        if colmean:
            for b in range(val.shape[0] // colmean):
                cm_ref[b, :, sl] = jnp.mean(val[b * colmean:(b + 1) * colmean], axis=0,
                                            keepdims=True)


def _proj(x, w, *, gain=None, rot=None, cos=None, sin=None, seg=None,
          colmean=None, out_dtype=BF16, tm=1024, tn=1024):
    T, K = x.shape
    N = w.shape[1]
    tm, tn = min(tm, T), min(tn, N)
    assert T % tm == 0 and N % tn == 0 and tn % LANES == 0
    norm = gain is not None
    in_specs = [pl.BlockSpec((tm, K), lambda i, j: (i, 0))]
    args = [x]
    if norm:
        in_specs.append(pl.BlockSpec((1, K), lambda i, j: (0, 0)))
        args.append(gain.reshape(1, K).astype(F32))
    in_specs.append(pl.BlockSpec((K, tn), lambda i, j: (0, j)))
    args.append(w)
    if rot:
        assert seg % tn == 0 and rot[0] % (2 * rot[1]) == 0
        tab = lambda i, j: (i, (j * tn) // seg)
        in_specs += [pl.BlockSpec((tm, LANES), tab), pl.BlockSpec((tm, LANES), tab)]
        args += [cos, sin]
    out_shape = [jax.ShapeDtypeStruct((T, N), out_dtype)]
    out_specs = [pl.BlockSpec((tm, tn), lambda i, j: (i, j))]
    if colmean:
        assert rot and tm % colmean == 0
        out_shape.append(jax.ShapeDtypeStruct((T // colmean, 1, N), F32))
        out_specs.append(pl.BlockSpec((tm // colmean, 1, tn), lambda i, j: (i, 0, j)))
    res = pl.pallas_call(
        functools.partial(_proj_body, norm=norm, rot=rot, colmean=colmean, tn=tn),
        grid=(T // tm, N // tn),
        in_specs=in_specs, out_specs=out_specs, out_shape=out_shape,
        scratch_shapes=[pltpu.VMEM((tm, K), BF16)],
        compiler_params=_cparams("parallel", "arbitrary"),
        name="proj",
    )(*args)
    return res if colmean else res[0]


def _matmul_res_body(*refs, n_in):
    xs, ws = refs[:n_in], refs[n_in:2 * n_in]
    r_ref, o_ref = refs[2 * n_in], refs[2 * n_in + 1]
    acc = r_ref[...]
    for x_ref, w_ref in zip(xs, ws):
        acc = acc + jnp.dot(x_ref[...], w_ref[...], preferred_element_type=F32)
    o_ref[...] = acc


def _matmul_res(xs, ws, res, *, tm=512, tn=512):
    T, N = res.shape
    n_in = len(xs)
    in_specs = [pl.BlockSpec((tm, x.shape[1]), lambda i, j: (i, 0)) for x in xs]
    in_specs += [pl.BlockSpec((w.shape[0], tn), lambda i, j: (0, j)) for w in ws]
    in_specs.append(pl.BlockSpec((tm, tn), lambda i, j: (i, j)))
    return pl.pallas_call(
        functools.partial(_matmul_res_body, n_in=n_in),
        grid=(T // tm, N // tn),
        in_specs=in_specs,
        out_specs=pl.BlockSpec((tm, tn), lambda i, j: (i, j)),
        out_shape=jax.ShapeDtypeStruct((T, N), F32),
        compiler_params=_cparams("parallel", "parallel"),
        name="matmul_res",
    )(*xs, *ws, res)


def _moba_gate_body(q_ref, k_ref, v_ref, km_ref, qo_ref, ko_ref, vo_ref, *, tq, heads):
    blk = pl.program_id(0)
    lane = lax.broadcasted_iota(jnp.int32, (tq, LANES), 1)
    bidx = lane - MOBA_HEAD_DIM
    slot = lax.broadcasted_iota(jnp.int32, (LANES, tq), 0)
    sblk = slot - MOBA_HEAD_DIM
    for h in range(heads):
        sl = slice(h * LANES, (h + 1) * LANES)
        q = q_ref[:, sl].astype(F32)
        gate = lax.dot_general(km_ref[h], q, (((1,), (1,)), ((), ())),
                               precision=lax.Precision.HIGHEST, preferred_element_type=F32)
        g = jnp.where((sblk >= 0) & (sblk < blk), gate, -jnp.inf)
        sel = sblk == blk
        for _ in range(MOBA_TOPK):
            m = jnp.max(g, axis=0, keepdims=True)
            idx = jnp.min(jnp.where(g == m, slot, 2 * LANES), axis=0, keepdims=True)
            pick = (slot == idx) & (m > -jnp.inf)
            sel = sel | pick
            g = jnp.where(pick, -jnp.inf, g)
        bias = jnp.where(sel, 0.0, MASK_BIAS).T
        qo_ref[:, sl] = jnp.where(bidx < 0, q, bias).astype(BF16)
        ko_ref[:, sl] = jnp.where(bidx == blk, 1.0, k_ref[:, sl].astype(F32)).astype(BF16)
        vo_ref[:, sl] = jnp.where(bidx == 0, 1.0, v_ref[:, sl].astype(F32)).astype(BF16)


def _moba_gate(qk, plain, kmean_pad, heads):
    T = qk.shape[0]
    W = heads * LANES
    tq = MOBA_BLOCK
    assert T % tq == 0 and T // tq <= LANES - MOBA_HEAD_DIM
    col = lambda c: pl.BlockSpec((tq, W), lambda i: (i, c))
    return pl.pallas_call(
        functools.partial(_moba_gate_body, tq=tq, heads=heads),
        grid=(T // tq,),
        in_specs=[col(0), col(1), col(0),
                  pl.BlockSpec((heads, LANES, LANES), lambda i: (0, 0, 0))],
        out_specs=[col(0)] * 3,
        out_shape=[jax.ShapeDtypeStruct((T, W), BF16)] * 3,
        compiler_params=_cparams("parallel"),
        name="moba_gate",
    )(qk, qk, plain, kmean_pad)


def _flash_body(q_ref, k_ref, v_ref, o_ref, m_ref, acc_ref, *, tq, l_lane):
    i = pl.program_id(1)
    m_ref[...] = jnp.full(m_ref.shape, -jnp.inf, F32)
    acc_ref[...] = jnp.zeros(acc_ref.shape, F32)

    def step(r0, nr, c0, nc, masked):
        for hh in range(2):
            hsl = slice(hh * LANES, (hh + 1) * LANES)
            s = lax.dot_general(q_ref[r0:r0 + nr, hsl], k_ref[pl.ds(c0, nc), hsl],
                                (((1,), (1,)), ((), ())), preferred_element_type=F32)
            if masked:
                row = i * tq + r0 + lax.broadcasted_iota(jnp.int32, (nr, nc), 0)
                col = c0 + lax.broadcasted_iota(jnp.int32, (nr, nc), 1)
                s = jnp.where(col <= row, s, -jnp.inf)
            m_prev = m_ref[hh, r0:r0 + nr]
            m_new = jnp.maximum(m_prev, jnp.max(s, axis=-1, keepdims=True))
            p = jnp.exp2(s - pltpu.repeat(m_new, nc // LANES, axis=1))
            acc_ref[hh, r0:r0 + nr] = jnp.exp2(m_prev - m_new) * acc_ref[hh, r0:r0 + nr] + jnp.dot(
                p.astype(BF16), v_ref[pl.ds(c0, nc), hsl], preferred_element_type=F32)
            m_ref[hh, r0:r0 + nr] = m_new

    def loop_body(j, carry):
        step(0, tq, pl.multiple_of(j * tq, tq), tq, False)
        return carry

    lax.fori_loop(0, i, loop_body, 0)
    half = tq // 2
    d0 = pl.multiple_of(i * tq, tq)
    step(0, half, d0, half, True)
    step(half, half, d0, half, False)
    step(half, half, pl.multiple_of(d0 + half, half), half, True)
    outs = [acc_ref[hh] / acc_ref[hh][:, l_lane:l_lane + 1] for hh in range(2)]
    lane = lax.broadcasted_iota(jnp.int32, (tq, LANES), 1)
    o_ref[...] = jnp.where(lane < l_lane, outs[0],
                           pltpu.roll(outs[1], l_lane, axis=1)).astype(o_ref.dtype)


def _flash(q, k, v, heads, *, q_off=0, k_off=0, v_off=0, l_lane, tq=1024):
    T = q.shape[0]
    tq = min(tq, T)
    assert heads % 2 == 0 and T % tq == 0 and tq % (2 * LANES) == 0 and 2 * l_lane == LANES
    pair = 2 * LANES
    return pl.pallas_call(
        functools.partial(_flash_body, tq=tq, l_lane=l_lane),
        grid=(heads // 2, T // tq),
        in_specs=[pl.BlockSpec((tq, pair), lambda h, i: (i, q_off + h)),
                  pl.BlockSpec((T, pair), lambda h, i: (0, k_off + h)),
                  pl.BlockSpec((T, pair), lambda h, i: (0, v_off + h))],
        out_specs=pl.BlockSpec((tq, LANES), lambda h, i: (i, h)),
        out_shape=jax.ShapeDtypeStruct((T, heads * l_lane), BF16),
        scratch_shapes=[pltpu.VMEM((2, tq, LANES), F32)] * 2,
        compiler_params=_cparams("parallel", "arbitrary"),
        name="flash",
    )(q, k, v)


def _retention_body(q_ref, k_ref, v_ref, g_ref, gn_ref, dm_ref, xi_ref, ze_ref, gc_ref,
                    o_ref, r_ref):
    @pl.when(pl.program_id(1) == 0)
    def _():
        r_ref[...] = jnp.zeros_like(r_ref)

    q = q_ref[...]
    k = k_ref[...]
    v = v_ref[...]
    r_old = r_ref[...]
    inner = lax.dot_general(q, k, (((1,), (1,)), ((), ())), preferred_element_type=F32) * dm_ref[0]
    out = jnp.dot(inner.astype(BF16), v, preferred_element_type=F32)
    out = out + jnp.dot(q, r_old.astype(BF16), preferred_element_type=F32) * xi_ref[0]
    kz = (k.astype(F32) * ze_ref[0]).T.astype(BF16)
    r_ref[...] = r_old * gc_ref[0] + jnp.dot(kz, v, preferred_element_type=F32)
    mu = jnp.mean(out, axis=-1, keepdims=True)
    cen = out - mu
    var = jnp.mean(cen * cen, axis=-1, keepdims=True)
    rn = cen * lax.rsqrt(var + EPS) * gn_ref[...]
    gate = g_ref[...].astype(F32)
    o_ref[...] = (rn * (gate / (1.0 + jnp.exp(-gate)))).astype(o_ref.dtype)


def _retention(qk, plain, gn, heads, *, q_off, k_off, v_off, g_off):
    T = qk.shape[0]
    W = heads * LANES
    C = min(RET_CHUNK, T)
    assert T % C == 0
    log_g = jnp.log(1.0 - 2.0 ** (-5.0 - jnp.arange(heads, dtype=F32)))
    pos = jnp.arange(C, dtype=F32)
    diff = pos[:, None] - pos[None, :]
    dmat = jnp.where(diff >= 0, jnp.exp(log_g[:, None, None] * jnp.maximum(diff, 0.0)), 0.0)
    rep = lambda t: jnp.broadcast_to(t[..., None], t.shape + (LANES,))
    xi = rep(jnp.exp(log_g[:, None] * (pos + 1.0)))
    zeta = rep(jnp.exp(log_g[:, None] * (C - 1.0 - pos)))
    g_chunk = rep(jnp.exp(log_g * C)[:, None])
    tile = lambda off: pl.BlockSpec((C, LANES), lambda h, c: (c, off + h))
    head_tab = lambda r: pl.BlockSpec((1, r, LANES), lambda h, c: (h, 0, 0))
    return pl.pallas_call(
        _retention_body,
        grid=(heads, T // C),
        in_specs=[tile(q_off), tile(k_off), tile(v_off), tile(g_off),
                  pl.BlockSpec((1, LANES), lambda h, c: (0, h)),
                  pl.BlockSpec((1, C, C), lambda h, c: (h, 0, 0)),
                  head_tab(C), head_tab(C), head_tab(1)],
        out_specs=tile(0),
        out_shape=jax.ShapeDtypeStruct((T, W), BF16),
        scratch_shapes=[pltpu.VMEM((LANES, LANES), F32)],
        compiler_params=_cparams("parallel", "arbitrary"),
        name="retention",
    )(qk, qk, plain, plain, gn.reshape(1, W).astype(F32), dmat, xi, zeta, g_chunk)


def _mla_mid_body(d_ref, qn_ref, kvn_ref, c_ref, s_ref, cq_ref, ckv_ref):
    def rms(x, g):
        return x * lax.rsqrt(jnp.mean(x * x, axis=-1, keepdims=True) + EPS) * g

    cq_ref[...] = rms(d_ref[:, :MLA_Q_RANK], qn_ref[...]).astype(BF16)
    lo = MLA_Q_RANK + MLA_KV_RANK
    ckv_ref[:, :MLA_KV_RANK] = rms(d_ref[:, MLA_Q_RANK:lo], kvn_ref[...]).astype(BF16)
    kr = d_ref[:, lo:lo + LANES] * c_ref[...] + d_ref[:, lo + LANES:lo + 2 * LANES] * s_ref[...]
    lane = lax.broadcasted_iota(jnp.int32, kr.shape, 1)
    ckv_ref[:, MLA_KV_RANK:] = jnp.where(lane == MLA_ROPE, 1.0, kr).astype(BF16)


def _mla_mid(down, q_norm, kv_norm, cos, sin, *, tm=256):
    T, W = down.shape
    wide = MLA_KV_RANK + LANES
    return pl.pallas_call(
        _mla_mid_body,
        grid=(T // tm,),
        in_specs=[pl.BlockSpec((tm, W), lambda i: (i, 0)),
                  pl.BlockSpec((1, MLA_Q_RANK), lambda i: (0, 0)),
                  pl.BlockSpec((1, MLA_KV_RANK), lambda i: (0, 0)),
                  pl.BlockSpec((tm, LANES), lambda i: (i, 0)),
                  pl.BlockSpec((tm, LANES), lambda i: (i, 0))],
        out_specs=[pl.BlockSpec((tm, MLA_Q_RANK), lambda i: (i, 0)),
                   pl.BlockSpec((tm, wide), lambda i: (i, 0))],
        out_shape=[jax.ShapeDtypeStruct((T, MLA_Q_RANK), BF16),
                   jax.ShapeDtypeStruct((T, wide), BF16)],
        compiler_params=_cparams("parallel"),
        name="mla_mid",
    )(down, q_norm.reshape(1, -1).astype(F32), kv_norm.reshape(1, -1).astype(F32), cos, sin)


PEER_CAND = [(i, PEER_TOPK // (i + 1)) for i in range(PEER_TOPK)]
PEER_NCAND = 64
assert sum(c for _, c in PEER_CAND) <= PEER_NCAND
PEER_NORANK = 64.0


def _oddeven_merge_sort(n):
    pairs, p = [], 1
    while p < n:
        k = p
        while k >= 1:
            for j in range(k % p, n - k, 2 * k):
                for i in range(min(k, n - j - k)):
                    if (i + j) // (2 * p) == (i + j + k) // (2 * p):
                        pairs.append((i + j, i + j + k))
            k //= 2
        p *= 2
    return pairs


_SORT16 = _oddeven_merge_sort(PEER_NKEYS // 8)
_SORT_CAND = _oddeven_merge_sort(PEER_NCAND // 8)


def _walk_best(x, network, count, emit):
    rows = [x[8 * k:8 * k + 8, :] for k in range(x.shape[0] // 8)]
    depth = len(rows)
    rows.append(jnp.full(rows[0].shape, -jnp.inf, F32))
    for i, j in network:
        rows[i], rows[j] = jnp.maximum(rows[i], rows[j]), jnp.minimum(rows[i], rows[j])
    for r in range(count):
        m = jnp.max(rows[0], axis=0, keepdims=True)
        emit(r, m)
        hit = rows[0] == m
        for i in range(min(depth, count - 1 - r)):
            rows[i] = jnp.where(hit, rows[i + 1], rows[i])


def _peer_topk_body(x_ref, g_ref, wq_ref, kt_ref, xn_ref, pkf_ref, pkb_ref,
                    st_ref, t1_ref, t2_ref, cand_ref):
    x = x_ref[...]
    xn = (x * lax.rsqrt(jnp.mean(x * x, axis=-1, keepdims=True) + EPS) * g_ref[...]).astype(BF16)
    xn_ref[...] = xn
    qry = jnp.dot(xn, wq_ref[...], preferred_element_type=F32).astype(BF16)
    st_ref[...] = lax.dot_general(kt_ref[...], qry, (((1,), (1,)), ((), ())),
                                  preferred_element_type=F32)
    n = PEER_NKEYS

    def store_rows(t_ref):
        def emit(r, m):
            t_ref[r:r + 1, :] = m
        return emit

    for h in range(PEER_HEADS):
        s1 = st_ref[(2 * h) * n:(2 * h + 1) * n, :]
        s2 = st_ref[(2 * h + 1) * n:(2 * h + 2) * n, :]
        _walk_best(s1, _SORT16, PEER_TOPK, store_rows(t1_ref))
        _walk_best(s2, _SORT16, PEER_TOPK, store_rows(t2_ref))
        rank2 = jnp.full(s2.shape, PEER_NORANK, F32)
        for r in range(PEER_TOPK):
            rank2 = jnp.where(s2 == t2_ref[r:r + 1, :], float(r), rank2)
        cand_ref[...] = jnp.full(cand_ref.shape, -jnp.inf, F32)
        rowp = 0
        for i, cnt in PEER_CAND:
            cand_ref[rowp:rowp + cnt, :] = t1_ref[i:i + 1, :] + t2_ref[0:cnt, :]
            rowp += cnt
        top1 = t1_ref[0:1, :]
        top2 = t2_ref[0:1, :]
        cmax = top1 + top2
        z = jnp.zeros_like(cmax)
        best = []
        _walk_best(cand_ref[...], _SORT_CAND, PEER_TOPK, lambda r, m: best.append(m))
        for m in best:
            z = z + jnp.exp(m - cmax)
        kth = best[-1]
        cnt = jnp.zeros(s1.shape, F32)
        for j in range(PEER_TOPK // 2):
            cnt = cnt + jnp.where(s1 + t2_ref[j:j + 1, :] >= kth, 1.0, 0.0)
        cnt_best = jnp.zeros_like(top1)
        for j in range(PEER_TOPK):
            cnt_best = cnt_best + jnp.where(top1 + t2_ref[j:j + 1, :] >= kth, 1.0, 0.0)
        cnt = jnp.where(s1 == top1, cnt_best, cnt)
        pkf_ref[h, 0] = jnp.exp(s1 - top1) / z
        pkf_ref[h, 1] = cnt
        pkb_ref[h, 0] = rank2.astype(BF16)
        pkb_ref[h, 1] = jnp.exp(s2 - top2).astype(BF16)


def _peer_topk(h, gain, wq, keys_t, *, tm=256):
    T, D = h.shape
    R = keys_t.shape[0]
    tm = min(tm, T)
    blk = lambda i: (0, 0, 0, i)
    shape = (PEER_HEADS, 2, PEER_NKEYS, T)
    return pl.pallas_call(
        _peer_topk_body,
        grid=(T // tm,),
        in_specs=[pl.BlockSpec((tm, D), lambda i: (i, 0)),
                  pl.BlockSpec((1, D), lambda i: (0, 0)),
                  pl.BlockSpec(wq.shape, lambda i: (0, 0)),
                  pl.BlockSpec(keys_t.shape, lambda i: (0, 0))],
        out_specs=[pl.BlockSpec((tm, D), lambda i: (i, 0))]
        + [pl.BlockSpec((PEER_HEADS, 2, PEER_NKEYS, tm), blk)] * 2,
        out_shape=[jax.ShapeDtypeStruct((T, D), BF16),
                   jax.ShapeDtypeStruct(shape, F32), jax.ShapeDtypeStruct(shape, BF16)],
        scratch_shapes=[pltpu.VMEM((R, tm), F32),
                        pltpu.VMEM((PEER_TOPK + 8, tm), F32),
                        pltpu.VMEM((PEER_TOPK + 8, tm), F32),
                        pltpu.VMEM((PEER_NCAND, tm), F32)],
        compiler_params=_cparams("parallel"),
        name="peer_topk",
    )(h, gain.reshape(1, D).astype(F32), wq, keys_t)


def _peer_dense_body(xn_ref, u_ref, v_ref, pkf_ref, pkb_ref, h_ref, *rest,
                     tm, te, n_e, n_steps, out_norm):
    og_ref = rest[0] if out_norm else None
    o_ref, ht0_ref, ht1_ref, acc_ref = rest[-4:]
    s = pl.program_id(0)
    n = PEER_NKEYS
    group = 2

    def scores(dst_ref):
        dst_ref[...] = lax.dot_general(u_ref[...], xn_ref[...], (((1,), (1,)), ((), ())),
                                       preferred_element_type=F32)

    def scores_half(dst_ref, half, anchor):
        hw = tm // 2
        bits = pltpu.bitcast(anchor[0:16, 0:LANES], jnp.uint32)
        zero = ((bits >> 16) >> 16)[0, 0].astype(jnp.int32)
        xs = xn_ref[pl.ds(pl.multiple_of(half * hw + zero * hw, hw), hw), :]
        dst_ref[:, half * hw:(half + 1) * hw] = lax.dot_general(
            u_ref[...], xs, (((1,), (1,)), ((), ())), preferred_element_type=F32)

    def experts(src_ref, dst_ref=None):
        e = lax.rem(s - 1, n_e)
        n_groups = te // (group * n)
        for gb in range(n_groups):
            if dst_ref is not None and gb in (1, n_groups // 2 + 1):
                scores_half(dst_ref, int(gb > 1), acts[0])
            acts = []
            for ab in range(gb * group, (gb + 1) * group):
                a = e * (te // n) + ab
                gsum = jnp.zeros((n, tm), BF16)
                for h in range(PEER_HEADS):
                    w1 = jnp.broadcast_to(pkf_ref[h, 0, pl.ds(a, 1), :], (n, tm)).astype(BF16)
                    cnt = jnp.broadcast_to(pkf_ref[h, 1, pl.ds(a, 1), :], (n, tm)).astype(BF16)
                    gsum = gsum + jnp.where(pkb_ref[h, 0] < cnt, pkb_ref[h, 1], 0.0) * w1
                hs = src_ref[ab * n:(ab + 1) * n, :]
                act = 0.5 * hs * (1.0 + lax.erf(hs * np.float32(1.0 / np.sqrt(2.0))))
                acts.append(act.astype(BF16) * gsum)
            rows = slice(gb * group * n, (gb + 1) * group * n)
            acc_ref[...] += lax.dot_general(v_ref[rows, :], jnp.concatenate(acts, axis=0),
                                            (((0,), (0,)), ((), ())),
                                            preferred_element_type=F32)

    even = lax.rem(s, 2) == 0
    steady = (s > 0) & (s < n_steps)

    @pl.when((s >= 2) & (lax.rem(s - 1, n_e) == 0))
    def _():
        y = h_ref[...] + acc_ref[...].T
        if out_norm:
            y = y * lax.rsqrt(jnp.mean(y * y, axis=-1, keepdims=True) + EPS) * og_ref[...]
        o_ref[...] = y
        acc_ref[...] = jnp.zeros_like(acc_ref)

    @pl.when(s == 0)
    def _():
        acc_ref[...] = jnp.zeros_like(acc_ref)
        scores(ht0_ref)

    @pl.when(steady & even)
    def _():
        experts(ht1_ref, ht0_ref)

    @pl.when(steady & jnp.logical_not(even))
    def _():
        experts(ht0_ref, ht1_ref)

    @pl.when(s == n_steps)
    def _():
        experts(ht1_ref if n_steps % 2 == 0 else ht0_ref)


def _peer_dense(xn, u, v, layer, pkf, pkb, h, *, out_gain=None, tm=512, te=1024):
    T, D = xn.shape
    E = u.shape[1]
    tm = min(tm, T)
    assert T % tm == 0 and E % te == 0 and te % (2 * PEER_NKEYS) == 0
    n_e = E // te
    n_steps = (T // tm) * n_e
    pair = lambda s, lag: jnp.clip(s - lag, 0, n_steps - 1)
    cur = lambda s: pair(s, 0)
    prev = lambda s: pair(s, 1)
    done = lambda s: pair(s, 2)
    pk_spec = pl.BlockSpec((PEER_HEADS, 2, PEER_NKEYS, tm), lambda s: (0, 0, 0, prev(s) // n_e))
    return pl.pallas_call(
        functools.partial(_peer_dense_body, tm=tm, te=te, n_e=n_e, n_steps=n_steps,
                          out_norm=out_gain is not None),
        grid=(n_steps + 2,),
        in_specs=[pl.BlockSpec((tm, D), lambda s: (cur(s) // n_e, 0)),
                  pl.BlockSpec((None, te, D), lambda s: (layer, cur(s) % n_e, 0)),
                  pl.BlockSpec((None, te, D), lambda s: (layer, prev(s) % n_e, 0)),
                  pk_spec, pk_spec,
                  pl.BlockSpec((tm, D), lambda s: (done(s) // n_e, 0))]
        + ([pl.BlockSpec((1, D), lambda s: (0, 0))] if out_gain is not None else []),
        out_specs=pl.BlockSpec((tm, D), lambda s: (done(s) // n_e, 0)),
        out_shape=jax.ShapeDtypeStruct((T, D), F32),
        scratch_shapes=[pltpu.VMEM((te, tm), F32), pltpu.VMEM((te, tm), F32),
                        pltpu.VMEM((D, tm), F32)],
        compiler_params=_cparams("arbitrary"),
        name="peer_dense",
    )(xn, u, v, pkf, pkb, h,
      *([out_gain.reshape(1, D).astype(F32)] if out_gain is not None else []))


def _peer_ffn(h, gain, wq, keys, u_all, v_all, layer, out_gain=None):
    nk, dh = PEER_NKEYS, PEER_DKEY // 2
    groups = PEER_HEADS * 2
    keys_t = jnp.einsum("gnd,gk->gnkd", keys.reshape(groups, nk, dh).astype(F32),
                        jnp.eye(groups, dtype=F32)).reshape(groups * nk, groups * dh).astype(BF16)
    xn, pkf, pkb = _peer_topk(h, gain, wq.astype(BF16), keys_t)
    return _peer_dense(xn, u_all, v_all, layer, pkf, pkb, h, out_gain=out_gain)


def _lane_tables(T, rot_dim, theta, *, rot_at, keep, scale):
    r = rot_dim // 2
    inv = 1.0 / (theta ** (jnp.arange(0, rot_dim, 2, dtype=F32) / rot_dim))
    lane = np.arange(LANES)
    in_rot = (lane >= rot_at) & (lane < rot_at + rot_dim)
    inv_lane = jnp.where(jnp.asarray(in_rot), inv[np.where(in_rot, (lane - rot_at) % r, 0)], 0.0)
    ang = jnp.arange(T, dtype=F32)[:, None] * inv_lane[None, :]
    c = jnp.cos(ang) * jnp.asarray((lane < keep) * scale, F32)[None, :]
    s = jnp.sin(ang) * jnp.asarray(in_rot * scale, F32)[None, :]
    return c, s


def _head_cols(n_heads, src_stride, src_off, width, *, dst_stride=LANES, dst_off=0):
    idx = np.zeros(n_heads * dst_stride, np.int32)
    sgn = np.zeros(n_heads * dst_stride, np.float32)
    for h in range(n_heads):
        d = h * dst_stride + dst_off
        idx[d:d + width] = h * src_stride + src_off + np.arange(width)
        sgn[d:d + width] = 1.0
    return idx, sgn


def _rot_cols(n_heads, src_stride, src_off, r, *, dst_stride=LANES, dst_off=0):
    idx = np.zeros(n_heads * dst_stride, np.int32)
    sgn = np.zeros(n_heads * dst_stride, np.float32)
    for h in range(n_heads):
        d = h * dst_stride + dst_off
        s = h * src_stride + src_off
        idx[d:d + r] = s + r + np.arange(r)
        sgn[d:d + r] = -1.0
        idx[d + r:d + 2 * r] = s + np.arange(r)
        sgn[d + r:d + 2 * r] = 1.0
    return idx, sgn


def _take_cols(w, idx_sgn):
    idx, sgn = idx_sgn
    return (jnp.take(w, jnp.asarray(idx), axis=1) * jnp.asarray(sgn)[None, :]).astype(BF16)


def _even_mixer(h, norm_g, w_in, ret_gn, w_o):
    T = h.shape[0]
    mw = MOBA_HEADS * MOBA_HEAD_DIM
    rw = RET_HEADS * RET_DK
    vw = RET_HEADS * RET_DV
    o_mq, o_mk, o_mv, o_rq, o_rk, o_rv, o_rg = np.cumsum([0, mw, mw, mw, rw, rw, vw])

    def heads128(off, heads, stride):
        return _take_cols(w_in[:, off:off + heads * stride], _head_cols(heads, stride, 0, stride))

    seg_w = MOBA_HEADS * LANES
    mtab = functools.partial(_lane_tables, T, MOBA_ROT, ROPE_THETA, rot_at=0, keep=MOBA_HEAD_DIM)
    rtab = functools.partial(_lane_tables, T, RET_DK, RET_THETA, rot_at=0, keep=RET_DK)
    cat = lambda a, b: jnp.concatenate([a, b], axis=1)
    (cq, sq), (ck, sk) = mtab(scale=MOBA_HEAD_DIM ** -0.5 * LOG2E), mtab(scale=1.0)
    qk_m, colmean = _proj(h, cat(heads128(o_mq, MOBA_HEADS, MOBA_HEAD_DIM),
                                 heads128(o_mk, MOBA_HEADS, MOBA_HEAD_DIM)),
                          gain=norm_g, rot=(0, MOBA_ROT // 2), cos=cat(cq, ck), sin=cat(sq, sk),
                          seg=seg_w, colmean=MOBA_BLOCK)
    (cq, sq), (ck, sk) = rtab(scale=1.0), rtab(scale=RET_DK ** -0.5)
    qk_r = _proj(h, cat(heads128(o_rq, RET_HEADS, RET_DK), heads128(o_rk, RET_HEADS, RET_DK)),
                 gain=norm_g, rot=(0, RET_DK // 2), cos=cat(cq, ck), sin=cat(sq, sk), seg=seg_w)
    w_plain = jnp.concatenate(
        [_take_cols(w_in[:, o_mv:o_rq], _head_cols(MOBA_HEADS, MOBA_HEAD_DIM, 0, MOBA_HEAD_DIM)),
         w_in[:, o_rv:].astype(BF16)], axis=1)
    plain = _proj(h, w_plain, gain=norm_g)
    nb = T // MOBA_BLOCK
    km = colmean[:, 0, seg_w:2 * seg_w].reshape(nb, MOBA_HEADS, LANES).transpose(1, 0, 2)
    km = jnp.pad(km, ((0, 0), (MOBA_HEAD_DIM, LANES - MOBA_HEAD_DIM - nb), (0, 0)))
    mq_b, mk_b, mv_b = _moba_gate(qk_m, plain, km, MOBA_HEADS)
    a_out = _flash(mq_b, mk_b, mv_b, MOBA_HEADS, l_lane=MOBA_HEAD_DIM)
    b_out = _retention(qk_r, plain, ret_gn, RET_HEADS, q_off=0, k_off=RET_HEADS,
                       v_off=MOBA_HEADS, g_off=MOBA_HEADS + RET_HEADS)
    return _matmul_res([a_out, b_out], [w_o[:mw].astype(BF16), w_o[mw:].astype(BF16)], h)


def _odd_mixer(h, norm_g, w_down, q_norm, w_uq, kv_norm, w_ukv, w_o):
    T = h.shape[0]
    lat = MLA_Q_RANK + MLA_KV_RANK
    half = MLA_ROPE // 2
    dq = MLA_NOPE + MLA_ROPE
    kw = MLA_HEADS * LANES
    w_dn = jnp.concatenate(
        [w_down[:, :lat].astype(BF16),
         _take_cols(w_down[:, lat:], _head_cols(1, MLA_ROPE, 0, MLA_ROPE)),
         _take_cols(w_down[:, lat:], _rot_cols(1, MLA_ROPE, 0, half))], axis=1)
    down = _proj(h, w_dn, gain=norm_g, out_dtype=F32)
    ck, sk = _lane_tables(T, MLA_ROPE, ROPE_THETA, rot_at=0, keep=MLA_ROPE, scale=1.0)
    cqn, ckvx = _mla_mid(down, q_norm, kv_norm, ck, sk)
    cq_t, sq_t = _lane_tables(T, MLA_ROPE, ROPE_THETA, rot_at=MLA_NOPE, keep=dq,
                              scale=dq ** -0.5 * LOG2E)
    q = _proj(cqn, _take_cols(w_uq, _head_cols(MLA_HEADS, dq, 0, dq)),
              rot=(MLA_NOPE, half), cos=cq_t, sin=sq_t, seg=kw)
    kvw = MLA_NOPE + MLA_V
    place_k = np.zeros((LANES, kw), np.float32)
    place_v = np.zeros((LANES, kw), np.float32)
    for hh in range(MLA_HEADS):
        place_k[np.arange(MLA_ROPE), hh * LANES + MLA_NOPE + np.arange(MLA_ROPE)] = 1.0
        place_v[MLA_ROPE, hh * LANES + MLA_V] = 1.0
    wk = jnp.concatenate([_take_cols(w_ukv, _head_cols(MLA_HEADS, kvw, 0, MLA_NOPE)),
                          jnp.asarray(place_k, BF16)], axis=0)
    wv = jnp.concatenate([_take_cols(w_ukv, _head_cols(MLA_HEADS, kvw, MLA_NOPE, MLA_V)),
                          jnp.asarray(place_v, BF16)], axis=0)
    kv = _proj(ckvx, jnp.concatenate([wk, wv], axis=1))
    o = _flash(q, kv, kv, MLA_HEADS, v_off=MLA_HEADS // 2, l_lane=MLA_V)
    return _matmul_res([o], [w_o.astype(BF16)], h)


def kernel(x, attn_norm, ffn_norm, ev_w_in, ev_ret_gn, ev_w_o, od_w_down, od_q_norm, od_w_uq,
           od_kv_norm, od_w_ukv, od_w_o, peer_wq, peer_keys, peer_u, peer_v, final_norm):
    B, S, D = x.shape
    assert B == 1
    h = x.reshape(S, D)
    depth = attn_norm.shape[0]
    u_all, v_all = peer_u.astype(BF16), peer_v.astype(BF16)
    for i in range(depth):
        j = i // 2
        if i % 2 == 0:
            h = _even_mixer(h, attn_norm[i], ev_w_in[j], ev_ret_gn[j], ev_w_o[j])
        else:
            h = _odd_mixer(h, attn_norm[i], od_w_down[j], od_q_norm[j], od_w_uq[j],
                           od_kv_norm[j], od_w_ukv[j], od_w_o[j])
        h = _peer_ffn(h, ffn_norm[i], peer_wq[i], peer_keys[i], u_all, v_all, i,
                      out_gain=final_norm if i == depth - 1 else None)
    return h.reshape(B, S, D)
```

```python
import functools

import numpy as np
import jax
import jax.numpy as jnp
from jax import lax
from jax.experimental import pallas as pl
from jax.experimental.pallas import tpu as pltpu

F32 = jnp.float32
BF16 = jnp.bfloat16

LANES = 128
VMEM_LIMIT = 56 * 1024 * 1024

D_MODEL = 1024
EPS = 1e-6
ROPE_THETA = 500000.0
LOG2E = float(np.log2(np.e))

MOBA_HEADS = 8
MOBA_HEAD_DIM = 64
MOBA_ROT = MOBA_HEAD_DIM // 4
MOBA_BLOCK = 256
MOBA_TOPK = 3
MASK_BIAS = -1e9

RET_HEADS = 8
RET_DK = 64
RET_DV = 128
RET_THETA = 10000.0
RET_CHUNK = 512

MLA_HEADS = 16
MLA_NOPE = 64
MLA_ROPE = 32
MLA_V = 64
MLA_Q_RANK = 512
MLA_KV_RANK = 256

PEER_HEADS = 8
PEER_NKEYS = 128
PEER_DKEY = 128
PEER_TOPK = 16


def _cparams(*sem):
    return pltpu.CompilerParams(dimension_semantics=sem, vmem_limit_bytes=VMEM_LIMIT)


def _proj_body(*refs, norm, rot, colmean, tn):
    it = iter(refs)
    x_ref = next(it)
    g_ref = next(it) if norm else None
    w_ref = next(it)
    if rot:
        c_ref, s_ref = next(it), next(it)
    o_ref = next(it)
    cm_ref = next(it) if colmean else None
    xn_ref = next(it)

    @pl.when(pl.program_id(1) == 0)
    def _():
        x = x_ref[...].astype(F32)
        if norm:
            x = x * lax.rsqrt(jnp.mean(x * x, axis=-1, keepdims=True) + EPS) * g_ref[...]
        xn_ref[...] = x.astype(BF16)

    xn = xn_ref[...]
    y = jnp.dot(xn, w_ref[...], preferred_element_type=F32)
    if not rot:
        o_ref[...] = y.astype(o_ref.dtype)
        return
    rot_at, r = rot
    c = c_ref[...]
    s = s_ref[...]
    lane = lax.broadcasted_iota(jnp.int32, c.shape, 1)
    first = ((lane - rot_at) & (2 * r - 1)) < r
    for k in range(tn // LANES):
        sl = slice(k * LANES, (k + 1) * LANES)
        yk = y[:, sl]
        half = jnp.where(first, -pltpu.roll(yk, LANES - r, axis=1), pltpu.roll(yk, r, axis=1))
        val = yk * c + half * s
        o_ref[:, sl] = val.astype(o_ref.dtype)
        if colmean:
            for b in range(val.shape[0] // colmean):
                cm_ref[b, :, sl] = jnp.mean(val[b * colmean:(b + 1) * colmean], axis=0,
                                            keepdims=True)


def _proj(x, w, *, gain=None, rot=None, cos=None, sin=None, seg=None,
          colmean=None, out_dtype=BF16, tm=1024, tn=1024):
    T, K = x.shape
    N = w.shape[1]
    tm, tn = min(tm, T), min(tn, N)
    assert T % tm == 0 and N % tn == 0 and tn % LANES == 0
    norm = gain is not None
    in_specs = [pl.BlockSpec((tm, K), lambda i, j: (i, 0))]
    args = [x]
    if norm:
        in_specs.append(pl.BlockSpec((1, K), lambda i, j: (0, 0)))
        args.append(gain.reshape(1, K).astype(F32))
    in_specs.append(pl.BlockSpec((K, tn), lambda i, j: (0, j)))
    args.append(w)
    if rot:
        assert seg % tn == 0 and rot[0] % (2 * rot[1]) == 0
        tab = lambda i, j: (i, (j * tn) // seg)
        in_specs += [pl.BlockSpec((tm, LANES), tab), pl.BlockSpec((tm, LANES), tab)]
        args += [cos, sin]
    out_shape = [jax.ShapeDtypeStruct((T, N), out_dtype)]
    out_specs = [pl.BlockSpec((tm, tn), lambda i, j: (i, j))]
    if colmean:
        assert rot and tm % colmean == 0
        out_shape.append(jax.ShapeDtypeStruct((T // colmean, 1, N), F32))
        out_specs.append(pl.BlockSpec((tm // colmean, 1, tn), lambda i, j: (i, 0, j)))
    res = pl.pallas_call(
        functools.partial(_proj_body, norm=norm, rot=rot, colmean=colmean, tn=tn),
        grid=(T // tm, N // tn),
        in_specs=in_specs, out_specs=out_specs, out_shape=out_shape,
        scratch_shapes=[pltpu.VMEM((tm, K), BF16)],
        compiler_params=_cparams("parallel", "arbitrary"),
        name="proj",
    )(*args)
    return res if colmean else res[0]


def _matmul_res_body(*refs, n_in):
    xs, ws = refs[:n_in], refs[n_in:2 * n_in]
    r_ref, o_ref = refs[2 * n_in], refs[2 * n_in + 1]
    acc = r_ref[...]
    for x_ref, w_ref in zip(xs, ws):
        acc = acc + jnp.dot(x_ref[...], w_ref[...], preferred_element_type=F32)
    o_ref[...] = acc


def _matmul_res(xs, ws, res, *, tm=512, tn=1024):
    T, N = res.shape
    n_in = len(xs)
    in_specs = [pl.BlockSpec((tm, x.shape[1]), lambda i, j: (i, 0)) for x in xs]
    in_specs += [pl.BlockSpec((w.shape[0], tn), lambda i, j: (0, j)) for w in ws]
    in_specs.append(pl.BlockSpec((tm, tn), lambda i, j: (i, j)))
    return pl.pallas_call(
        functools.partial(_matmul_res_body, n_in=n_in),
        grid=(T // tm, N // tn),
        in_specs=in_specs,
        out_specs=pl.BlockSpec((tm, tn), lambda i, j: (i, j)),
        out_shape=jax.ShapeDtypeStruct((T, N), F32),
        compiler_params=_cparams("parallel", "parallel"),
        name="matmul_res",
    )(*xs, *ws, res)


def _moba_gate_body(q_ref, k_ref, v_ref, km_ref, qo_ref, ko_ref, vo_ref, *, tq, heads):
    blk = pl.program_id(0)
    lane = lax.broadcasted_iota(jnp.int32, (tq, LANES), 1)
    bidx = lane - MOBA_HEAD_DIM
    slot = lax.broadcasted_iota(jnp.int32, (LANES, tq), 0)
    sblk = slot - MOBA_HEAD_DIM
    for h in range(heads):
        sl = slice(h * LANES, (h + 1) * LANES)
        q = q_ref[:, sl].astype(F32)
        gate = lax.dot_general(km_ref[h], q, (((1,), (1,)), ((), ())),
                               precision=lax.Precision.HIGHEST, preferred_element_type=F32)
        g = jnp.where((sblk >= 0) & (sblk < blk), gate, -jnp.inf)
        sel = sblk == blk
        for _ in range(MOBA_TOPK):
            m = jnp.max(g, axis=0, keepdims=True)
            idx = jnp.min(jnp.where(g == m, slot, 2 * LANES), axis=0, keepdims=True)
            pick = (slot == idx) & (m > -jnp.inf)
            sel = sel | pick
            g = jnp.where(pick, -jnp.inf, g)
        bias = jnp.where(sel, 0.0, MASK_BIAS).T
        qo_ref[:, sl] = jnp.where(bidx < 0, q, bias).astype(BF16)
        ko_ref[:, sl] = jnp.where(bidx == blk, 1.0, k_ref[:, sl].astype(F32)).astype(BF16)
        vo_ref[:, sl] = jnp.where(bidx == 0, 1.0, v_ref[:, sl].astype(F32)).astype(BF16)


def _moba_gate(qk, plain, kmean_pad, heads):
    T = qk.shape[0]
    W = heads * LANES
    tq = MOBA_BLOCK
    assert T % tq == 0 and T // tq <= LANES - MOBA_HEAD_DIM
    col = lambda c: pl.BlockSpec((tq, W), lambda i: (i, c))
    return pl.pallas_call(
        functools.partial(_moba_gate_body, tq=tq, heads=heads),
        grid=(T // tq,),
        in_specs=[col(0), col(1), col(0),
                  pl.BlockSpec((heads, LANES, LANES), lambda i: (0, 0, 0))],
        out_specs=[col(0)] * 3,
        out_shape=[jax.ShapeDtypeStruct((T, W), BF16)] * 3,
        compiler_params=_cparams("parallel"),
        name="moba_gate",
    )(qk, qk, plain, kmean_pad)


def _flash_body(q_ref, k_ref, v_ref, o_ref, m_ref, acc_ref, *, tq, l_lane):
    i = pl.program_id(1)
    m_ref[...] = jnp.full(m_ref.shape, -jnp.inf, F32)
    acc_ref[...] = jnp.zeros(acc_ref.shape, F32)

    def step(r0, nr, c0, nc, masked):
        for hh in range(2):
            hsl = slice(hh * LANES, (hh + 1) * LANES)
            s = lax.dot_general(q_ref[r0:r0 + nr, hsl], k_ref[pl.ds(c0, nc), hsl],
                                (((1,), (1,)), ((), ())), preferred_element_type=F32)
            if masked:
                row = i * tq + r0 + lax.broadcasted_iota(jnp.int32, (nr, nc), 0)
                col = c0 + lax.broadcasted_iota(jnp.int32, (nr, nc), 1)
                s = jnp.where(col <= row, s, -jnp.inf)
            m_prev = m_ref[hh, r0:r0 + nr]
            m_new = jnp.maximum(m_prev, jnp.max(s, axis=-1, keepdims=True))
            p = jnp.exp2(s - pltpu.repeat(m_new, nc // LANES, axis=1))
            acc_ref[hh, r0:r0 + nr] = jnp.exp2(m_prev - m_new) * acc_ref[hh, r0:r0 + nr] + jnp.dot(
                p.astype(BF16), v_ref[pl.ds(c0, nc), hsl], preferred_element_type=F32)
            m_ref[hh, r0:r0 + nr] = m_new

    def loop_body(j, carry):
        step(0, tq, pl.multiple_of(j * tq, tq), tq, False)
        return carry

    lax.fori_loop(0, i, loop_body, 0)
    half = tq // 2
    d0 = pl.multiple_of(i * tq, tq)
    step(0, half, d0, half, True)
    step(half, half, d0, half, False)
    step(half, half, pl.multiple_of(d0 + half, half), half, True)
    outs = [acc_ref[hh] / acc_ref[hh][:, l_lane:l_lane + 1] for hh in range(2)]
    lane = lax.broadcasted_iota(jnp.int32, (tq, LANES), 1)
    o_ref[...] = jnp.where(lane < l_lane, outs[0],
                           pltpu.roll(outs[1], l_lane, axis=1)).astype(o_ref.dtype)


def _flash(q, k, v, heads, *, q_off=0, k_off=0, v_off=0, l_lane, tq=1024):
    T = q.shape[0]
    tq = min(tq, T)
    assert heads % 2 == 0 and T % tq == 0 and tq % (2 * LANES) == 0 and 2 * l_lane == LANES
    pair = 2 * LANES
    return pl.pallas_call(
        functools.partial(_flash_body, tq=tq, l_lane=l_lane),
        grid=(heads // 2, T // tq),
        in_specs=[pl.BlockSpec((tq, pair), lambda h, i: (i, q_off + h)),
                  pl.BlockSpec((T, pair), lambda h, i: (0, k_off + h)),
                  pl.BlockSpec((T, pair), lambda h, i: (0, v_off + h))],
        out_specs=pl.BlockSpec((tq, LANES), lambda h, i: (i, h)),
        out_shape=jax.ShapeDtypeStruct((T, heads * l_lane), BF16),
        scratch_shapes=[pltpu.VMEM((2, tq, LANES), F32)] * 2,
        compiler_params=_cparams("parallel", "arbitrary"),
        name="flash",
    )(q, k, v)


def _retention_body(q_ref, k_ref, v_ref, g_ref, gn_ref, dm_ref, xi_ref, ze_ref, gc_ref,
                    o_ref, r_ref):
    @pl.when(pl.program_id(1) == 0)
    def _():
        r_ref[...] = jnp.zeros_like(r_ref)

    q = q_ref[...]
    k = k_ref[...]
    v = v_ref[...]
    r_old = r_ref[...]
    inner = lax.dot_general(q, k, (((1,), (1,)), ((), ())), preferred_element_type=F32) * dm_ref[0]
    out = jnp.dot(inner.astype(BF16), v, preferred_element_type=F32)
    out = out + jnp.dot(q, r_old.astype(BF16), preferred_element_type=F32) * xi_ref[0]
    kz = (k.astype(F32) * ze_ref[0]).T.astype(BF16)
    r_ref[...] = r_old * gc_ref[0] + jnp.dot(kz, v, preferred_element_type=F32)
    mu = jnp.mean(out, axis=-1, keepdims=True)
    cen = out - mu
    var = jnp.mean(cen * cen, axis=-1, keepdims=True)
    rn = cen * lax.rsqrt(var + EPS) * gn_ref[...]
    gate = g_ref[...].astype(F32)
    o_ref[...] = (rn * (gate / (1.0 + jnp.exp(-gate)))).astype(o_ref.dtype)


def _retention(qk, plain, gn, heads, *, q_off, k_off, v_off, g_off):
    T = qk.shape[0]
    W = heads * LANES
    C = min(RET_CHUNK, T)
    assert T % C == 0
    log_g = jnp.log(1.0 - 2.0 ** (-5.0 - jnp.arange(heads, dtype=F32)))
    pos = jnp.arange(C, dtype=F32)
    diff = pos[:, None] - pos[None, :]
    dmat = jnp.where(diff >= 0, jnp.exp(log_g[:, None, None] * jnp.maximum(diff, 0.0)), 0.0)
    rep = lambda t: jnp.broadcast_to(t[..., None], t.shape + (LANES,))
    xi = rep(jnp.exp(log_g[:, None] * (pos + 1.0)))
    zeta = rep(jnp.exp(log_g[:, None] * (C - 1.0 - pos)))
    g_chunk = rep(jnp.exp(log_g * C)[:, None])
    tile = lambda off: pl.BlockSpec((C, LANES), lambda h, c: (c, off + h))
    head_tab = lambda r: pl.BlockSpec((1, r, LANES), lambda h, c: (h, 0, 0))
    return pl.pallas_call(
        _retention_body,
        grid=(heads, T // C),
        in_specs=[tile(q_off), tile(k_off), tile(v_off), tile(g_off),
                  pl.BlockSpec((1, LANES), lambda h, c: (0, h)),
                  pl.BlockSpec((1, C, C), lambda h, c: (h, 0, 0)),
                  head_tab(C), head_tab(C), head_tab(1)],
        out_specs=tile(0),
        out_shape=jax.ShapeDtypeStruct((T, W), BF16),
        scratch_shapes=[pltpu.VMEM((LANES, LANES), F32)],
        compiler_params=_cparams("parallel", "arbitrary"),
        name="retention",
    )(qk, qk, plain, plain, gn.reshape(1, W).astype(F32), dmat, xi, zeta, g_chunk)


def _mla_mid_body(d_ref, qn_ref, kvn_ref, c_ref, s_ref, cq_ref, ckv_ref):
    def rms(x, g):
        return x * lax.rsqrt(jnp.mean(x * x, axis=-1, keepdims=True) + EPS) * g

    cq_ref[...] = rms(d_ref[:, :MLA_Q_RANK], qn_ref[...]).astype(BF16)
    lo = MLA_Q_RANK + MLA_KV_RANK
    ckv_ref[:, :MLA_KV_RANK] = rms(d_ref[:, MLA_Q_RANK:lo], kvn_ref[...]).astype(BF16)
    kr = d_ref[:, lo:lo + LANES] * c_ref[...] + d_ref[:, lo + LANES:lo + 2 * LANES] * s_ref[...]
    lane = lax.broadcasted_iota(jnp.int32, kr.shape, 1)
    ckv_ref[:, MLA_KV_RANK:] = jnp.where(lane == MLA_ROPE, 1.0, kr).astype(BF16)


def _mla_mid(down, q_norm, kv_norm, cos, sin, *, tm=512):
    T, W = down.shape
    wide = MLA_KV_RANK + LANES
    return pl.pallas_call(
        _mla_mid_body,
        grid=(T // tm,),
        in_specs=[pl.BlockSpec((tm, W), lambda i: (i, 0)),
                  pl.BlockSpec((1, MLA_Q_RANK), lambda i: (0, 0)),
                  pl.BlockSpec((1, MLA_KV_RANK), lambda i: (0, 0)),
                  pl.BlockSpec((tm, LANES), lambda i: (i, 0)),
                  pl.BlockSpec((tm, LANES), lambda i: (i, 0))],
        out_specs=[pl.BlockSpec((tm, MLA_Q_RANK), lambda i: (i, 0)),
                   pl.BlockSpec((tm, wide), lambda i: (i, 0))],
        out_shape=[jax.ShapeDtypeStruct((T, MLA_Q_RANK), BF16),
                   jax.ShapeDtypeStruct((T, wide), BF16)],
        compiler_params=_cparams("parallel"),
        name="mla_mid",
    )(down, q_norm.reshape(1, -1).astype(F32), kv_norm.reshape(1, -1).astype(F32), cos, sin)


PEER_CAND = [(i, PEER_TOPK // (i + 1)) for i in range(PEER_TOPK)]
PEER_NCAND = 64
assert sum(c for _, c in PEER_CAND) <= PEER_NCAND
PEER_NORANK = 64.0


def _oddeven_merge_sort(n):
    pairs, p = [], 1
    while p < n:
        k = p
        while k >= 1:
            for j in range(k % p, n - k, 2 * k):
                for i in range(min(k, n - j - k)):
                    if (i + j) // (2 * p) == (i + j + k) // (2 * p):
                        pairs.append((i + j, i + j + k))
            k //= 2
        p *= 2
    return pairs


_SORT16 = _oddeven_merge_sort(PEER_NKEYS // 8)
_SORT_CAND = _oddeven_merge_sort(PEER_NCAND // 8)


def _walk_best(x, network, count, emit):
    rows = [x[8 * k:8 * k + 8, :] for k in range(x.shape[0] // 8)]
    depth = len(rows)
    rows.append(jnp.full(rows[0].shape, -jnp.inf, F32))
    for i, j in network:
        rows[i], rows[j] = jnp.maximum(rows[i], rows[j]), jnp.minimum(rows[i], rows[j])
    for r in range(count):
        m = jnp.max(rows[0], axis=0, keepdims=True)
        emit(r, m)
        hit = rows[0] == m
        for i in range(min(depth, count - 1 - r)):
            rows[i] = jnp.where(hit, rows[i + 1], rows[i])


def _peer_topk_body(x_ref, g_ref, wq_ref, kt_ref, xn_ref, pkf_ref, pkb_ref,
                    st_ref, t1_ref, t2_ref, cand_ref):
    x = x_ref[...]
    xn = (x * lax.rsqrt(jnp.mean(x * x, axis=-1, keepdims=True) + EPS) * g_ref[...]).astype(BF16)
    xn_ref[...] = xn
    qry = jnp.dot(xn, wq_ref[...], preferred_element_type=F32).astype(BF16)
    st_ref[...] = lax.dot_general(kt_ref[...], qry, (((1,), (1,)), ((), ())),
                                  preferred_element_type=F32)
    n = PEER_NKEYS

    def store_rows(t_ref):
        def emit(r, m):
            t_ref[r:r + 1, :] = m
        return emit

    for h in range(PEER_HEADS):
        s1 = st_ref[(2 * h) * n:(2 * h + 1) * n, :]
        s2 = st_ref[(2 * h + 1) * n:(2 * h + 2) * n, :]
        _walk_best(s1, _SORT16, PEER_TOPK, store_rows(t1_ref))
        _walk_best(s2, _SORT16, PEER_TOPK, store_rows(t2_ref))
        rank2 = jnp.full(s2.shape, PEER_NORANK, F32)
        for r in range(PEER_TOPK):
            rank2 = jnp.where(s2 == t2_ref[r:r + 1, :], float(r), rank2)
        cand_ref[...] = jnp.full(cand_ref.shape, -jnp.inf, F32)
        rowp = 0
        for i, cnt in PEER_CAND:
            cand_ref[rowp:rowp + cnt, :] = t1_ref[i:i + 1, :] + t2_ref[0:cnt, :]
            rowp += cnt
        top1 = t1_ref[0:1, :]
        top2 = t2_ref[0:1, :]
        cmax = top1 + top2
        z = jnp.zeros_like(cmax)
        best = []
        _walk_best(cand_ref[...], _SORT_CAND, PEER_TOPK, lambda r, m: best.append(m))
        for m in best:
            z = z + jnp.exp(m - cmax)
        kth = best[-1]
        cnt = jnp.zeros(s1.shape, F32)
        for j in range(PEER_TOPK // 2):
            cnt = cnt + jnp.where(s1 + t2_ref[j:j + 1, :] >= kth, 1.0, 0.0)
        cnt_best = jnp.zeros_like(top1)
        for j in range(PEER_TOPK):
            cnt_best = cnt_best + jnp.where(top1 + t2_ref[j:j + 1, :] >= kth, 1.0, 0.0)
        cnt = jnp.where(s1 == top1, cnt_best, cnt)
        pkf_ref[h, 0] = jnp.exp(s1 - top1) / z
        pkf_ref[h, 1] = cnt
        pkb_ref[h, 0] = rank2.astype(BF16)
        pkb_ref[h, 1] = jnp.exp(s2 - top2).astype(BF16)


def _peer_topk(h, gain, wq, keys_t, *, tm=256):
    T, D = h.shape
    R = keys_t.shape[0]
    tm = min(tm, T)
    blk = lambda i: (0, 0, 0, i)
    shape = (PEER_HEADS, 2, PEER_NKEYS, T)
    return pl.pallas_call(
        _peer_topk_body,
        grid=(T // tm,),
        in_specs=[pl.BlockSpec((tm, D), lambda i: (i, 0)),
                  pl.BlockSpec((1, D), lambda i: (0, 0)),
                  pl.BlockSpec(wq.shape, lambda i: (0, 0)),
                  pl.BlockSpec(keys_t.shape, lambda i: (0, 0))],
        out_specs=[pl.BlockSpec((tm, D), lambda i: (i, 0))]
        + [pl.BlockSpec((PEER_HEADS, 2, PEER_NKEYS, tm), blk)] * 2,
        out_shape=[jax.ShapeDtypeStruct((T, D), BF16),
                   jax.ShapeDtypeStruct(shape, F32), jax.ShapeDtypeStruct(shape, BF16)],
        scratch_shapes=[pltpu.VMEM((R, tm), F32),
                        pltpu.VMEM((PEER_TOPK + 8, tm), F32),
                        pltpu.VMEM((PEER_TOPK + 8, tm), F32),
                        pltpu.VMEM((PEER_NCAND, tm), F32)],
        compiler_params=_cparams("parallel"),
        name="peer_topk",
    )(h, gain.reshape(1, D).astype(F32), wq, keys_t)


def _peer_dense_body(xn_ref, u_ref, v_ref, pkf_ref, pkb_ref, h_ref, *rest,
                     tm, te, n_e, n_steps, out_norm):
    og_ref = rest[0] if out_norm else None
    o_ref, ht0_ref, ht1_ref, acc_ref = rest[-4:]
    s = pl.program_id(0)
    n = PEER_NKEYS
    group = 2

    def scores(dst_ref):
        dst_ref[...] = lax.dot_general(u_ref[...], xn_ref[...], (((1,), (1,)), ((), ())),
                                       preferred_element_type=F32)

    def scores_half(dst_ref, half, anchor):
        hw = tm // 2
        bits = pltpu.bitcast(anchor[0:16, 0:LANES], jnp.uint32)
        zero = ((bits >> 16) >> 16)[0, 0].astype(jnp.int32)
        xs = xn_ref[pl.ds(pl.multiple_of(half * hw + zero * hw, hw), hw), :]
        dst_ref[:, half * hw:(half + 1) * hw] = lax.dot_general(
            u_ref[...], xs, (((1,), (1,)), ((), ())), preferred_element_type=F32)

    def experts(src_ref, dst_ref=None):
        e = lax.rem(s - 1, n_e)
        n_groups = te // (group * n)
        for gb in range(n_groups):
            if dst_ref is not None and gb in (1, n_groups // 2 + 1):
                scores_half(dst_ref, int(gb > 1), acts[0])
            acts = []
            for ab in range(gb * group, (gb + 1) * group):
                a = e * (te // n) + ab
                gsum = jnp.zeros((n, tm), BF16)
                for h in range(PEER_HEADS):
                    w1 = jnp.broadcast_to(pkf_ref[h, 0, pl.ds(a, 1), :], (n, tm)).astype(BF16)
                    cnt = jnp.broadcast_to(pkf_ref[h, 1, pl.ds(a, 1), :], (n, tm)).astype(BF16)
                    gsum = gsum + jnp.where(pkb_ref[h, 0] < cnt, pkb_ref[h, 1], 0.0) * w1
                hs = src_ref[ab * n:(ab + 1) * n, :]
                act = 0.5 * hs * (1.0 + lax.erf(hs * np.float32(1.0 / np.sqrt(2.0))))
                acts.append(act.astype(BF16) * gsum)
            rows = slice(gb * group * n, (gb + 1) * group * n)
            acc_ref[...] += lax.dot_general(v_ref[rows, :], jnp.concatenate(acts, axis=0),
                                            (((0,), (0,)), ((), ())),
                                            preferred_element_type=F32)

    even = lax.rem(s, 2) == 0
    steady = (s > 0) & (s < n_steps)

    @pl.when((s >= 2) & (lax.rem(s - 1, n_e) == 0))
    def _():
        y = h_ref[...] + acc_ref[...].T
        if out_norm:
            y = y * lax.rsqrt(jnp.mean(y * y, axis=-1, keepdims=True) + EPS) * og_ref[...]
        o_ref[...] = y
        acc_ref[...] = jnp.zeros_like(acc_ref)

    @pl.when(s == 0)
    def _():
        acc_ref[...] = jnp.zeros_like(acc_ref)
        scores(ht0_ref)

    @pl.when(steady & even)
    def _():
        experts(ht1_ref, ht0_ref)

    @pl.when(steady & jnp.logical_not(even))
    def _():
        experts(ht0_ref, ht1_ref)

    @pl.when(s == n_steps)
    def _():
        experts(ht1_ref if n_steps % 2 == 0 else ht0_ref)


def _peer_dense(xn, u, v, layer, pkf, pkb, h, *, out_gain=None, tm=512, te=1024):
    T, D = xn.shape
    E = u.shape[1]
    tm = min(tm, T)
    assert T % tm == 0 and E % te == 0 and te % (2 * PEER_NKEYS) == 0
    n_e = E // te
    n_steps = (T // tm) * n_e
    pair = lambda s, lag: jnp.clip(s - lag, 0, n_steps - 1)
    cur = lambda s: pair(s, 0)
    prev = lambda s: pair(s, 1)
    done = lambda s: pair(s, 2)
    pk_spec = pl.BlockSpec((PEER_HEADS, 2, PEER_NKEYS, tm), lambda s: (0, 0, 0, prev(s) // n_e))
    return pl.pallas_call(
        functools.partial(_peer_dense_body, tm=tm, te=te, n_e=n_e, n_steps=n_steps,
                          out_norm=out_gain is not None),
        grid=(n_steps + 2,),
        in_specs=[pl.BlockSpec((tm, D), lambda s: (cur(s) // n_e, 0)),
                  pl.BlockSpec((None, te, D), lambda s: (layer, cur(s) % n_e, 0)),
                  pl.BlockSpec((None, te, D), lambda s: (layer, prev(s) % n_e, 0)),
                  pk_spec, pk_spec,
                  pl.BlockSpec((tm, D), lambda s: (done(s) // n_e, 0))]
        + ([pl.BlockSpec((1, D), lambda s: (0, 0))] if out_gain is not None else []),
        out_specs=pl.BlockSpec((tm, D), lambda s: (done(s) // n_e, 0)),
        out_shape=jax.ShapeDtypeStruct((T, D), F32),
        scratch_shapes=[pltpu.VMEM((te, tm), F32), pltpu.VMEM((te, tm), F32),
                        pltpu.VMEM((D, tm), F32)],
        compiler_params=_cparams("arbitrary"),
        name="peer_dense",
    )(xn, u, v, pkf, pkb, h,
      *([out_gain.reshape(1, D).astype(F32)] if out_gain is not None else []))


def _peer_ffn(h, gain, wq, keys, u_all, v_all, layer, out_gain=None):
    nk, dh = PEER_NKEYS, PEER_DKEY // 2
    groups = PEER_HEADS * 2
    keys_t = jnp.einsum("gnd,gk->gnkd", keys.reshape(groups, nk, dh).astype(F32),
                        jnp.eye(groups, dtype=F32)).reshape(groups * nk, groups * dh).astype(BF16)
    xn, pkf, pkb = _peer_topk(h, gain, wq.astype(BF16), keys_t)
    return _peer_dense(xn, u_all, v_all, layer, pkf, pkb, h, out_gain=out_gain)


def _lane_tables(T, rot_dim, theta, *, rot_at, keep, scale):
    r = rot_dim // 2
    inv = 1.0 / (theta ** (jnp.arange(0, rot_dim, 2, dtype=F32) / rot_dim))
    lane = np.arange(LANES)
    in_rot = (lane >= rot_at) & (lane < rot_at + rot_dim)
    inv_lane = jnp.where(jnp.asarray(in_rot), inv[np.where(in_rot, (lane - rot_at) % r, 0)], 0.0)
    ang = jnp.arange(T, dtype=F32)[:, None] * inv_lane[None, :]
    c = jnp.cos(ang) * jnp.asarray((lane < keep) * scale, F32)[None, :]
    s = jnp.sin(ang) * jnp.asarray(in_rot * scale, F32)[None, :]
    return c, s


def _head_cols(n_heads, src_stride, src_off, width, *, dst_stride=LANES, dst_off=0):
    idx = np.zeros(n_heads * dst_stride, np.int32)
    sgn = np.zeros(n_heads * dst_stride, np.float32)
    for h in range(n_heads):
        d = h * dst_stride + dst_off
        idx[d:d + width] = h * src_stride + src_off + np.arange(width)
        sgn[d:d + width] = 1.0
    return idx, sgn


def _rot_cols(n_heads, src_stride, src_off, r, *, dst_stride=LANES, dst_off=0):
    idx = np.zeros(n_heads * dst_stride, np.int32)
    sgn = np.zeros(n_heads * dst_stride, np.float32)
    for h in range(n_heads):
        d = h * dst_stride + dst_off
        s = h * src_stride + src_off
        idx[d:d + r] = s + r + np.arange(r)
        sgn[d:d + r] = -1.0
        idx[d + r:d + 2 * r] = s + np.arange(r)
        sgn[d + r:d + 2 * r] = 1.0
    return idx, sgn


def _take_cols(w, idx_sgn):
    idx, sgn = idx_sgn
    return (jnp.take(w, jnp.asarray(idx), axis=1) * jnp.asarray(sgn)[None, :]).astype(BF16)


def _even_mixer(h, norm_g, w_in, ret_gn, w_o):
    T = h.shape[0]
    mw = MOBA_HEADS * MOBA_HEAD_DIM
    rw = RET_HEADS * RET_DK
    vw = RET_HEADS * RET_DV
    o_mq, o_mk, o_mv, o_rq, o_rk, o_rv, o_rg = np.cumsum([0, mw, mw, mw, rw, rw, vw])

    def heads128(off, heads, stride, scale=1.0):
        return _take_cols(w_in[:, off:off + heads * stride] * scale,
                          _head_cols(heads, stride, 0, stride))

    seg_w = MOBA_HEADS * LANES
    cat = lambda a, b: jnp.concatenate([a, b], axis=1)
    cos, sin = _lane_tables(T, MOBA_ROT, ROPE_THETA, rot_at=0, keep=MOBA_HEAD_DIM, scale=1.0)
    qk_m, colmean = _proj(
        h, cat(heads128(o_mq, MOBA_HEADS, MOBA_HEAD_DIM, MOBA_HEAD_DIM ** -0.5 * LOG2E),
               heads128(o_mk, MOBA_HEADS, MOBA_HEAD_DIM)),
        gain=norm_g, rot=(0, MOBA_ROT // 2), cos=cos, sin=sin, seg=2 * seg_w, colmean=MOBA_BLOCK)
    cos, sin = _lane_tables(T, RET_DK, RET_THETA, rot_at=0, keep=RET_DK, scale=1.0)
    qk_r = _proj(h, cat(heads128(o_rq, RET_HEADS, RET_DK),
                        heads128(o_rk, RET_HEADS, RET_DK, RET_DK ** -0.5)),
                 gain=norm_g, rot=(0, RET_DK // 2), cos=cos, sin=sin, seg=2 * seg_w)
    w_plain = jnp.concatenate(
        [_take_cols(w_in[:, o_mv:o_rq], _head_cols(MOBA_HEADS, MOBA_HEAD_DIM, 0, MOBA_HEAD_DIM)),
         w_in[:, o_rv:].astype(BF16)], axis=1)
    plain = _proj(h, w_plain, gain=norm_g)
    nb = T // MOBA_BLOCK
    km = colmean[:, 0, seg_w:2 * seg_w].reshape(nb, MOBA_HEADS, LANES).transpose(1, 0, 2)
    km = jnp.pad(km, ((0, 0), (MOBA_HEAD_DIM, LANES - MOBA_HEAD_DIM - nb), (0, 0)))
    mq_b, mk_b, mv_b = _moba_gate(qk_m, plain, km, MOBA_HEADS)
    a_out = _flash(mq_b, mk_b, mv_b, MOBA_HEADS, l_lane=MOBA_HEAD_DIM)
    b_out = _retention(qk_r, plain, ret_gn, RET_HEADS, q_off=0, k_off=RET_HEADS,
                       v_off=MOBA_HEADS, g_off=MOBA_HEADS + RET_HEADS)
    return _matmul_res([a_out, b_out], [w_o[:mw].astype(BF16), w_o[mw:].astype(BF16)], h)


def _odd_mixer(h, norm_g, w_down, q_norm, w_uq, kv_norm, w_ukv, w_o):
    T = h.shape[0]
    lat = MLA_Q_RANK + MLA_KV_RANK
    half = MLA_ROPE // 2
    dq = MLA_NOPE + MLA_ROPE
    kw = MLA_HEADS * LANES
    w_dn = jnp.concatenate(
        [w_down[:, :lat].astype(BF16),
         _take_cols(w_down[:, lat:], _head_cols(1, MLA_ROPE, 0, MLA_ROPE)),
         _take_cols(w_down[:, lat:], _rot_cols(1, MLA_ROPE, 0, half))], axis=1)
    down = _proj(h, w_dn, gain=norm_g, out_dtype=F32)
    ck, sk = _lane_tables(T, MLA_ROPE, ROPE_THETA, rot_at=0, keep=MLA_ROPE, scale=1.0)
    cqn, ckvx = _mla_mid(down, q_norm, kv_norm, ck, sk)
    cq_t, sq_t = _lane_tables(T, MLA_ROPE, ROPE_THETA, rot_at=MLA_NOPE, keep=dq,
                              scale=dq ** -0.5 * LOG2E)
    q = _proj(cqn, _take_cols(w_uq, _head_cols(MLA_HEADS, dq, 0, dq)),
              rot=(MLA_NOPE, half), cos=cq_t, sin=sq_t, seg=kw)
    kvw = MLA_NOPE + MLA_V
    place_k = np.zeros((LANES, kw), np.float32)
    place_v = np.zeros((LANES, kw), np.float32)
    for hh in range(MLA_HEADS):
        place_k[np.arange(MLA_ROPE), hh * LANES + MLA_NOPE + np.arange(MLA_ROPE)] = 1.0
        place_v[MLA_ROPE, hh * LANES + MLA_V] = 1.0
    wk = jnp.concatenate([_take_cols(w_ukv, _head_cols(MLA_HEADS, kvw, 0, MLA_NOPE)),
                          jnp.asarray(place_k, BF16)], axis=0)
    wv = jnp.concatenate([_take_cols(w_ukv, _head_cols(MLA_HEADS, kvw, MLA_NOPE, MLA_V)),
                          jnp.asarray(place_v, BF16)], axis=0)
    kv = _proj(ckvx, jnp.concatenate([wk, wv], axis=1))
    o = _flash(q, kv, kv, MLA_HEADS, v_off=MLA_HEADS // 2, l_lane=MLA_V)
    return _matmul_res([o], [w_o.astype(BF16)], h)


def kernel(x, attn_norm, ffn_norm, ev_w_in, ev_ret_gn, ev_w_o, od_w_down, od_q_norm, od_w_uq,
           od_kv_norm, od_w_ukv, od_w_o, peer_wq, peer_keys, peer_u, peer_v, final_norm):
    B, S, D = x.shape
    assert B == 1
    h = x.reshape(S, D)
    depth = attn_norm.shape[0]
    u_all, v_all = peer_u.astype(BF16), peer_v.astype(BF16)
    for i in range(depth):
        j = i // 2
        if i % 2 == 0:
            h = _even_mixer(h, attn_norm[i], ev_w_in[j], ev_ret_gn[j], ev_w_o[j])
        else:
            h = _odd_mixer(h, attn_norm[i], od_w_down[j], od_q_norm[j], od_w_uq[j],
                           od_kv_norm[j], od_w_ukv[j], od_w_o[j])
        h = _peer_ffn(h, ffn_norm[i], peer_wq[i], peer_keys[i], u_all, v_all, i,
                      out_gain=final_norm if i == depth - 1 else None)
    return h.reshape(B, S, D)
```

```python
import functools

import numpy as np
import jax
import jax.numpy as jnp
from jax import lax
from jax.experimental import pallas as pl
from jax.experimental.pallas import tpu as pltpu

F32 = jnp.float32
BF16 = jnp.bfloat16

LANES = 128
VMEM_LIMIT = 56 * 1024 * 1024

EPS = 1e-6
ROPE_THETA = 500000.0
LOG2E = float(np.log2(np.e))

MOBA_HEADS = 8
MOBA_HEAD_DIM = 64
MOBA_ROT = MOBA_HEAD_DIM // 4
MOBA_BLOCK = 256
MOBA_TOPK = 3
MASK_BIAS = -1e9

RET_HEADS = 8
RET_DK = 64
RET_DV = 128
RET_THETA = 10000.0
RET_CHUNK = 512

MLA_HEADS = 16
MLA_NOPE = 64
MLA_ROPE = 32
MLA_V = 64
MLA_Q_RANK = 512
MLA_KV_RANK = 256

PEER_HEADS = 8
PEER_NKEYS = 128
PEER_DKEY = 128
PEER_TOPK = 16


def _cparams(*sem):
    return pltpu.CompilerParams(dimension_semantics=sem, vmem_limit_bytes=VMEM_LIMIT)


def _proj_body(*refs, norm, rot, colmean, tn):
    it = iter(refs)
    x_ref = next(it)
    g_ref = next(it) if norm else None
    w_ref = next(it)
    if rot:
        c_ref, s_ref = next(it), next(it)
    o_ref = next(it)
    cm_ref = next(it) if colmean else None
    xn_ref = next(it)

    @pl.when(pl.program_id(1) == 0)
    def _():
        x = x_ref[...].astype(F32)
        if norm:
            x = x * lax.rsqrt(jnp.mean(x * x, axis=-1, keepdims=True) + EPS) * g_ref[...]
        xn_ref[...] = x.astype(BF16)

    xn = xn_ref[...]
    y = jnp.dot(xn, w_ref[...], preferred_element_type=F32)
    if not rot:
        o_ref[...] = y.astype(o_ref.dtype)
        return
    rot_at, r = rot
    c = c_ref[...]
    s = s_ref[...]
    lane = lax.broadcasted_iota(jnp.int32, c.shape, 1)
    first = ((lane - rot_at) & (2 * r - 1)) < r
    for k in range(tn // LANES):
        sl = slice(k * LANES, (k + 1) * LANES)
        yk = y[:, sl]
        half = jnp.where(first, -pltpu.roll(yk, LANES - r, axis=1), pltpu.roll(yk, r, axis=1))
        val = yk * c + half * s
        o_ref[:, sl] = val.astype(o_ref.dtype)
        if colmean:
            for b in range(val.shape[0] // colmean):
                cm_ref[b, :, sl] = jnp.mean(val[b * colmean:(b + 1) * colmean], axis=0,
                                            keepdims=True)


def _proj(x, w, *, gain=None, rot=None, cos=None, sin=None, seg=None,
          colmean=None, out_dtype=BF16, tm=1024, tn=1024):
    T, K = x.shape
    N = w.shape[1]
    tm, tn = min(tm, T), min(tn, N)
    assert T % tm == 0 and N % tn == 0 and tn % LANES == 0
    norm = gain is not None
    in_specs = [pl.BlockSpec((tm, K), lambda i, j: (i, 0))]
    args = [x]
    if norm:
        in_specs.append(pl.BlockSpec((1, K), lambda i, j: (0, 0)))
        args.append(gain.reshape(1, K).astype(F32))
    in_specs.append(pl.BlockSpec((K, tn), lambda i, j: (0, j)))
    args.append(w)
    if rot:
        assert seg % tn == 0 and rot[0] % (2 * rot[1]) == 0
        tab = lambda i, j: (i, (j * tn) // seg)
        in_specs += [pl.BlockSpec((tm, LANES), tab), pl.BlockSpec((tm, LANES), tab)]
        args += [cos, sin]
    out_shape = [jax.ShapeDtypeStruct((T, N), out_dtype)]
    out_specs = [pl.BlockSpec((tm, tn), lambda i, j: (i, j))]
    if colmean:
        assert rot and tm % colmean == 0
        out_shape.append(jax.ShapeDtypeStruct((T // colmean, 1, N), F32))
        out_specs.append(pl.BlockSpec((tm // colmean, 1, tn), lambda i, j: (i, 0, j)))
    res = pl.pallas_call(
        functools.partial(_proj_body, norm=norm, rot=rot, colmean=colmean, tn=tn),
        grid=(T // tm, N // tn),
        in_specs=in_specs, out_specs=out_specs, out_shape=out_shape,
        scratch_shapes=[pltpu.VMEM((tm, K), BF16)],
        compiler_params=_cparams("parallel", "arbitrary"),
        name="proj",
    )(*args)
    return res if colmean else res[0]


def _matmul_res_body(*refs, n_in):
    xs, ws = refs[:n_in], refs[n_in:2 * n_in]
    r_ref, o_ref = refs[2 * n_in], refs[2 * n_in + 1]
    acc = r_ref[...]
    for x_ref, w_ref in zip(xs, ws):
        acc = acc + jnp.dot(x_ref[...], w_ref[...], preferred_element_type=F32)
    o_ref[...] = acc


def _matmul_res(xs, ws, res, *, tm=512, tn=1024):
    T, N = res.shape
    n_in = len(xs)
    in_specs = [pl.BlockSpec((tm, x.shape[1]), lambda i, j: (i, 0)) for x in xs]
    in_specs += [pl.BlockSpec((w.shape[0], tn), lambda i, j: (0, j)) for w in ws]
    in_specs.append(pl.BlockSpec((tm, tn), lambda i, j: (i, j)))
    return pl.pallas_call(
        functools.partial(_matmul_res_body, n_in=n_in),
        grid=(T // tm, N // tn),
        in_specs=in_specs,
        out_specs=pl.BlockSpec((tm, tn), lambda i, j: (i, j)),
        out_shape=jax.ShapeDtypeStruct((T, N), F32),
        compiler_params=_cparams("parallel", "parallel"),
        name="matmul_res",
    )(*xs, *ws, res)


def _moba_gate_body(q_ref, k_ref, v_ref, km_ref, qo_ref, ko_ref, vo_ref, *, tq, heads):
    blk = pl.program_id(0)
    lane = lax.broadcasted_iota(jnp.int32, (tq, LANES), 1)
    bidx = lane - MOBA_HEAD_DIM
    slot = lax.broadcasted_iota(jnp.int32, (LANES, tq), 0)
    sblk = slot - MOBA_HEAD_DIM
    for h in range(heads):
        sl = slice(h * LANES, (h + 1) * LANES)
        q = q_ref[:, sl].astype(F32)
        gate = lax.dot_general(km_ref[h], q, (((1,), (1,)), ((), ())),
                               precision=lax.Precision.HIGHEST, preferred_element_type=F32)
        g = jnp.where((sblk >= 0) & (sblk < blk), gate, -jnp.inf)
        sel = sblk == blk
        for _ in range(MOBA_TOPK):
            m = jnp.max(g, axis=0, keepdims=True)
            idx = jnp.min(jnp.where(g == m, slot, 2 * LANES), axis=0, keepdims=True)
            pick = (slot == idx) & (m > -jnp.inf)
            sel = sel | pick
            g = jnp.where(pick, -jnp.inf, g)
        bias = jnp.where(sel, 0.0, MASK_BIAS).T
        qo_ref[:, sl] = jnp.where(bidx < 0, q, bias).astype(BF16)
        ko_ref[:, sl] = jnp.where(bidx == blk, 1.0, k_ref[:, sl].astype(F32)).astype(BF16)
        vo_ref[:, sl] = jnp.where(bidx == 0, 1.0, v_ref[:, sl].astype(F32)).astype(BF16)


def _moba_gate(qk, plain, kmean_pad, heads):
    T = qk.shape[0]
    W = heads * LANES
    tq = MOBA_BLOCK
    assert T % tq == 0 and T // tq <= LANES - MOBA_HEAD_DIM
    col = lambda c: pl.BlockSpec((tq, W), lambda i: (i, c))
    return pl.pallas_call(
        functools.partial(_moba_gate_body, tq=tq, heads=heads),
        grid=(T // tq,),
        in_specs=[col(0), col(1), col(0),
                  pl.BlockSpec((heads, LANES, LANES), lambda i: (0, 0, 0))],
        out_specs=[col(0)] * 3,
        out_shape=[jax.ShapeDtypeStruct((T, W), BF16)] * 3,
        compiler_params=_cparams("parallel"),
        name="moba_gate",
    )(qk, qk, plain, kmean_pad)


def _flash_body(q_ref, k_ref, v_ref, o_ref, m_ref, acc_ref, *, tq, l_lane):
    i = pl.program_id(1)
    m_ref[...] = jnp.full(m_ref.shape, -jnp.inf, F32)
    acc_ref[...] = jnp.zeros(acc_ref.shape, F32)

    def step(r0, nr, c0, nc, masked):
        for hh in range(2):
            hsl = slice(hh * LANES, (hh + 1) * LANES)
            s = lax.dot_general(q_ref[r0:r0 + nr, hsl], k_ref[pl.ds(c0, nc), hsl],
                                (((1,), (1,)), ((), ())), preferred_element_type=F32)
            if masked:
                row = i * tq + r0 + lax.broadcasted_iota(jnp.int32, (nr, nc), 0)
                col = c0 + lax.broadcasted_iota(jnp.int32, (nr, nc), 1)
                s = jnp.where(col <= row, s, -jnp.inf)
            m_prev = m_ref[hh, r0:r0 + nr]
            m_new = jnp.maximum(m_prev, jnp.max(s, axis=-1, keepdims=True))
            p = jnp.exp2(s - jnp.tile(m_new, (1, nc // LANES)))
            acc_ref[hh, r0:r0 + nr] = jnp.exp2(m_prev - m_new) * acc_ref[hh, r0:r0 + nr] + jnp.dot(
                p.astype(BF16), v_ref[pl.ds(c0, nc), hsl], preferred_element_type=F32)
            m_ref[hh, r0:r0 + nr] = m_new

    def loop_body(j, carry):
        step(0, tq, pl.multiple_of(j * tq, tq), tq, False)
        return carry

    lax.fori_loop(0, i, loop_body, 0)
    half = tq // 2
    d0 = pl.multiple_of(i * tq, tq)
    step(0, half, d0, half, True)
    step(half, half, d0, half, False)
    step(half, half, pl.multiple_of(d0 + half, half), half, True)
    outs = [acc_ref[hh] / acc_ref[hh][:, l_lane:l_lane + 1] for hh in range(2)]
    lane = lax.broadcasted_iota(jnp.int32, (tq, LANES), 1)
    o_ref[...] = jnp.where(lane < l_lane, outs[0],
                           pltpu.roll(outs[1], l_lane, axis=1)).astype(o_ref.dtype)


def _flash(q, k, v, heads, *, q_off=0, k_off=0, v_off=0, l_lane, tq=1024):
    T = q.shape[0]
    tq = min(tq, T)
    assert heads % 2 == 0 and T % tq == 0 and tq % (2 * LANES) == 0 and 2 * l_lane == LANES
    pair = 2 * LANES
    return pl.pallas_call(
        functools.partial(_flash_body, tq=tq, l_lane=l_lane),
        grid=(heads // 2, T // tq),
        in_specs=[pl.BlockSpec((tq, pair), lambda h, i: (i, q_off + h)),
                  pl.BlockSpec((T, pair), lambda h, i: (0, k_off + h)),
                  pl.BlockSpec((T, pair), lambda h, i: (0, v_off + h))],
        out_specs=pl.BlockSpec((tq, LANES), lambda h, i: (i, h)),
        out_shape=jax.ShapeDtypeStruct((T, heads * l_lane), BF16),
        scratch_shapes=[pltpu.VMEM((2, tq, LANES), F32)] * 2,
        compiler_params=_cparams("parallel", "arbitrary"),
        name="flash",
    )(q, k, v)


RET_GROUP = 2


def _retention_body(q_ref, k_ref, v_ref, g_ref, gn_ref, dm_ref, xi_ref, ze_ref, gc_ref,
                    o_ref, r_ref):
    @pl.when(pl.program_id(1) == 0)
    def _():
        r_ref[...] = jnp.zeros_like(r_ref)

    for hh in range(RET_GROUP):
        sl = slice(hh * LANES, (hh + 1) * LANES)
        q = q_ref[:, sl]
        k = k_ref[:, sl]
        v = v_ref[:, sl]
        r_old = r_ref[hh]
        inner = lax.dot_general(q, k, (((1,), (1,)), ((), ())),
                                preferred_element_type=F32) * dm_ref[hh]
        out = jnp.dot(inner.astype(BF16), v, preferred_element_type=F32)
        out = out + jnp.dot(q, r_old.astype(BF16), preferred_element_type=F32) * xi_ref[hh]
        kz = (k.astype(F32) * ze_ref[hh]).T.astype(BF16)
        r_ref[hh] = r_old * gc_ref[hh] + jnp.dot(kz, v, preferred_element_type=F32)
        mu = jnp.mean(out, axis=-1, keepdims=True)
        cen = out - mu
        var = jnp.mean(cen * cen, axis=-1, keepdims=True)
        rn = cen * lax.rsqrt(var + EPS) * gn_ref[:, sl]
        gate = g_ref[:, sl].astype(F32)
        o_ref[:, sl] = (rn * (gate / (1.0 + jnp.exp(-gate)))).astype(o_ref.dtype)


def _retention(qk, plain, gn, heads, *, q_off, k_off, v_off, g_off):
    T = qk.shape[0]
    W = heads * LANES
    C = min(RET_CHUNK, T)
    G = RET_GROUP
    assert T % C == 0 and heads % G == 0 and all(o % G == 0 for o in (q_off, k_off, v_off, g_off))
    log_g = jnp.log(1.0 - 2.0 ** (-5.0 - jnp.arange(heads, dtype=F32)))
    pos = jnp.arange(C, dtype=F32)
    diff = pos[:, None] - pos[None, :]
    dmat = jnp.where(diff >= 0, jnp.exp(log_g[:, None, None] * jnp.maximum(diff, 0.0)), 0.0)
    rep = lambda t: jnp.broadcast_to(t[..., None], t.shape + (LANES,))
    xi = rep(jnp.exp(log_g[:, None] * (pos + 1.0)))
    zeta = rep(jnp.exp(log_g[:, None] * (C - 1.0 - pos)))
    g_chunk = rep(jnp.exp(log_g * C)[:, None])
    tile = lambda off: pl.BlockSpec((C, G * LANES), lambda h, c: (c, off // G + h))
    head_tab = lambda r, w: pl.BlockSpec((G, r, w), lambda h, c: (h, 0, 0))
    return pl.pallas_call(
        _retention_body,
        grid=(heads // G, T // C),
        in_specs=[tile(q_off), tile(k_off), tile(v_off), tile(g_off),
                  pl.BlockSpec((1, G * LANES), lambda h, c: (0, h)),
                  head_tab(C, C), head_tab(C, LANES), head_tab(C, LANES), head_tab(1, LANES)],
        out_specs=tile(0),
        out_shape=jax.ShapeDtypeStruct((T, W), BF16),
        scratch_shapes=[pltpu.VMEM((G, LANES, LANES), F32)],
        compiler_params=_cparams("parallel", "arbitrary"),
        name="retention",
    )(qk, qk, plain, plain, gn.reshape(1, W).astype(F32), dmat, xi, zeta, g_chunk)


def _mla_mid_body(d_ref, qn_ref, kvn_ref, c_ref, s_ref, cq_ref, ckv_ref):
    def rms(x, g):
        return x * lax.rsqrt(jnp.mean(x * x, axis=-1, keepdims=True) + EPS) * g

    cq_ref[...] = rms(d_ref[:, :MLA_Q_RANK], qn_ref[...]).astype(BF16)
    lo = MLA_Q_RANK + MLA_KV_RANK
    ckv_ref[:, :MLA_KV_RANK] = rms(d_ref[:, MLA_Q_RANK:lo], kvn_ref[...]).astype(BF16)
    kr = d_ref[:, lo:lo + LANES] * c_ref[...] + d_ref[:, lo + LANES:lo + 2 * LANES] * s_ref[...]
    lane = lax.broadcasted_iota(jnp.int32, kr.shape, 1)
    ckv_ref[:, MLA_KV_RANK:] = jnp.where(lane == MLA_ROPE, 1.0, kr).astype(BF16)


def _mla_mid(down, q_norm, kv_norm, cos, sin, *, tm=512):
    T, W = down.shape
    wide = MLA_KV_RANK + LANES
    return pl.pallas_call(
        _mla_mid_body,
        grid=(T // tm,),
        in_specs=[pl.BlockSpec((tm, W), lambda i: (i, 0)),
                  pl.BlockSpec((1, MLA_Q_RANK), lambda i: (0, 0)),
                  pl.BlockSpec((1, MLA_KV_RANK), lambda i: (0, 0)),
                  pl.BlockSpec((tm, LANES), lambda i: (i, 0)),
                  pl.BlockSpec((tm, LANES), lambda i: (i, 0))],
        out_specs=[pl.BlockSpec((tm, MLA_Q_RANK), lambda i: (i, 0)),
                   pl.BlockSpec((tm, wide), lambda i: (i, 0))],
        out_shape=[jax.ShapeDtypeStruct((T, MLA_Q_RANK), BF16),
                   jax.ShapeDtypeStruct((T, wide), BF16)],
        compiler_params=_cparams("parallel"),
        name="mla_mid",
    )(down, q_norm.reshape(1, -1).astype(F32), kv_norm.reshape(1, -1).astype(F32), cos, sin)


PEER_CAND = [(i, PEER_TOPK // (i + 1)) for i in range(PEER_TOPK)]
PEER_NCAND = 64
assert sum(c for _, c in PEER_CAND) <= PEER_NCAND
PEER_NORANK = 64.0


def _oddeven_merge_sort(n):
    pairs, p = [], 1
    while p < n:
        k = p
        while k >= 1:
            for j in range(k % p, n - k, 2 * k):
                for i in range(min(k, n - j - k)):
                    if (i + j) // (2 * p) == (i + j + k) // (2 * p):
                        pairs.append((i + j, i + j + k))
            k //= 2
        p *= 2
    return pairs


_SORT16 = _oddeven_merge_sort(PEER_NKEYS // 8)
_SORT_CAND = _oddeven_merge_sort(PEER_NCAND // 8)


def _walk_best(x, network, count, emit):
    rows = [x[8 * k:8 * k + 8, :] for k in range(x.shape[0] // 8)]
    depth = len(rows)
    rows.append(jnp.full(rows[0].shape, -jnp.inf, F32))
    for i, j in network:
        rows[i], rows[j] = jnp.maximum(rows[i], rows[j]), jnp.minimum(rows[i], rows[j])
    for r in range(count):
        m = jnp.max(rows[0], axis=0, keepdims=True)
        emit(r, m)
        hit = rows[0] == m
        for i in range(min(depth, count - 1 - r)):
            rows[i] = jnp.where(hit, rows[i + 1], rows[i])


def _peer_topk_body(x_ref, g_ref, wq_ref, kt_ref, xn_ref, pkf_ref, pkb_ref,
                    st_ref, t1_ref, t2_ref, cand_ref):
    x = x_ref[...]
    xn = (x * lax.rsqrt(jnp.mean(x * x, axis=-1, keepdims=True) + EPS) * g_ref[...]).astype(BF16)
    xn_ref[...] = xn
    qry = jnp.dot(xn, wq_ref[...], preferred_element_type=F32).astype(BF16)
    st_ref[...] = lax.dot_general(kt_ref[...], qry, (((1,), (1,)), ((), ())),
                                  preferred_element_type=F32)
    n = PEER_NKEYS

    def store_rows(t_ref):
        def emit(r, m):
            t_ref[r:r + 1, :] = m
        return emit

    for h in range(PEER_HEADS):
        s1 = st_ref[(2 * h) * n:(2 * h + 1) * n, :]
        s2 = st_ref[(2 * h + 1) * n:(2 * h + 2) * n, :]
        _walk_best(s1, _SORT16, PEER_TOPK, store_rows(t1_ref))
        _walk_best(s2, _SORT16, PEER_TOPK, store_rows(t2_ref))
        rank2 = jnp.full(s2.shape, PEER_NORANK, F32)
        for r in range(PEER_TOPK):
            rank2 = jnp.where(s2 == t2_ref[r:r + 1, :], float(r), rank2)
        cand_ref[...] = jnp.full(cand_ref.shape, -jnp.inf, F32)
        rowp = 0
        for i, cnt in PEER_CAND:
            cand_ref[rowp:rowp + cnt, :] = t1_ref[i:i + 1, :] + t2_ref[0:cnt, :]
            rowp += cnt
        top1 = t1_ref[0:1, :]
        top2 = t2_ref[0:1, :]
        cmax = top1 + top2
        z = jnp.zeros_like(cmax)
        best = []
        _walk_best(cand_ref[...], _SORT_CAND, PEER_TOPK, lambda r, m: best.append(m))
        for m in best:
            z = z + jnp.exp(m - cmax)
        kth = best[-1]
        cnt = jnp.zeros(s1.shape, F32)
        for j in range(PEER_TOPK // 2):
            cnt = cnt + jnp.where(s1 + t2_ref[j:j + 1, :] >= kth, 1.0, 0.0)
        cnt_best = jnp.zeros_like(top1)
        for j in range(PEER_TOPK):
            cnt_best = cnt_best + jnp.where(top1 + t2_ref[j:j + 1, :] >= kth, 1.0, 0.0)
        cnt = jnp.where(s1 == top1, cnt_best, cnt)
        pkf_ref[h, 0] = jnp.exp(s1 - top1) / z
        pkf_ref[h, 1] = cnt
        pkb_ref[h, 0] = rank2.astype(BF16)
        pkb_ref[h, 1] = jnp.exp(s2 - top2).astype(BF16)


def _peer_topk(h, gain, wq, keys_t, *, tm=256):
    T, D = h.shape
    R = keys_t.shape[0]
    tm = min(tm, T)
    blk = lambda i: (0, 0, 0, i)
    shape = (PEER_HEADS, 2, PEER_NKEYS, T)
    return pl.pallas_call(
        _peer_topk_body,
        grid=(T // tm,),
        in_specs=[pl.BlockSpec((tm, D), lambda i: (i, 0)),
                  pl.BlockSpec((1, D), lambda i: (0, 0)),
                  pl.BlockSpec(wq.shape, lambda i: (0, 0)),
                  pl.BlockSpec(keys_t.shape, lambda i: (0, 0))],
        out_specs=[pl.BlockSpec((tm, D), lambda i: (i, 0))]
        + [pl.BlockSpec((PEER_HEADS, 2, PEER_NKEYS, tm), blk)] * 2,
        out_shape=[jax.ShapeDtypeStruct((T, D), BF16),
                   jax.ShapeDtypeStruct(shape, F32), jax.ShapeDtypeStruct(shape, BF16)],
        scratch_shapes=[pltpu.VMEM((R, tm), F32),
                        pltpu.VMEM((PEER_TOPK + 8, tm), F32),
                        pltpu.VMEM((PEER_TOPK + 8, tm), F32),
                        pltpu.VMEM((PEER_NCAND, tm), F32)],
        compiler_params=_cparams("parallel"),
        name="peer_topk",
    )(h, gain.reshape(1, D).astype(F32), wq, keys_t)


def _peer_dense_body(xn_ref, u_ref, v_ref, pkf_ref, pkb_ref, h_ref, *rest,
                     tm, te, n_e, n_steps, out_norm):
    og_ref = rest[0] if out_norm else None
    o_ref, ht0_ref, ht1_ref, acc_ref = rest[-4:]
    s = pl.program_id(0)
    n = PEER_NKEYS
    group = 2

    def scores(dst_ref):
        dst_ref[...] = lax.dot_general(u_ref[...], xn_ref[...], (((1,), (1,)), ((), ())),
                                       preferred_element_type=F32)

    def scores_half(dst_ref, half, anchor):
        hw = tm // 2
        bits = pltpu.bitcast(anchor[0:16, 0:LANES], jnp.uint32)
        zero = ((bits >> 16) >> 16)[0, 0].astype(jnp.int32)
        xs = xn_ref[pl.ds(pl.multiple_of(half * hw + zero * hw, hw), hw), :]
        dst_ref[:, half * hw:(half + 1) * hw] = lax.dot_general(
            u_ref[...], xs, (((1,), (1,)), ((), ())), preferred_element_type=F32)

    def experts(src_ref, dst_ref=None):
        e = lax.rem(s - 1, n_e)
        n_groups = te // (group * n)
        for gb in range(n_groups):
            if dst_ref is not None and gb in (1, n_groups // 2 + 1):
                scores_half(dst_ref, int(gb > 1), acts[0])
            acts = []
            for ab in range(gb * group, (gb + 1) * group):
                a = e * (te // n) + ab
                gsum = jnp.zeros((n, tm), BF16)
                for h in range(PEER_HEADS):
                    w1 = jnp.broadcast_to(pkf_ref[h, 0, pl.ds(a, 1), :], (n, tm)).astype(BF16)
                    cnt = jnp.broadcast_to(pkf_ref[h, 1, pl.ds(a, 1), :], (n, tm)).astype(BF16)
                    gsum = gsum + jnp.where(pkb_ref[h, 0] < cnt, pkb_ref[h, 1], 0.0) * w1
                hs = src_ref[ab * n:(ab + 1) * n, :]
                act = 0.5 * hs * (1.0 + lax.erf(hs * np.float32(1.0 / np.sqrt(2.0))))
                acts.append(act.astype(BF16) * gsum)
            rows = slice(gb * group * n, (gb + 1) * group * n)
            acc_ref[...] += lax.dot_general(v_ref[rows, :], jnp.concatenate(acts, axis=0),
                                            (((0,), (0,)), ((), ())),
                                            preferred_element_type=F32)

    even = lax.rem(s, 2) == 0
    steady = (s > 0) & (s < n_steps)

    @pl.when((s >= 2) & (lax.rem(s - 1, n_e) == 0))
    def _():
        y = h_ref[...] + acc_ref[...].T
        if out_norm:
            y = y * lax.rsqrt(jnp.mean(y * y, axis=-1, keepdims=True) + EPS) * og_ref[...]
        o_ref[...] = y
        acc_ref[...] = jnp.zeros_like(acc_ref)

    @pl.when(s == 0)
    def _():
        acc_ref[...] = jnp.zeros_like(acc_ref)
        scores(ht0_ref)

    @pl.when(steady & even)
    def _():
        experts(ht1_ref, ht0_ref)

    @pl.when(steady & jnp.logical_not(even))
    def _():
        experts(ht0_ref, ht1_ref)

    @pl.when(s == n_steps)
    def _():
        experts(ht1_ref if n_steps % 2 == 0 else ht0_ref)


def _peer_dense(xn, u, v, layer, pkf, pkb, h, *, out_gain=None, tm=512, te=1024):
    T, D = xn.shape
    E = u.shape[1]
    tm = min(tm, T)
    assert T % tm == 0 and E % te == 0 and te % (2 * PEER_NKEYS) == 0
    n_e = E // te
    n_steps = (T // tm) * n_e
    pair = lambda s, lag: jnp.clip(s - lag, 0, n_steps - 1)
    cur = lambda s: pair(s, 0)
    prev = lambda s: pair(s, 1)
    done = lambda s: pair(s, 2)
    pk_spec = pl.BlockSpec((PEER_HEADS, 2, PEER_NKEYS, tm), lambda s: (0, 0, 0, prev(s) // n_e))
    return pl.pallas_call(
        functools.partial(_peer_dense_body, tm=tm, te=te, n_e=n_e, n_steps=n_steps,
                          out_norm=out_gain is not None),
        grid=(n_steps + 2,),
        in_specs=[pl.BlockSpec((tm, D), lambda s: (cur(s) // n_e, 0)),
                  pl.BlockSpec((None, te, D), lambda s: (layer, cur(s) % n_e, 0)),
                  pl.BlockSpec((None, te, D), lambda s: (layer, prev(s) % n_e, 0)),
                  pk_spec, pk_spec,
                  pl.BlockSpec((tm, D), lambda s: (done(s) // n_e, 0))]
        + ([pl.BlockSpec((1, D), lambda s: (0, 0))] if out_gain is not None else []),
        out_specs=pl.BlockSpec((tm, D), lambda s: (done(s) // n_e, 0)),
        out_shape=jax.ShapeDtypeStruct((T, D), F32),
        scratch_shapes=[pltpu.VMEM((te, tm), F32), pltpu.VMEM((te, tm), F32),
                        pltpu.VMEM((D, tm), F32)],
        compiler_params=_cparams("arbitrary"),
        name="peer_dense",
    )(xn, u, v, pkf, pkb, h,
      *([out_gain.reshape(1, D).astype(F32)] if out_gain is not None else []))


def _peer_ffn(h, gain, wq, keys, u_all, v_all, layer, out_gain=None):
    nk, dh = PEER_NKEYS, PEER_DKEY // 2
    groups = PEER_HEADS * 2
    keys_t = jnp.einsum("gnd,gk->gnkd", keys.reshape(groups, nk, dh).astype(F32),
                        jnp.eye(groups, dtype=F32)).reshape(groups * nk, groups * dh).astype(BF16)
    xn, pkf, pkb = _peer_topk(h, gain, wq.astype(BF16), keys_t)
    return _peer_dense(xn, u_all, v_all, layer, pkf, pkb, h, out_gain=out_gain)


def _lane_tables(T, rot_dim, theta, *, rot_at, keep, scale):
    r = rot_dim // 2
    inv = 1.0 / (theta ** (jnp.arange(0, rot_dim, 2, dtype=F32) / rot_dim))
    lane = np.arange(LANES)
    in_rot = (lane >= rot_at) & (lane < rot_at + rot_dim)
    inv_lane = jnp.where(jnp.asarray(in_rot), inv[np.where(in_rot, (lane - rot_at) % r, 0)], 0.0)
    ang = jnp.arange(T, dtype=F32)[:, None] * inv_lane[None, :]
    c = jnp.cos(ang) * jnp.asarray((lane < keep) * scale, F32)[None, :]
    s = jnp.sin(ang) * jnp.asarray(in_rot * scale, F32)[None, :]
    return c, s


def _head_cols(n_heads, src_stride, src_off, width, *, dst_stride=LANES, dst_off=0):
    idx = np.zeros(n_heads * dst_stride, np.int32)
    sgn = np.zeros(n_heads * dst_stride, np.float32)
    for h in range(n_heads):
        d = h * dst_stride + dst_off
        idx[d:d + width] = h * src_stride + src_off + np.arange(width)
        sgn[d:d + width] = 1.0
    return idx, sgn


def _rot_cols(n_heads, src_stride, src_off, r, *, dst_stride=LANES, dst_off=0):
    idx = np.zeros(n_heads * dst_stride, np.int32)
    sgn = np.zeros(n_heads * dst_stride, np.float32)
    for h in range(n_heads):
        d = h * dst_stride + dst_off
        s = h * src_stride + src_off
        idx[d:d + r] = s + r + np.arange(r)
        sgn[d:d + r] = -1.0
        idx[d + r:d + 2 * r] = s + np.arange(r)
        sgn[d + r:d + 2 * r] = 1.0
    return idx, sgn


def _take_cols(w, idx_sgn):
    idx, sgn = idx_sgn
    return (jnp.take(w, jnp.asarray(idx), axis=1) * jnp.asarray(sgn)[None, :]).astype(BF16)


def _even_mixer(h, norm_g, w_in, ret_gn, w_o):
    T = h.shape[0]
    mw = MOBA_HEADS * MOBA_HEAD_DIM
    rw = RET_HEADS * RET_DK
    vw = RET_HEADS * RET_DV
    o_mq, o_mk, o_mv, o_rq, o_rk, o_rv, o_rg = np.cumsum([0, mw, mw, mw, rw, rw, vw])

    def heads128(off, heads, stride, scale=1.0):
        return _take_cols(w_in[:, off:off + heads * stride] * scale,
                          _head_cols(heads, stride, 0, stride))

    seg_w = MOBA_HEADS * LANES
    cat = lambda a, b: jnp.concatenate([a, b], axis=1)
    cos, sin = _lane_tables(T, MOBA_ROT, ROPE_THETA, rot_at=0, keep=MOBA_HEAD_DIM, scale=1.0)
    qk_m, colmean = _proj(
        h, cat(heads128(o_mq, MOBA_HEADS, MOBA_HEAD_DIM, MOBA_HEAD_DIM ** -0.5 * LOG2E),
               heads128(o_mk, MOBA_HEADS, MOBA_HEAD_DIM)),
        gain=norm_g, rot=(0, MOBA_ROT // 2), cos=cos, sin=sin, seg=2 * seg_w, colmean=MOBA_BLOCK)
    cos, sin = _lane_tables(T, RET_DK, RET_THETA, rot_at=0, keep=RET_DK, scale=1.0)
    qk_r = _proj(h, cat(heads128(o_rq, RET_HEADS, RET_DK),
                        heads128(o_rk, RET_HEADS, RET_DK, RET_DK ** -0.5)),
                 gain=norm_g, rot=(0, RET_DK // 2), cos=cos, sin=sin, seg=2 * seg_w)
    w_plain = jnp.concatenate(
        [_take_cols(w_in[:, o_mv:o_rq], _head_cols(MOBA_HEADS, MOBA_HEAD_DIM, 0, MOBA_HEAD_DIM)),
         w_in[:, o_rv:].astype(BF16)], axis=1)
    plain = _proj(h, w_plain, gain=norm_g)
    nb = T // MOBA_BLOCK
    km = colmean[:, 0, seg_w:2 * seg_w].reshape(nb, MOBA_HEADS, LANES).transpose(1, 0, 2)
    km = jnp.pad(km, ((0, 0), (MOBA_HEAD_DIM, LANES - MOBA_HEAD_DIM - nb), (0, 0)))
    mq_b, mk_b, mv_b = _moba_gate(qk_m, plain, km, MOBA_HEADS)
    a_out = _flash(mq_b, mk_b, mv_b, MOBA_HEADS, l_lane=MOBA_HEAD_DIM)
    b_out = _retention(qk_r, plain, ret_gn, RET_HEADS, q_off=0, k_off=RET_HEADS,
                       v_off=MOBA_HEADS, g_off=MOBA_HEADS + RET_HEADS)
    return _matmul_res([a_out, b_out], [w_o[:mw].astype(BF16), w_o[mw:].astype(BF16)], h)


def _odd_mixer(h, norm_g, w_down, q_norm, w_uq, kv_norm, w_ukv, w_o):
    T = h.shape[0]
    lat = MLA_Q_RANK + MLA_KV_RANK
    half = MLA_ROPE // 2
    dq = MLA_NOPE + MLA_ROPE
    kw = MLA_HEADS * LANES
    w_dn = jnp.concatenate(
        [w_down[:, :lat].astype(BF16),
         _take_cols(w_down[:, lat:], _head_cols(1, MLA_ROPE, 0, MLA_ROPE)),
         _take_cols(w_down[:, lat:], _rot_cols(1, MLA_ROPE, 0, half))], axis=1)
    down = _proj(h, w_dn, gain=norm_g, out_dtype=F32)
    ck, sk = _lane_tables(T, MLA_ROPE, ROPE_THETA, rot_at=0, keep=MLA_ROPE, scale=1.0)
    cqn, ckvx = _mla_mid(down, q_norm, kv_norm, ck, sk)
    cq_t, sq_t = _lane_tables(T, MLA_ROPE, ROPE_THETA, rot_at=MLA_NOPE, keep=dq,
                              scale=dq ** -0.5 * LOG2E)
    q = _proj(cqn, _take_cols(w_uq, _head_cols(MLA_HEADS, dq, 0, dq)),
              rot=(MLA_NOPE, half), cos=cq_t, sin=sq_t, seg=kw)
    kvw = MLA_NOPE + MLA_V
    place_k = np.zeros((LANES, kw), np.float32)
    place_v = np.zeros((LANES, kw), np.float32)
    for hh in range(MLA_HEADS):
        place_k[np.arange(MLA_ROPE), hh * LANES + MLA_NOPE + np.arange(MLA_ROPE)] = 1.0
        place_v[MLA_ROPE, hh * LANES + MLA_V] = 1.0
    wk = jnp.concatenate([_take_cols(w_ukv, _head_cols(MLA_HEADS, kvw, 0, MLA_NOPE)),
                          jnp.asarray(place_k, BF16)], axis=0)
    wv = jnp.concatenate([_take_cols(w_ukv, _head_cols(MLA_HEADS, kvw, MLA_NOPE, MLA_V)),
                          jnp.asarray(place_v, BF16)], axis=0)
    kv = _proj(ckvx, jnp.concatenate([wk, wv], axis=1))
    o = _flash(q, kv, kv, MLA_HEADS, v_off=MLA_HEADS // 2, l_lane=MLA_V)
    return _matmul_res([o], [w_o.astype(BF16)], h)


def kernel(x, attn_norm, ffn_norm, ev_w_in, ev_ret_gn, ev_w_o, od_w_down, od_q_norm, od_w_uq,
           od_kv_norm, od_w_ukv, od_w_o, peer_wq, peer_keys, peer_u, peer_v, final_norm):
    B, S, D = x.shape
    assert B == 1
    h = x.reshape(S, D)
    depth = attn_norm.shape[0]
    u_all, v_all = peer_u.astype(BF16), peer_v.astype(BF16)
    for i in range(depth):
        j = i // 2
        if i % 2 == 0:
            h = _even_mixer(h, attn_norm[i], ev_w_in[j], ev_ret_gn[j], ev_w_o[j])
        else:
            h = _odd_mixer(h, attn_norm[i], od_w_down[j], od_q_norm[j], od_w_uq[j],
                           od_kv_norm[j], od_w_ukv[j], od_w_o[j])
        h = _peer_ffn(h, ffn_norm[i], peer_wq[i], peer_keys[i], u_all, v_all, i,
                      out_gain=final_norm if i == depth - 1 else None)
    return h.reshape(B, S, D)
```
